```python
import math, functools
import jax, jax.numpy as jnp
from jax import lax
import numpy as np

D_MODEL = 1024
BATCH = 1
SEQ = 16384
DEPTH = 1
DEC_BATCH = 128
DEC_SEQ = 8
PAST_LEN = 8192
PAGE_SIZE = 128

ATTN_WIDTH = D_MODEL // 2
SSM_WIDTH = D_MODEL - ATTN_WIDTH
HEAD_DIM = 64
N_HEADS = ATTN_WIDTH // HEAD_DIM
SSM_GROUP = 16
N_SSM_GROUPS = SSM_WIDTH // SSM_GROUP
SSM_STATE = 64
PROJ_WIDTH = 3 * ATTN_WIDTH + SSM_WIDTH
N_EXPERTS = 32
TOP_K = 4
D_FF = D_MODEL
SWIGLU_LIMIT = 7.0
SWIGLU_ALPHA = 1.702
Q_BLOCK = 128
EXPERT_BLOCK = 128
N_MOD = 6
RMS_EPS = 1e-6
DT_MIN = 1e-3
DT_MAX = 1e-1
SB_BIAS_INIT = -5.0

kernel_name = 'hybrid_stickbreak_s5_moe_step'


def rmsnorm(x, g):
    xf = x.astype(jnp.float32)
    y = xf * lax.rsqrt(jnp.mean(xf * xf, axis=-1, keepdims=True) + RMS_EPS)
    return (y * g.astype(jnp.float32)).astype(x.dtype)


def stick_breaking(q, k, v, bias, q_pos, k_pos):
    z = jnp.einsum('bqhd,bkhd->bhqk', q.astype(jnp.float32), k.astype(jnp.float32)) * (HEAD_DIM ** -0.5)
    z = z + bias.astype(jnp.float32)[None, :, None, None]
    causal = (k_pos[None, :] < q_pos[:, None])[None, None]
    log_beta = jax.nn.log_sigmoid(z)
    log_keep = jnp.where(causal, jax.nn.log_sigmoid(-z), 0.0)
    later = lax.cumsum(log_keep, axis=3, reverse=True) - log_keep
    a = jnp.where(causal, jnp.exp(log_beta + later), 0.0)
    o = jnp.einsum('bhqk,bkhd->bqhd', a, v.astype(jnp.float32))
    return o.astype(v.dtype)


def sb_prompt(q, k, v, bias):
    b, t, h, dh = q.shape
    n_blocks = t // Q_BLOCK
    q_blocks = q.reshape(b, n_blocks, Q_BLOCK, h, dh).transpose(1, 0, 2, 3, 4)
    k_pos = jnp.arange(t, dtype=jnp.int32)

    def one_block(args):
        q_blk, i = args
        q_pos = i * Q_BLOCK + jnp.arange(Q_BLOCK, dtype=jnp.int32)
        return stick_breaking(q_blk, k, v, bias, q_pos, k_pos)

    o = lax.map(one_block, (q_blocks, jnp.arange(n_blocks, dtype=jnp.int32)))
    return o.transpose(1, 0, 2, 3, 4).reshape(b, t, h, dh)


def sb_sample(q, k, v, bias, k_past, v_past):
    past = k_past.shape[1]
    t = q.shape[1]
    k_all = jnp.concatenate([k_past.astype(k.dtype), k], axis=1)
    v_all = jnp.concatenate([v_past.astype(v.dtype), v], axis=1)
    k_pos = jnp.arange(past + t, dtype=jnp.int32)
    q_pos = past + jnp.arange(t, dtype=jnp.int32)
    return stick_breaking(q, k_all, v_all, bias, q_pos, k_pos)


def _ssm_combine(left, right):
    a_l, b_l = left
    a_r, b_r = right
    return a_r * a_l, a_r * b_l + b_r


def s5_mixer(u, h0_re, h0_im, lam_re, lam_im, log_dt, b_re, b_im, c_re, c_im, d_skip, w_glu, b_glu):
    bsz, t, _ = u.shape
    f32 = jnp.float32
    uf = u.astype(f32).reshape(bsz, t, N_SSM_GROUPS, SSM_GROUP)
    lam = lax.complex(lam_re.astype(f32), lam_im.astype(f32))
    dt = jnp.exp(log_dt.astype(f32))[:, None]
    lam_bar = jnp.exp(lam * dt)
    b_mat = lax.complex(b_re.astype(f32), b_im.astype(f32))
    c_mat = lax.complex(c_re.astype(f32), c_im.astype(f32))
    b_bar = ((lam_bar - 1.0) / lam)[..., None] * b_mat
    bu = jnp.einsum('gph,btgh->btgp', b_bar, uf.astype(jnp.complex64))
    h0 = lax.complex(h0_re.astype(f32), h0_im.astype(f32))
    bu = bu.at[:, 0].add(lam_bar[None] * h0)
    a = jnp.broadcast_to(lam_bar, bu.shape)
    _, h = lax.associative_scan(_ssm_combine, (a, bu), axis=1)
    y = jnp.einsum('ghp,btgp->btgh', c_mat, h).real + d_skip.astype(f32) * uf
    g = jax.nn.gelu(y, approximate=False).reshape(bsz, t, SSM_WIDTH)
    out = g * jax.nn.sigmoid(g @ w_glu.astype(f32) + b_glu.astype(f32))
    h_last = h[:, -1]
    return out.astype(u.dtype), h_last.real, h_last.imag


def moe_ffn(h, w_router, b_router, w_gu, b_gu, w_down, b_down):
    bsz, t, d = h.shape
    xt = h.reshape(-1, d)
    n_tok = xt.shape[0]
    logits = (xt @ w_router + b_router).astype(jnp.float32)
    top_val, top_idx = lax.top_k(logits, TOP_K)
    gates = jax.nn.softmax(top_val, axis=-1)
    n_assign = n_tok * TOP_K
    flat_e = top_idx.reshape(-1).astype(jnp.int32)
    order = jnp.argsort(flat_e).astype(jnp.int32)
    sorted_e = flat_e[order]
    counts = jnp.zeros((N_EXPERTS,), jnp.int32).at[flat_e].add(1)
    padded = ((counts + EXPERT_BLOCK - 1) // EXPERT_BLOCK) * EXPERT_BLOCK
    start = jnp.cumsum(counts) - counts
    pend = jnp.cumsum(padded)
    pstart = pend - padded
    dest = pstart[sorted_e] + (jnp.arange(n_assign, dtype=jnp.int32) - start[sorted_e])
    n_rows = -(-n_assign // EXPERT_BLOCK) * EXPERT_BLOCK + N_EXPERTS * EXPERT_BLOCK
    n_blocks = n_rows // EXPERT_BLOCK
    row_tok = jnp.full((n_rows,), n_tok, jnp.int32).at[dest].set(order // TOP_K)
    assign_dest = jnp.zeros((n_assign,), jnp.int32).at[order].set(dest)
    block_starts = jnp.arange(n_blocks, dtype=jnp.int32) * EXPERT_BLOCK
    block_e = jnp.minimum(jnp.searchsorted(pend, block_starts, side='right'), N_EXPERTS - 1).astype(jnp.int32)
    x_pad = jnp.concatenate([xt, jnp.zeros((1, d), xt.dtype)], axis=0)
    xr = x_pad[row_tok].reshape(n_blocks, EXPERT_BLOCK, d)

    def expert_block(args):
        xb, e = args
        gu = xb @ w_gu[e] + b_gu[e]
        gate, up = gu[:, :D_FF], gu[:, D_FF:]
        gate = jnp.minimum(gate, SWIGLU_LIMIT)
        up = jnp.clip(up, -SWIGLU_LIMIT, SWIGLU_LIMIT)
        act = (up + 1.0) * (gate * jax.nn.sigmoid(SWIGLU_ALPHA * gate))
        return act @ w_down[e] + b_down[e]

    yr = lax.map(expert_block, (xr, block_e)).reshape(n_rows, d)
    y_assign = yr[assign_dest].reshape(n_tok, TOP_K, d)
    y = jnp.einsum('tk,tkd->td', gates.astype(h.dtype), y_assign)
    return y.reshape(bsz, t, d)


def layer(x, c, h0_re, h0_im, attend, norm1_g, norm2_g, w_ada, b_ada, w_in, w_out, sb_bias,
          lam_re, lam_im, log_dt, ssm_b_re, ssm_b_im, ssm_c_re, ssm_c_im, ssm_d, w_glu, b_glu,
          w_router, b_router, w_gu, b_gu, w_down, b_down):
    bsz, t, _ = x.shape
    mod = (c @ w_ada + b_ada)[:, None, :]
    shift1, scale1, gate1, shift2, scale2, gate2 = jnp.split(mod, N_MOD, axis=-1)
    hn = rmsnorm(x, norm1_g) * (1.0 + scale1) + shift1
    proj = hn @ w_in
    q, k, v, u = jnp.split(proj, [ATTN_WIDTH, 2 * ATTN_WIDTH, 3 * ATTN_WIDTH], axis=-1)
    q = q.reshape(bsz, t, N_HEADS, HEAD_DIM)
    k = k.reshape(bsz, t, N_HEADS, HEAD_DIM)
    v = v.reshape(bsz, t, N_HEADS, HEAD_DIM)
    o_attn = attend(q, k, v, sb_bias).reshape(bsz, t, ATTN_WIDTH)
    y_ssm, h_re, h_im = s5_mixer(u, h0_re, h0_im, lam_re, lam_im, log_dt, ssm_b_re, ssm_b_im,
                                 ssm_c_re, ssm_c_im, ssm_d, w_glu, b_glu)
    mix = jnp.concatenate([o_attn, y_ssm], axis=-1) @ w_out
    x = x + gate1 * mix
    hn2 = rmsnorm(x, norm2_g) * (1.0 + scale2) + shift2
    x = x + gate2 * moe_ffn(hn2, w_router, b_router, w_gu, b_gu, w_down, b_down)
    return x, k, v, h_re, h_im


def setup_inputs(seed: int = 0) -> dict:
    key = jax.random.key(seed)
    ks = iter(jax.random.split(key, 40))
    f32 = jnp.float32

    def nrm(shape, scale):
        return jax.random.normal(next(ks), shape, f32) * scale

    n_pages = PAST_LEN // PAGE_SIZE
    n_used = DEC_BATCH * n_pages
    n_pool = (n_used * 5) // 4
    x_prompt = nrm((BATCH, SEQ, D_MODEL), 1.0)
    x_sample = nrm((DEC_BATCH, DEC_SEQ, D_MODEL), 1.0)
    c_prompt = nrm((BATCH, D_MODEL), 1.0)
    c_sample = nrm((DEC_BATCH, D_MODEL), 1.0)
    cache_k = nrm((DEPTH, n_pool, PAGE_SIZE, N_HEADS, HEAD_DIM), 1.0)
    cache_v = nrm((DEPTH, n_pool, PAGE_SIZE, N_HEADS, HEAD_DIM), 1.0)
    state_ssm_re = nrm((DEPTH, DEC_BATCH, N_SSM_GROUPS, SSM_STATE), 0.3)
    state_ssm_im = nrm((DEPTH, DEC_BATCH, N_SSM_GROUPS, SSM_STATE), 0.3)
    page_table = jax.random.permutation(next(ks), n_pool)[:n_used].reshape(DEC_BATCH, n_pages).astype(jnp.int32)
    norm1_g = 1.0 + nrm((DEPTH, D_MODEL), 0.02)
    norm2_g = 1.0 + nrm((DEPTH, D_MODEL), 0.02)
    w_ada = nrm((DEPTH, D_MODEL, N_MOD * D_MODEL), 0.5 * D_MODEL ** -0.5)
    b_ada = nrm((DEPTH, N_MOD * D_MODEL), 0.02)
    w_in = nrm((DEPTH, D_MODEL, PROJ_WIDTH), D_MODEL ** -0.5)
    w_out = nrm((DEPTH, ATTN_WIDTH + SSM_WIDTH, D_MODEL), (ATTN_WIDTH + SSM_WIDTH) ** -0.5)
    sb_bias = SB_BIAS_INIT + nrm((DEPTH, N_HEADS), 0.1)
    n_idx = jnp.arange(SSM_STATE, dtype=f32)
    lam_re = -0.5 + nrm((DEPTH, N_SSM_GROUPS, SSM_STATE), 0.01)
    lam_im = math.pi * n_idx + nrm((DEPTH, N_SSM_GROUPS, SSM_STATE), 0.01)
    log_dt = jax.random.uniform(next(ks), (DEPTH, N_SSM_GROUPS), f32, math.log(DT_MIN), math.log(DT_MAX))
    ssm_b_re = nrm((DEPTH, N_SSM_GROUPS, SSM_STATE, SSM_GROUP), (2.0 * SSM_GROUP) ** -0.5)
    ssm_b_im = nrm((DEPTH, N_SSM_GROUPS, SSM_STATE, SSM_GROUP), (2.0 * SSM_GROUP) ** -0.5)
    ssm_c_re = nrm((DEPTH, N_SSM_GROUPS, SSM_GROUP, SSM_STATE), (2.0 * SSM_STATE) ** -0.5)
    ssm_c_im = nrm((DEPTH, N_SSM_GROUPS, SSM_GROUP, SSM_STATE), (2.0 * SSM_STATE) ** -0.5)
    ssm_d = nrm((DEPTH, N_SSM_GROUPS, SSM_GROUP), 0.5)
    w_glu = nrm((DEPTH, SSM_WIDTH, SSM_WIDTH), SSM_WIDTH ** -0.5)
    b_glu = nrm((DEPTH, SSM_WIDTH), 0.02)
    w_router = nrm((DEPTH, D_MODEL, N_EXPERTS), D_MODEL ** -0.5)
    b_router = nrm((DEPTH, N_EXPERTS), 0.01)
    w_gu = nrm((DEPTH, N_EXPERTS, D_MODEL, 2 * D_FF), D_MODEL ** -0.5)
    b_gu = nrm((DEPTH, N_EXPERTS, 2 * D_FF), 0.02)
    w_down = nrm((DEPTH, N_EXPERTS, D_FF, D_MODEL), D_FF ** -0.5)
    b_down = nrm((DEPTH, N_EXPERTS, D_MODEL), 0.02)
    norm_f_g = 1.0 + nrm((D_MODEL,), 0.02)
    return {'x_prompt': x_prompt, 'x_sample': x_sample, 'c_prompt': c_prompt, 'c_sample': c_sample,
            'cache_k': cache_k, 'cache_v': cache_v, 'state_ssm_re': state_ssm_re, 'state_ssm_im': state_ssm_im,
            'page_table': page_table, 'norm1_g': norm1_g, 'norm2_g': norm2_g, 'w_ada': w_ada, 'b_ada': b_ada,
            'w_in': w_in, 'w_out': w_out, 'sb_bias': sb_bias, 'lam_re': lam_re, 'lam_im': lam_im, 'log_dt': log_dt,
            'ssm_b_re': ssm_b_re, 'ssm_b_im': ssm_b_im, 'ssm_c_re': ssm_c_re, 'ssm_c_im': ssm_c_im,
            'ssm_d': ssm_d, 'w_glu': w_glu, 'b_glu': b_glu, 'w_router': w_router, 'b_router': b_router,
            'w_gu': w_gu, 'b_gu': b_gu, 'w_down': w_down, 'b_down': b_down, 'norm_f_g': norm_f_g}


def reference(x_prompt, x_sample, c_prompt, c_sample, cache_k, cache_v, state_ssm_re, state_ssm_im,
              page_table, norm1_g, norm2_g, w_ada, b_ada, w_in, w_out, sb_bias, lam_re, lam_im, log_dt,
              ssm_b_re, ssm_b_im, ssm_c_re, ssm_c_im, ssm_d, w_glu, b_glu, w_router, b_router,
              w_gu, b_gu, w_down, b_down, norm_f_g):
    dec_b, n_pages = page_table.shape
    past = n_pages * cache_k.shape[2]
    h0_prompt = jnp.zeros((x_prompt.shape[0], N_SSM_GROUPS, SSM_STATE), jnp.float32)
    xp, xs = x_prompt, x_sample
    kp_l, vp_l, rp_l, ip_l, ks_l, vs_l, rs_l, is_l = [], [], [], [], [], [], [], []
    for l in range(DEPTH):
        lw = (norm1_g[l], norm2_g[l], w_ada[l], b_ada[l], w_in[l], w_out[l], sb_bias[l], lam_re[l], lam_im[l],
              log_dt[l], ssm_b_re[l], ssm_b_im[l], ssm_c_re[l], ssm_c_im[l], ssm_d[l], w_glu[l], b_glu[l],
              w_router[l], b_router[l], w_gu[l], b_gu[l], w_down[l], b_down[l])
        xp, kp, vp, rp, ip = layer(xp, c_prompt, h0_prompt, h0_prompt, sb_prompt, *lw)
        k_past = cache_k[l][page_table].reshape(dec_b, past, N_HEADS, HEAD_DIM)
        v_past = cache_v[l][page_table].reshape(dec_b, past, N_HEADS, HEAD_DIM)
        attend_s = functools.partial(sb_sample, k_past=k_past, v_past=v_past)
        xs, ks_, vs_, rs, is_ = layer(xs, c_sample, state_ssm_re[l], state_ssm_im[l], attend_s, *lw)
        kp_l.append(kp); vp_l.append(vp)
        rp_l.append(rp.astype(state_ssm_re.dtype)); ip_l.append(ip.astype(state_ssm_im.dtype))
        ks_l.append(ks_); vs_l.append(vs_)
        rs_l.append(rs.astype(state_ssm_re.dtype)); is_l.append(is_.astype(state_ssm_im.dtype))
    y_prompt = rmsnorm(xp, norm_f_g)
    y_sample = rmsnorm(xs, norm_f_g)
    return (y_prompt, y_sample, jnp.stack(kp_l), jnp.stack(vp_l), jnp.stack(rp_l), jnp.stack(ip_l),
            jnp.stack(ks_l), jnp.stack(vs_l), jnp.stack(rs_l), jnp.stack(is_l))
```

```python
import functools
import math

import jax
import jax.numpy as jnp
from jax import lax
from jax.experimental import pallas as pl
from jax.experimental.pallas import tpu as pltpu

F32 = jnp.float32
BF16 = jnp.bfloat16

HEAD_DIM = 64
SSM_GROUP = 16
TOP_K = 4
SWIGLU_LIMIT = 7.0
SWIGLU_ALPHA = 1.702
RMS_EPS = 1e-6

LANES = 128
SUBLANES = 8
VMEM_LIMIT_BYTES = 56 * 1024 * 1024

KEY_BLOCK = LANES
SSM_CHUNK = 8
GROUP_BLOCK = LANES // SSM_GROUP
ROUTER_PAD = LANES
NEG_BIG = -1e30


def _cparams(semantics):
    return pltpu.CompilerParams(dimension_semantics=semantics, vmem_limit_bytes=VMEM_LIMIT_BYTES)


def _dot(a, b):
    return jnp.dot(a, b, preferred_element_type=F32)


def _dot_nt(a, b):
    return lax.dot_general(a, b, (((1,), (1,)), ((), ())), preferred_element_type=F32)


def _split_bf16(x):
    hi = x.astype(BF16)
    lo = (x - hi.astype(F32)).astype(BF16)
    return hi, lo


def _ada_kernel(c_ref, w_ref, b_ref, o_ref):
    o_ref[...] = _dot(c_ref[...].astype(BF16), w_ref[...].astype(BF16)) + b_ref[...]


def ada_mod(c, w_ada, b_ada):
    n, d = c.shape
    n_out = w_ada.shape[1]
    return pl.pallas_call(
        _ada_kernel,
        grid=(n_out // d,),
        in_specs=[pl.BlockSpec((n, d), lambda j: (0, 0)),
                  pl.BlockSpec((d, d), lambda j: (0, j)),
                  pl.BlockSpec((1, d), lambda j: (0, j))],
        out_specs=pl.BlockSpec((n, d), lambda j: (0, j)),
        out_shape=jax.ShapeDtypeStruct((n, n_out), F32),
        compiler_params=_cparams(("arbitrary",)),
        name="ada_mod",
    )(c, w_ada, b_ada.reshape(1, n_out))


def _inproj_kernel(x_ref, shift_ref, scale_ref, g_ref, w_ref, q_ref, k_ref, v_ref, kb_ref, vb_ref, u_ref, *, aw):
    x = x_ref[...]
    ms = jnp.mean(x * x, axis=-1, keepdims=True)
    hn = x * lax.rsqrt(ms + RMS_EPS) * g_ref[...]
    hn = hn * (1.0 + scale_ref[...]) + shift_ref[...]
    proj = _dot(hn.astype(BF16), w_ref[...])
    q_ref[...] = (proj[:, :aw] * (HEAD_DIM ** -0.5)).astype(q_ref.dtype)
    k = proj[:, aw:2 * aw]
    v = proj[:, 2 * aw:3 * aw]
    k_ref[...] = k
    v_ref[...] = v
    kb_ref[...] = k.astype(BF16)
    vb_ref[...] = v.astype(BF16)
    u_ref[...] = proj[:, 3 * aw:]


def _row_spec(n_mod_rows, rows, d):
    if n_mod_rows == 1:
        return pl.BlockSpec((1, d), lambda i: (0, 0))
    return pl.BlockSpec((rows, d), lambda i: (i, 0))


def inproj(x, shift, scale, g, w_in_b, aw, q_dtype, rows):
    n, d = x.shape
    pw = w_in_b.shape[1]
    sw = pw - 3 * aw
    tok = lambda width: pl.BlockSpec((rows, width), lambda i: (i, 0))
    return pl.pallas_call(
        functools.partial(_inproj_kernel, aw=aw),
        grid=(n // rows,),
        in_specs=[tok(d), _row_spec(shift.shape[0], rows, d), _row_spec(scale.shape[0], rows, d),
                  pl.BlockSpec((1, d), lambda i: (0, 0)),
                  pl.BlockSpec((d, pw), lambda i: (0, 0))],
        out_specs=[tok(aw), tok(aw), tok(aw), tok(aw), tok(aw), tok(sw)],
        out_shape=[jax.ShapeDtypeStruct((n, aw), q_dtype),
                   jax.ShapeDtypeStruct((n, aw), F32), jax.ShapeDtypeStruct((n, aw), F32),
                   jax.ShapeDtypeStruct((n, aw), BF16), jax.ShapeDtypeStruct((n, aw), BF16),
                   jax.ShapeDtypeStruct((n, sw), F32)],
        compiler_params=_cparams(("arbitrary",)),
        name="inproj",
    )(x, shift, scale, g, w_in_b)


def _tri_ones():
    j = lax.broadcasted_iota(jnp.int32, (KEY_BLOCK, 2 * KEY_BLOCK), 0)
    s = lax.broadcasted_iota(jnp.int32, (KEY_BLOCK, 2 * KEY_BLOCK), 1)
    return jnp.where((j > s) | (s >= KEY_BLOCK), 1.0, 0.0).astype(BF16)


def _sb_weights(z, carry, tri, mask):
    lk = -(jnp.maximum(z, 0.0) + jnp.log(1.0 + jnp.exp(-jnp.abs(z))))
    if mask is not None:
        lk = jnp.where(mask, lk, 0.0)
    hi, lo = _split_bf16(lk)
    cs = _dot(hi, tri) + _dot(lo, tri)
    later = cs[:, :KEY_BLOCK] + carry
    a = jnp.exp(lk + z + later)
    if mask is not None:
        a = jnp.where(mask, a, 0.0)
    return a.astype(BF16), cs[:, KEY_BLOCK:]


def _attn_prompt_kernel(bias_ref, q_ref, k_ref, v_ref, tri_ref, o_ref, acc_ref, carry_ref):
    hp = pl.program_id(0)
    i = pl.program_id(1)
    tq = q_ref.shape[0]
    q = q_ref[...]
    lane = lax.broadcasted_iota(jnp.int32, (tq, LANES), 1)
    row = lax.broadcasted_iota(jnp.int32, (tq, LANES), 0)
    zero = jnp.zeros_like(q)
    qm = (jnp.where(lane < HEAD_DIM, q, zero), jnp.where(lane >= HEAD_DIM, q, zero))
    bias = (bias_ref[2 * hp], bias_ref[2 * hp + 1])
    tri = tri_ref[...]

    start = pl.multiple_of(i * KEY_BLOCK, KEY_BLOCK)
    kblk = k_ref[pl.ds(start, KEY_BLOCK), :]
    vblk = v_ref[pl.ds(start, KEY_BLOCK), :]
    causal = lane < row
    for h in range(2):
        z = _dot_nt(qm[h], kblk) + bias[h]
        a, tot = _sb_weights(z, 0.0, tri, causal)
        acc_ref[h] = _dot(a, vblk)
        carry_ref[h] = tot

    def body(jj, c):
        st = pl.multiple_of((i - 1 - jj) * KEY_BLOCK, KEY_BLOCK)
        kb = k_ref[pl.ds(st, KEY_BLOCK), :]
        vb = v_ref[pl.ds(st, KEY_BLOCK), :]
        for h in range(2):
            z = _dot_nt(qm[h], kb) + bias[h]
            a, tot = _sb_weights(z, carry_ref[h], tri, None)
            acc_ref[h] += _dot(a, vb)
            carry_ref[h] += tot
        return c

    lax.fori_loop(0, i, body, 0)
    o_ref[...] = jnp.where(lane < HEAD_DIM, acc_ref[0], acc_ref[1]).astype(o_ref.dtype)


def attn_prompt(q, kb, vb, bias):
    n, aw = q.shape
    tq = KEY_BLOCK
    return pl.pallas_call(
        _attn_prompt_kernel,
        grid=(aw // LANES, n // tq),
        in_specs=[pl.BlockSpec(memory_space=pltpu.SMEM),
                  pl.BlockSpec((tq, LANES), lambda hp, i: (i, hp)),
                  pl.BlockSpec((n, LANES), lambda hp, i: (0, hp)),
                  pl.BlockSpec((n, LANES), lambda hp, i: (0, hp)),
                  pl.BlockSpec((KEY_BLOCK, 2 * KEY_BLOCK), lambda hp, i: (0, 0))],
        out_specs=pl.BlockSpec((tq, LANES), lambda hp, i: (i, hp)),
        out_shape=jax.ShapeDtypeStruct((n, aw), BF16),
        scratch_shapes=[pltpu.VMEM((2, tq, LANES), F32), pltpu.VMEM((2, tq, LANES), F32)],
        compiler_params=_cparams(("arbitrary", "arbitrary")),
        name="attn_prompt",
    )(bias, q, kb, vb, _tri_ones())


def _attn_sample_kernel(pt_ref, q_ref, kn_ref, vn_ref, *rest, pages_per_step, n_heads, t_new):
    del pt_ref
    k_refs = rest[:pages_per_step]
    v_refs = rest[pages_per_step:2 * pages_per_step]
    tri_ref, bias_ref, o_ref, qbd_ref, acc_ref, carry_ref = rest[2 * pages_per_step:]
    p = pl.program_id(1)
    m = n_heads * t_new
    aw = n_heads * HEAD_DIM
    tri = tri_ref[...]
    bias = bias_ref[...]

    @pl.when(p == 0)
    def _():
        q = q_ref[...].astype(BF16)
        qt = jnp.concatenate([q] * n_heads, axis=0)
        rw = lax.broadcasted_iota(jnp.int32, (m, aw), 0)
        ln = lax.broadcasted_iota(jnp.int32, (m, aw), 1)
        qbd = jnp.where(ln // HEAD_DIM == rw // t_new, qt, jnp.zeros_like(qt))
        qbd_ref[...] = qbd
        pad = jnp.zeros((KEY_BLOCK - t_new, aw), F32)
        kn = jnp.concatenate([kn_ref[...], pad], axis=0).astype(BF16)
        vn = jnp.concatenate([vn_ref[...], pad], axis=0).astype(BF16)
        r2 = lax.broadcasted_iota(jnp.int32, (m, KEY_BLOCK), 0)
        l2 = lax.broadcasted_iota(jnp.int32, (m, KEY_BLOCK), 1)
        mask = l2 < (r2 % t_new)
        z = _dot_nt(qbd, kn) + bias
        a, tot = _sb_weights(z, 0.0, tri, mask)
        acc_ref[...] = _dot(a, vn)
        carry_ref[...] = tot

    qbd = qbd_ref[...]
    for i in range(pages_per_step):
        kt = k_refs[i][0].astype(BF16)
        vt = v_refs[i][0].astype(BF16)
        z = _dot(qbd, kt) + bias
        a, tot = _sb_weights(z, carry_ref[...], tri, None)
        acc_ref[...] += _dot_nt(a, vt)
        carry_ref[...] += tot

    @pl.when(p == pl.num_programs(1) - 1)
    def _():
        acc = acc_ref[...]
        ln = lax.broadcasted_iota(jnp.int32, (t_new, aw), 1)
        o = jnp.zeros((t_new, aw), F32)
        for h in range(n_heads):
            o = o + jnp.where(ln // HEAD_DIM == h, acc[h * t_new:(h + 1) * t_new, :], 0.0)
        o_ref[...] = o.astype(o_ref.dtype)


def attn_sample(q, k_new, v_new, cache_k, cache_v, page_table, bias, t_new, pages_per_step):
    n, aw = q.shape
    bsz, n_pages = page_table.shape
    n_heads = aw // HEAD_DIM
    page = cache_k.shape[2]
    assert page == KEY_BLOCK and t_new == SUBLANES and n_pages % pages_per_step == 0
    m = n_heads * t_new
    bias_rows = jnp.broadcast_to(jnp.repeat(bias, t_new)[:, None], (m, KEY_BLOCK)).astype(F32)

    def page_spec(i):
        return pl.BlockSpec((1, aw, page),
                            lambda b, p, pt: (pt[b, n_pages - 1 - (p * pages_per_step + i)], 0, 0))

    tok = pl.BlockSpec((t_new, aw), lambda b, p, pt: (b, 0))
    grid_spec = pltpu.PrefetchScalarGridSpec(
        num_scalar_prefetch=1,
        grid=(bsz, n_pages // pages_per_step),
        in_specs=[tok, tok, tok]
        + [page_spec(i) for i in range(pages_per_step)]
        + [page_spec(i) for i in range(pages_per_step)]
        + [pl.BlockSpec((KEY_BLOCK, 2 * KEY_BLOCK), lambda b, p, pt: (0, 0)),
           pl.BlockSpec((m, KEY_BLOCK), lambda b, p, pt: (0, 0))],
        out_specs=tok,
        scratch_shapes=[pltpu.VMEM((m, aw), BF16), pltpu.VMEM((m, aw), F32), pltpu.VMEM((m, KEY_BLOCK), F32)],
    )
    return pl.pallas_call(
        functools.partial(_attn_sample_kernel, pages_per_step=pages_per_step, n_heads=n_heads, t_new=t_new),
        grid_spec=grid_spec,
        out_shape=jax.ShapeDtypeStruct((n, aw), BF16),
        compiler_params=_cparams(("arbitrary", "arbitrary")),
        name="attn_sample",
    )(page_table, q, k_new, v_new, *([cache_k] * pages_per_step), *([cache_v] * pages_per_step),
      _tri_ones(), bias_rows)


def _ssm_mats(lam_re, lam_im, log_dt, b_re, b_im, c_re, c_im):
    g, p = lam_re.shape
    hh = b_re.shape[-1]
    el = SSM_CHUNK
    nb = g // GROUP_BLOCK
    dt = jnp.exp(log_dt)[:, None]
    ar, ai = lam_re * dt, lam_im * dt
    lbr, lbi = jnp.exp(ar) * jnp.cos(ai), jnp.exp(ar) * jnp.sin(ai)
    den = lam_re * lam_re + lam_im * lam_im
    fr = ((lbr - 1.0) * lam_re + lbi * lam_im) / den
    fi = (lbi * lam_re - (lbr - 1.0) * lam_im) / den
    bbr = fr[..., None] * b_re - fi[..., None] * b_im
    bbi = fr[..., None] * b_im + fi[..., None] * b_re
    n = jnp.arange(el + 1, dtype=F32)[:, None, None]
    pr = jnp.exp(ar[None] * n) * jnp.cos(ai[None] * n)
    pi = jnp.exp(ar[None] * n) * jnp.sin(ai[None] * n)
    eye = jnp.eye(GROUP_BLOCK, dtype=F32)

    cpr = c_re[None] * pr[:el, :, None, :] - c_im[None] * pi[:el, :, None, :]
    cpi = c_re[None] * pi[:el, :, None, :] + c_im[None] * pr[:el, :, None, :]
    kd = jnp.einsum('dgip,gpj->dgij', cpr, bbr) - jnp.einsum('dgip,gpj->dgij', cpi, bbi)
    kmat = jnp.einsum('dkgij,gh->dkgjhi', kd.reshape(el, nb, GROUP_BLOCK, hh, hh), eye)
    kmat = kmat.reshape(el, nb, LANES, LANES)

    rev = pr[el - 1 - jnp.arange(el)], pi[el - 1 - jnp.arange(el)]
    scr = rev[0][..., None] * bbr[None] - rev[1][..., None] * bbi[None]
    sci = rev[0][..., None] * bbi[None] + rev[1][..., None] * bbr[None]

    def blk_s(a):
        a = jnp.einsum('skgpj,gh->skgjhp', a.reshape(el, nb, GROUP_BLOCK, p, hh), eye)
        return a.reshape(el, nb, LANES, GROUP_BLOCK * p)

    smat = jnp.concatenate([blk_s(scr), blk_s(sci)], axis=-1)

    c1r = c_re[None] * pr[1:, :, None, :] - c_im[None] * pi[1:, :, None, :]
    c1i = c_re[None] * pi[1:, :, None, :] + c_im[None] * pr[1:, :, None, :]

    def blk_r(a):
        a = jnp.einsum('tkgip,gh->tkgphi', a.reshape(el, nb, GROUP_BLOCK, hh, p), eye)
        return a.reshape(el, nb, GROUP_BLOCK * p, LANES)

    rmat = jnp.concatenate([blk_r(c1r), -blk_r(c1i)], axis=-2)

    decay_re = pr[el].reshape(1, g * p)
    decay_im = pi[el].reshape(1, g * p)
    return kmat.astype(BF16), smat.astype(BF16), rmat.astype(BF16), decay_re, decay_im


def _state_to_lanes(h_re, h_im, nb):
    b = h_re.shape[0]
    return jnp.stack([h_re.reshape(b, nb, -1), h_im.reshape(b, nb, -1)], axis=2).reshape(b, -1)


def _lanes_to_state(h, g, p):
    b = h.shape[0]
    nb = g // GROUP_BLOCK
    h = h.reshape(b, nb, 2, GROUP_BLOCK, p)
    return h[:, :, 0].reshape(b, g, p), h[:, :, 1].reshape(b, g, p)


def _ssm_state_kernel(*refs, has_h0, nb, sw, half):
    if has_h0:
        u_ref, smat_ref, h0_ref, dre_ref, dim_ref, s_ref = refs
    else:
        u_ref, smat_ref, s_ref = refs
    for k in range(nb):
        acc = None
        for s in range(SSM_CHUNK):
            ub = u_ref[:, s * sw + k * LANES: s * sw + (k + 1) * LANES].astype(BF16)
            d = _dot(ub, smat_ref[s, k])
            acc = d if acc is None else acc + d
        base = 2 * half * k
        if has_h0:
            hr = h0_ref[:, base:base + half]
            hi = h0_ref[:, base + half:base + 2 * half]
            dr = dre_ref[:, k * half:(k + 1) * half]
            di = dim_ref[:, k * half:(k + 1) * half]
            s_ref[:, base:base + half] = acc[:, :half] + dr * hr - di * hi
            s_ref[:, base + half:base + 2 * half] = acc[:, half:] + dr * hi + di * hr
        else:
            s_ref[:, base:base + 2 * half] = acc


def ssm_state(u2, smat, rows, h0=None, decay_re=None, decay_im=None):
    nc, width = u2.shape
    sw = width // SSM_CHUNK
    nb = sw // LANES
    half = smat.shape[-1] // 2
    sl = nb * 2 * half
    has_h0 = h0 is not None
    row = lambda w: pl.BlockSpec((rows, w), lambda i: (i, 0))
    in_specs = [row(width), pl.BlockSpec(smat.shape, lambda i: (0, 0, 0, 0))]
    args = [u2, smat]
    if has_h0:
        in_specs += [row(sl), pl.BlockSpec((1, nb * half), lambda i: (0, 0)),
                     pl.BlockSpec((1, nb * half), lambda i: (0, 0))]
        args += [h0, decay_re, decay_im]
    return pl.pallas_call(
        functools.partial(_ssm_state_kernel, has_h0=has_h0, nb=nb, sw=sw, half=half),
        grid=(nc // rows,),
        in_specs=in_specs,
        out_specs=row(sl),
        out_shape=jax.ShapeDtypeStruct((nc, sl), F32),
        compiler_params=_cparams(("arbitrary",)),
        name="ssm_state",
    )(*args)


def _ssm_scan_kernel(s_ref, dre_ref, dim_ref, hs_ref, hend_ref, h_scr, *, nb, half):
    @pl.when(pl.program_id(0) == 0)
    def _():
        h_scr[...] = jnp.zeros_like(h_scr)

    dr = dre_ref[...]
    di = dim_ref[...]

    def body(r, h):
        hs_ref[pl.ds(r, 1), :] = h
        s = s_ref[pl.ds(r, 1), :]
        parts = []
        for k in range(nb):
            base = 2 * half * k
            hr, hi = h[:, base:base + half], h[:, base + half:base + 2 * half]
            ar, ai = dr[:, k * half:(k + 1) * half], di[:, k * half:(k + 1) * half]
            parts.append(ar * hr - ai * hi + s[:, base:base + half])
            parts.append(ar * hi + ai * hr + s[:, base + half:base + 2 * half])
        return jnp.concatenate(parts, axis=1)

    h = lax.fori_loop(0, s_ref.shape[0], body, h_scr[...])
    h_scr[...] = h
    hend_ref[...] = h


def ssm_scan(s, decay_re, decay_im, rows, nb):
    nc, sl = s.shape
    n_half_total = decay_re.shape[1]
    half = n_half_total // nb
    row = pl.BlockSpec((rows, sl), lambda i: (i, 0))
    vec = pl.BlockSpec((1, n_half_total), lambda i: (0, 0))
    return pl.pallas_call(
        functools.partial(_ssm_scan_kernel, nb=nb, half=half),
        grid=(nc // rows,),
        in_specs=[row, vec, vec],
        out_specs=[row, pl.BlockSpec((1, sl), lambda i: (0, 0))],
        out_shape=[jax.ShapeDtypeStruct((nc, sl), F32), jax.ShapeDtypeStruct((1, sl), F32)],
        scratch_shapes=[pltpu.VMEM((1, sl), F32)],
        compiler_params=_cparams(("arbitrary",)),
        name="ssm_scan",
    )(s, decay_re, decay_im)


def _gelu_exact(y):
    return 0.5 * y * (1.0 + lax.erf(y * (0.5 ** 0.5)))


def _ssm_out_kernel(u_ref, hs_ref, kmat_ref, rmat_ref, d_ref, wglu_ref, bglu_ref, y_ref, *, nb, sw, half):
    hb = [hs_ref[:, 2 * half * k:2 * half * (k + 1)].astype(BF16) for k in range(nb)]
    ub = [[u_ref[:, s * sw + k * LANES: s * sw + (k + 1) * LANES].astype(BF16) for k in range(nb)]
          for s in range(SSM_CHUNK)]
    wglu = wglu_ref[...]
    for t in range(SSM_CHUNK):
        cols = []
        for k in range(nb):
            acc = _dot(hb[k], rmat_ref[t, k])
            for s in range(t + 1):
                acc = acc + _dot(ub[s][k], kmat_ref[t - s, k])
            cols.append(acc)
        y = jnp.concatenate(cols, axis=1) + d_ref[...] * u_ref[:, t * sw:(t + 1) * sw]
        g = _gelu_exact(y)
        gate = jax.nn.sigmoid(_dot(g.astype(BF16), wglu) + bglu_ref[...])
        y_ref[:, t * sw:(t + 1) * sw] = (g * gate).astype(y_ref.dtype)


def ssm_out(u2, hs, kmat, rmat, d_vec, w_glu_b, b_glu, rows):
    nc, width = u2.shape
    sw = width // SSM_CHUNK
    nb = sw // LANES
    sl = hs.shape[1]
    half = sl // (2 * nb)
    row = lambda w: pl.BlockSpec((rows, w), lambda i: (i, 0))
    const = lambda a: pl.BlockSpec(a.shape, lambda i: (0,) * a.ndim)
    return pl.pallas_call(
        functools.partial(_ssm_out_kernel, nb=nb, sw=sw, half=half),
        grid=(nc // rows,),
        in_specs=[row(width), row(sl), const(kmat), const(rmat), const(d_vec), const(w_glu_b), const(b_glu)],
        out_specs=row(width),
        out_shape=jax.ShapeDtypeStruct((nc, width), BF16),
        compiler_params=_cparams(("arbitrary",)),
        name="ssm_out",
    )(u2, hs, kmat, rmat, d_vec, w_glu_b, b_glu)


def _postmix_kernel(x_ref, o_ref, y_ref, wo_ref, gate1_ref, g2_ref, scale2_ref, shift2_ref, wrh_ref, wrl_ref,
                    br_ref, tril_ref, x1_ref, hn2_ref, topi_ref, topg_ref, topr_ref, cnt_ref, carry_ref, *, aw):
    @pl.when(pl.program_id(0) == 0)
    def _():
        carry_ref[...] = jnp.zeros_like(carry_ref)

    mix = _dot(o_ref[...], wo_ref[:aw, :]) + _dot(y_ref[...], wo_ref[aw:, :])
    x1 = x_ref[...] + gate1_ref[...] * mix
    x1_ref[...] = x1
    ms = jnp.mean(x1 * x1, axis=-1, keepdims=True)
    hn2 = x1 * lax.rsqrt(ms + RMS_EPS) * g2_ref[...]
    hn2 = hn2 * (1.0 + scale2_ref[...]) + shift2_ref[...]
    hn2_ref[...] = hn2

    hh, hl = _split_bf16(hn2)
    wrh = wrh_ref[...]
    logits = _dot(hh, wrh) + _dot(hl, wrh) + _dot(hh, wrl_ref[...]) + br_ref[...]

    rows = logits.shape[0]
    lane = lax.broadcasted_iota(jnp.int32, (rows, ROUTER_PAD), 1)
    work = logits
    sel = jnp.zeros((rows, ROUTER_PAD), F32)
    picks, vals, idxs = [], [], []
    for _ in range(TOP_K):
        m = jnp.max(work, axis=-1, keepdims=True)
        idx = jnp.min(jnp.where(work == m, lane, ROUTER_PAD), axis=-1, keepdims=True)
        pick = lane == idx
        picks.append(pick)
        vals.append(m)
        idxs.append(idx)
        sel = jnp.where(pick, 1.0, sel)
        work = jnp.where(pick, -jnp.inf, work)

    rank = _dot(tril_ref[...], sel.astype(BF16)) + carry_ref[...]
    carry_ref[...] = rank[rows - 1:rows, :] + sel[rows - 1:rows, :]
    cnt_ref[...] = carry_ref[...]

    es = [jnp.exp(v - vals[0]) for v in vals]
    den = es[0]
    for e in es[1:]:
        den = den + e
    topi = jnp.zeros((rows, ROUTER_PAD), F32)
    topg = jnp.zeros((rows, ROUTER_PAD), F32)
    topr = jnp.zeros((rows, ROUTER_PAD), F32)
    for r in range(TOP_K):
        rk = jnp.sum(jnp.where(picks[r], rank, 0.0), axis=-1, keepdims=True)
        topi = jnp.where(lane == r, idxs[r].astype(F32), topi)
        topg = jnp.where(lane == r, es[r] / den, topg)
        topr = jnp.where(lane == r, rk, topr)
    topi_ref[...] = topi
    topg_ref[...] = topg
    topr_ref[...] = topr


def postmix(x, o_attn, y_ssm, w_out_b, gate1, g2, scale2, shift2, wr_hi, wr_lo, br, rows):
    n, d = x.shape
    aw = o_attn.shape[1]
    tok = lambda width: pl.BlockSpec((rows, width), lambda i: (i, 0))
    const = lambda a: pl.BlockSpec(a.shape, lambda i: (0,) * a.ndim)
    r = lax.broadcasted_iota(jnp.int32, (rows, rows), 0)
    c = lax.broadcasted_iota(jnp.int32, (rows, rows), 1)
    tril = jnp.where(c < r, 1.0, 0.0).astype(BF16)
    lanes_out = jax.ShapeDtypeStruct((n, ROUTER_PAD), F32)
    return pl.pallas_call(
        functools.partial(_postmix_kernel, aw=aw),
        grid=(n // rows,),
        in_specs=[tok(d), tok(aw), tok(y_ssm.shape[1]), const(w_out_b),
                  _row_spec(gate1.shape[0], rows, d), const(g2),
                  _row_spec(scale2.shape[0], rows, d), _row_spec(shift2.shape[0], rows, d),
                  const(wr_hi), const(wr_lo), const(br), const(tril)],
        out_specs=[tok(d), tok(d), tok(ROUTER_PAD), tok(ROUTER_PAD), tok(ROUTER_PAD),
                   pl.BlockSpec((1, ROUTER_PAD), lambda i: (0, 0))],
        out_shape=[jax.ShapeDtypeStruct((n, d), F32), jax.ShapeDtypeStruct((n, d), F32),
                   lanes_out, lanes_out, lanes_out, jax.ShapeDtypeStruct((1, ROUTER_PAD), F32)],
        scratch_shapes=[pltpu.VMEM((1, ROUTER_PAD), F32)],
        compiler_params=_cparams(("arbitrary",)),
        name="postmix",
    )(x, o_attn, y_ssm, w_out_b, gate1, g2, scale2, shift2, wr_hi, wr_lo, br, tril)


def _row_copy(src_hbm, s, dst, d, sem):
    return pltpu.make_async_copy(src_hbm.at[s], dst.at[d], sem)


def _dispatch_kernel(dest_ref, hn_hbm, xr_in, xr_hbm, sem, *, tile):
    del xr_in
    base = pl.program_id(0) * tile
    n_copy = tile * TOP_K

    def start(a, c):
        _row_copy(hn_hbm, base + a // TOP_K, xr_hbm, dest_ref[a], sem).start()
        return c

    def wait(a, c):
        _row_copy(hn_hbm, 0, xr_hbm, 0, sem).wait()
        return c

    lax.fori_loop(0, n_copy, start, 0)
    lax.fori_loop(0, n_copy, wait, 0)


def dispatch(hn3, dest_flat, n_rows, tile):
    n = hn3.shape[0]
    xr0 = jnp.zeros((n_rows,) + hn3.shape[1:], hn3.dtype)
    return pl.pallas_call(
        functools.partial(_dispatch_kernel, tile=tile),
        grid=(n // tile,),
        in_specs=[pl.BlockSpec((tile * TOP_K,), lambda i: (i,), memory_space=pltpu.SMEM),
                  pl.BlockSpec(memory_space=pl.ANY), pl.BlockSpec(memory_space=pl.ANY)],
        out_specs=pl.BlockSpec(memory_space=pl.ANY),
        out_shape=jax.ShapeDtypeStruct(xr0.shape, xr0.dtype),
        scratch_shapes=[pltpu.SemaphoreType.DMA(())],
        input_output_aliases={2: 0},
        compiler_params=_cparams(("arbitrary",)),
        name="moe_dispatch",
    )(dest_flat, hn3, xr0)


def _expert_kernel(be_ref, nb_ref, x_ref, wgu_ref, bgu_ref, wd_ref, bd_ref, y_ref, wgu_b, wd_b, *, ff):
    i = pl.program_id(0)

    @pl.when(i < nb_ref[0])
    def _():
        prev = be_ref[jnp.maximum(i - 1, 0)]

        @pl.when((i == 0) | (be_ref[i] != prev))
        def _():
            wgu_b[...] = wgu_ref[0].astype(BF16)
            wd_b[...] = wd_ref[0].astype(BF16)

        gu = _dot(x_ref[...].astype(BF16), wgu_b[...]) + bgu_ref[0]
        gate = jnp.minimum(gu[:, :ff], SWIGLU_LIMIT)
        up = jnp.clip(gu[:, ff:], -SWIGLU_LIMIT, SWIGLU_LIMIT)
        act = (up + 1.0) * (gate * jax.nn.sigmoid(SWIGLU_ALPHA * gate))
        y_ref[...] = _dot(act.astype(BF16), wd_b[...]) + bd_ref[0]

    @pl.when(i >= nb_ref[0])
    def _():
        y_ref[...] = jnp.zeros_like(y_ref)


def experts(xr, block_e, n_used, w_gu, b_gu, w_down, b_down, bm):
    n_rows, d = xr.shape
    ne, _, ff2 = w_gu.shape
    ff = ff2 // 2
    grid_spec = pltpu.PrefetchScalarGridSpec(
        num_scalar_prefetch=2,
        grid=(n_rows // bm,),
        in_specs=[pl.BlockSpec((bm, d), lambda i, be, nb: (i, 0)),
                  pl.BlockSpec((1, d, ff2), lambda i, be, nb: (be[i], 0, 0)),
                  pl.BlockSpec((1, 1, ff2), lambda i, be, nb: (be[i], 0, 0)),
                  pl.BlockSpec((1, ff, d), lambda i, be, nb: (be[i], 0, 0)),
                  pl.BlockSpec((1, 1, d), lambda i, be, nb: (be[i], 0, 0))],
        out_specs=pl.BlockSpec((bm, d), lambda i, be, nb: (i, 0)),
        scratch_shapes=[pltpu.VMEM((d, ff2), BF16), pltpu.VMEM((ff, d), BF16)],
    )
    return pl.pallas_call(
        functools.partial(_expert_kernel, ff=ff),
        grid_spec=grid_spec,
        out_shape=jax.ShapeDtypeStruct((n_rows, d), F32),
        compiler_params=_cparams(("arbitrary",)),
        name="moe_experts",
    )(block_e, n_used, xr, w_gu, b_gu.reshape(ne, 1, ff2), w_down, b_down.reshape(ne, 1, d))


def _combine_kernel(dest_ref, yr_hbm, gx_ref, x1_ref, gate2_ref, gf_ref, out_ref, buf, sem, *, tile, final_norm):
    n_copy = tile * TOP_K

    def start(a, c):
        pltpu.make_async_copy(yr_hbm.at[dest_ref[a]], buf.at[a % TOP_K, a // TOP_K], sem).start()
        return c

    def wait(a, c):
        pltpu.make_async_copy(yr_hbm.at[0], buf.at[0, 0], sem).wait()
        return c

    lax.fori_loop(0, n_copy, start, 0)
    lax.fori_loop(0, n_copy, wait, 0)

    y = None
    for k in range(TOP_K):
        term = gx_ref[:, k:k + 1, :] * buf[k]
        y = term if y is None else y + term
    x2 = x1_ref[...] + gate2_ref[...] * y
    if final_norm:
        ssq = jnp.sum(jnp.sum(x2 * x2, axis=2, keepdims=True), axis=1, keepdims=True)
        inv = lax.rsqrt(ssq / (SUBLANES * LANES) + RMS_EPS)
        x2 = x2 * inv * gf_ref[...]
    out_ref[...] = x2


def combine(yr3, dest_flat, gx, x1_3, gate2_3, gf3, tile, final_norm):
    n = x1_3.shape[0]
    tok = pl.BlockSpec((tile, SUBLANES, LANES), lambda i: (i, 0, 0))
    gate_spec = tok if gate2_3.shape[0] == n else pl.BlockSpec((1, SUBLANES, LANES), lambda i: (0, 0, 0))
    return pl.pallas_call(
        functools.partial(_combine_kernel, tile=tile, final_norm=final_norm),
        grid=(n // tile,),
        in_specs=[pl.BlockSpec((tile * TOP_K,), lambda i: (i,), memory_space=pltpu.SMEM),
                  pl.BlockSpec(memory_space=pl.ANY),
                  pl.BlockSpec((tile, TOP_K, LANES), lambda i: (i, 0, 0)),
                  tok, gate_spec, pl.BlockSpec((1, SUBLANES, LANES), lambda i: (0, 0, 0))],
        out_specs=tok,
        out_shape=jax.ShapeDtypeStruct(x1_3.shape, F32),
        scratch_shapes=[pltpu.VMEM((TOP_K, tile, SUBLANES, LANES), F32), pltpu.SemaphoreType.DMA(())],
        compiler_params=_cparams(("arbitrary",)),
        name="moe_combine",
    )(dest_flat, yr3, gx, x1_3, gate2_3, gf3)


def moe_block(x1, hn2, topi, topg, topr, cnt, gate2, norm_f_g, final_norm, w_gu, b_gu, w_down, b_down, bm, tile):
    n, d = x1.shape
    ne = w_gu.shape[0]
    assert d == SUBLANES * LANES
    counts = cnt[0, :ne].astype(jnp.int32)
    padded = ((counts + bm - 1) // bm) * bm
    pend = jnp.cumsum(padded)
    pstart = pend - padded
    idx = topi[:, :TOP_K].astype(jnp.int32)
    onehot = (idx[..., None] == jnp.arange(ne, dtype=jnp.int32)).astype(jnp.int32)
    dest = jnp.sum(onehot * pstart, axis=-1) + topr[:, :TOP_K].astype(jnp.int32)
    dest_flat = dest.reshape(-1)
    n_rows = -(-(n * TOP_K) // bm) * bm + ne * bm
    n_blocks = n_rows // bm
    starts = jnp.arange(n_blocks, dtype=jnp.int32) * bm
    block_e = jnp.minimum(jnp.sum((starts[:, None] >= pend[None, :]).astype(jnp.int32), axis=1), ne - 1)
    n_used = (pend[-1:] // bm).astype(jnp.int32)
    last_e = block_e[jnp.maximum(n_used[0] - 1, 0)]
    block_e = jnp.where(jnp.arange(n_blocks) < n_used[0], block_e, last_e).astype(jnp.int32)

    xr = dispatch(hn2.reshape(n, SUBLANES, LANES), dest_flat, n_rows, tile)
    yr = experts(xr.reshape(n_rows, d), block_e, n_used, w_gu, b_gu, w_down, b_down, bm)
    gx = jnp.broadcast_to(topg[:, :TOP_K, None], (n, TOP_K, LANES))
    gate2_3 = gate2.reshape(gate2.shape[0], SUBLANES, LANES)
    out = combine(yr.reshape(n_rows, SUBLANES, LANES), dest_flat, gx, x1.reshape(n, SUBLANES, LANES),
                  gate2_3, norm_f_g.reshape(1, SUBLANES, LANES), tile, final_norm)
    return out.reshape(n, d)


def _group_forward(x, mod, attend, h0_lanes, t_per_seq, lw, mats, norm_f_g, final_norm, rows):
    n, d = x.shape
    rows = min(rows, n)
    (norm1_g, norm2_g, w_in_b, w_out_b, w_glu_b, b_glu, d_vec, wr_hi, wr_lo, br, w_gu, b_gu, w_down, b_down) = lw
    kmat, smat, rmat, decay_re, decay_im = mats
    shift1, scale1, gate1, shift2, scale2, gate2 = [mod[:, j * d:(j + 1) * d] for j in range(6)]
    aw = (w_in_b.shape[1] - w_glu_b.shape[0]) // 3
    q_dtype = BF16 if h0_lanes is None else F32
    q, k, v, kb, vb, u = inproj(x, shift1, scale1, norm1_g, w_in_b, aw, q_dtype, rows)
    o_attn = attend(q, k, v, kb, vb)

    sw = u.shape[1]
    nb = sw // LANES
    u2 = u.reshape(n // SSM_CHUNK, SSM_CHUNK * sw)
    nc = u2.shape[0]
    crow = min(256, nc)
    if h0_lanes is None:
        s = ssm_state(u2, smat, crow)
        hs, hend = ssm_scan(s, decay_re, decay_im, crow, nb)
    else:
        assert t_per_seq == SSM_CHUNK
        hs = h0_lanes
        hend = ssm_state(u2, smat, crow, h0_lanes, decay_re, decay_im)
    y_ssm = ssm_out(u2, hs, kmat, rmat, d_vec, w_glu_b, b_glu, crow).reshape(n, sw)

    x1, hn2, topi, topg, topr, cnt = postmix(x, o_attn, y_ssm, w_out_b, gate1, norm2_g, scale2, shift2,
                                             wr_hi, wr_lo, br, rows)
    y = moe_block(x1, hn2, topi, topg, topr, cnt, gate2, norm_f_g, final_norm, w_gu, b_gu, w_down, b_down,
                  bm=256, tile=256)
    return y, k, v, hend


def kernel(x_prompt, x_sample, c_prompt, c_sample, cache_k, cache_v, state_ssm_re, state_ssm_im, page_table, norm1_g, norm2_g, w_ada, b_ada, w_in, w_out, sb_bias, lam_re, lam_im, log_dt, ssm_b_re, ssm_b_im, ssm_c_re, ssm_c_im, ssm_d, w_glu, b_glu, w_router, b_router, w_gu, b_gu, w_down, b_down, norm_f_g):
    depth = w_in.shape[0]
    bp, tp, d = x_prompt.shape
    bs, ts, _ = x_sample.shape
    assert bp == 1, "the prompt group is handled as one long sequence"
    n_heads = cache_k.shape[3]
    aw = n_heads * HEAD_DIM
    g, p = lam_re.shape[1:]
    nb = g // GROUP_BLOCK
    ne = w_router.shape[-1]
    n_pool, page = cache_k.shape[1:3]
    rows = 512

    xp = x_prompt.reshape(bp * tp, d)
    xs = x_sample.reshape(bs * ts, d)
    n_c = bp + bs
    c_all = jnp.concatenate([c_prompt, c_sample, jnp.zeros((-n_c % SUBLANES, d), F32)], axis=0)

    outs = {name: [] for name in ("kp", "vp", "rp", "ip", "ks", "vs", "rs", "is")}
    for l in range(depth):
        mod = ada_mod(c_all, w_ada[l], b_ada[l])
        mod_p = mod[:bp]
        mod_s = jnp.repeat(mod[bp:n_c], ts, axis=0)
        mats = _ssm_mats(lam_re[l], lam_im[l], log_dt[l], ssm_b_re[l], ssm_b_im[l], ssm_c_re[l], ssm_c_im[l])
        wr = jnp.pad(w_router[l], ((0, 0), (0, ROUTER_PAD - ne)))
        wr_hi = wr.astype(BF16)
        wr_lo = (wr - wr_hi.astype(F32)).astype(BF16)
        br = jnp.concatenate([b_router[l], jnp.full((ROUTER_PAD - ne,), NEG_BIG, F32)]).reshape(1, ROUTER_PAD)
        lw = (norm1_g[l].reshape(1, d), norm2_g[l].reshape(1, d), w_in[l].astype(BF16), w_out[l].astype(BF16),
              w_glu[l].astype(BF16), b_glu[l].reshape(1, -1), ssm_d[l].reshape(1, -1), wr_hi, wr_lo, br,
              w_gu[l], b_gu[l], w_down[l], b_down[l])
        bias = sb_bias[l]

        def attend_p(q, k, v, kb, vb):
            return attn_prompt(q, kb, vb, bias)

        ck = cache_k[l].transpose(0, 2, 3, 1).reshape(n_pool, aw, page)
        cv = cache_v[l].transpose(0, 2, 3, 1).reshape(n_pool, aw, page)

        def attend_s(q, k, v, kb, vb):
            return attn_sample(q, k, v, ck, cv, page_table, bias, ts, pages_per_step=8)

        h0 = _state_to_lanes(state_ssm_re[l], state_ssm_im[l], nb)
        xp, kp, vp, hp = _group_forward(xp, mod_p, attend_p, None, tp, lw, mats, norm_f_g, l == depth - 1, rows)
        xs, ks, vs, hs = _group_forward(xs, mod_s, attend_s, h0, ts, lw, mats, norm_f_g, l == depth - 1, rows)
        rp, ip = _lanes_to_state(hp, g, p)
        rs, is_ = _lanes_to_state(hs, g, p)
        outs["kp"].append(kp.reshape(bp, tp, n_heads, HEAD_DIM))
        outs["vp"].append(vp.reshape(bp, tp, n_heads, HEAD_DIM))
        outs["rp"].append(rp)
        outs["ip"].append(ip)
        outs["ks"].append(ks.reshape(bs, ts, n_heads, HEAD_DIM))
        outs["vs"].append(vs.reshape(bs, ts, n_heads, HEAD_DIM))
        outs["rs"].append(rs)
        outs["is"].append(is_)
    st = lambda name: jnp.stack(outs[name])
    return (xp.reshape(bp, tp, d), xs.reshape(bs, ts, d), st("kp"), st("vp"), st("rp"), st("ip"),
            st("ks"), st("vs"), st("rs"), st("is"))
```

```python
import functools
import math

import jax
import jax.numpy as jnp
from jax import lax
from jax.experimental import pallas as pl
from jax.experimental.pallas import tpu as pltpu

F32 = jnp.float32
BF16 = jnp.bfloat16

HEAD_DIM = 64
SSM_GROUP = 16
TOP_K = 4
SWIGLU_LIMIT = 7.0
SWIGLU_ALPHA = 1.702
RMS_EPS = 1e-6

LANES = 128
SUBLANES = 8
VMEM_LIMIT_BYTES = 56 * 1024 * 1024

KEY_BLOCK = LANES
SSM_CHUNK = 8
GROUP_BLOCK = LANES // SSM_GROUP
ROUTER_PAD = LANES
NEG_BIG = -1e30
SIGN_BIT = -2 ** 31


def _cparams(semantics):
    return pltpu.CompilerParams(dimension_semantics=semantics, vmem_limit_bytes=VMEM_LIMIT_BYTES)


def _dot(a, b):
    return jnp.dot(a, b, preferred_element_type=F32)


def _dot_nt(a, b):
    return lax.dot_general(a, b, (((1,), (1,)), ((), ())), preferred_element_type=F32)


def _split_bf16(x):
    hi = x.astype(BF16)
    lo = (x - hi.astype(F32)).astype(BF16)
    return hi, lo


def _ada_kernel(c_ref, w_ref, b_ref, o_ref):
    o_ref[...] = _dot(c_ref[...].astype(BF16), w_ref[...].astype(BF16)) + b_ref[...]


def ada_mod(c, w_ada, b_ada):
    n, d = c.shape
    n_out = w_ada.shape[1]
    return pl.pallas_call(
        _ada_kernel,
        grid=(n_out // d,),
        in_specs=[pl.BlockSpec((n, d), lambda j: (0, 0)),
                  pl.BlockSpec((d, d), lambda j: (0, j)),
                  pl.BlockSpec((1, d), lambda j: (0, j))],
        out_specs=pl.BlockSpec((n, d), lambda j: (0, j)),
        out_shape=jax.ShapeDtypeStruct((n, n_out), F32),
        compiler_params=_cparams(("arbitrary",)),
        name="ada_mod",
    )(c, w_ada, b_ada.reshape(1, n_out))


def _inproj_kernel(x_ref, shift_ref, scale_ref, g_ref, w_ref, q_ref, k_ref, v_ref, kb_ref, vb_ref, u_ref, *, aw):
    x = x_ref[...]
    ms = jnp.mean(x * x, axis=-1, keepdims=True)
    hn = x * lax.rsqrt(ms + RMS_EPS) * g_ref[...]
    hn = hn * (1.0 + scale_ref[...]) + shift_ref[...]
    proj = _dot(hn.astype(BF16), w_ref[...])
    q_ref[...] = (proj[:, :aw] * (HEAD_DIM ** -0.5)).astype(q_ref.dtype)
    k = proj[:, aw:2 * aw]
    v = proj[:, 2 * aw:3 * aw]
    k_ref[...] = k
    v_ref[...] = v
    kb_ref[...] = k.astype(BF16)
    vb_ref[...] = v.astype(BF16)
    u_ref[...] = proj[:, 3 * aw:]


def _row_spec(n_mod_rows, rows, d):
    if n_mod_rows == 1:
        return pl.BlockSpec((1, d), lambda i: (0, 0))
    return pl.BlockSpec((rows, d), lambda i: (i, 0))


def inproj(x, shift, scale, g, w_in_b, aw, q_dtype, rows):
    n, d = x.shape
    pw = w_in_b.shape[1]
    sw = pw - 3 * aw
    tok = lambda width: pl.BlockSpec((rows, width), lambda i: (i, 0))
    return pl.pallas_call(
        functools.partial(_inproj_kernel, aw=aw),
        grid=(n // rows,),
        in_specs=[tok(d), _row_spec(shift.shape[0], rows, d), _row_spec(scale.shape[0], rows, d),
                  pl.BlockSpec((1, d), lambda i: (0, 0)),
                  pl.BlockSpec((d, pw), lambda i: (0, 0))],
        out_specs=[tok(aw), tok(aw), tok(aw), tok(aw), tok(aw), tok(sw)],
        out_shape=[jax.ShapeDtypeStruct((n, aw), q_dtype),
                   jax.ShapeDtypeStruct((n, aw), F32), jax.ShapeDtypeStruct((n, aw), F32),
                   jax.ShapeDtypeStruct((n, aw), BF16), jax.ShapeDtypeStruct((n, aw), BF16),
                   jax.ShapeDtypeStruct((n, sw), F32)],
        compiler_params=_cparams(("arbitrary",)),
        name="inproj",
    )(x, shift, scale, g, w_in_b)


def _tri_ones():
    j = lax.broadcasted_iota(jnp.int32, (KEY_BLOCK, 2 * KEY_BLOCK), 0)
    s = lax.broadcasted_iota(jnp.int32, (KEY_BLOCK, 2 * KEY_BLOCK), 1)
    return jnp.where((j > s) | (s >= KEY_BLOCK), 1.0, 0.0).astype(BF16)


def _sb_logits(z, mask):
    neg_abs = lax.bitcast_convert_type(lax.bitcast_convert_type(z, jnp.int32) | SIGN_BIT, F32)
    drop = jnp.maximum(z, 0.0) + jnp.log(1.0 + jnp.exp(neg_abs))
    if mask is not None:
        drop = jnp.where(mask, drop, 0.0)
    return z - drop, drop.astype(BF16)


def _sb_finish(log_beta, drop_b, carry, tri, mask):
    groups = drop_b.shape[1] // KEY_BLOCK
    later, total = [], []
    for g in range(groups):
        cs = _dot(drop_b[:, g * KEY_BLOCK:(g + 1) * KEY_BLOCK], tri)
        later.append(cs[:, :KEY_BLOCK])
        total.append(cs[:, KEY_BLOCK:])
    later = jnp.concatenate(later, axis=1) if groups > 1 else later[0]
    total = jnp.concatenate(total, axis=1) if groups > 1 else total[0]
    a = jnp.exp(log_beta - (later + carry))
    if mask is not None:
        a = jnp.where(mask, a, 0.0)
    return a.astype(BF16), total


def _sb_weights(z, carry, tri, mask):
    if mask is not None:
        mask = jnp.concatenate([mask] * (z.shape[1] // KEY_BLOCK), axis=1)
    log_beta, drop_b = _sb_logits(z, mask)
    return _sb_finish(log_beta, drop_b, carry, tri, mask)


def _attn_prompt_kernel(bias_ref, q_ref, k_ref, v_ref, tri_ref, o_ref, acc_ref, carry_ref, lb_scr, drop_scr, *, nsb):
    it = pl.program_id(1)
    tq = nsb * KEY_BLOCK
    bias = bias_ref[0]
    tri = tri_ref[...]
    acc_ref[...] = jnp.zeros_like(acc_ref)
    carry_ref[...] = jnp.zeros_like(carry_ref)
    lane = lax.broadcasted_iota(jnp.int32, (KEY_BLOCK, LANES), 1)
    first_head = lane < HEAD_DIM

    def stacked(ref, block_index):
        st = pl.multiple_of(block_index * KEY_BLOCK, KEY_BLOCK)
        blk = ref[pl.ds(st, KEY_BLOCK), :]
        zero = jnp.zeros_like(blk)
        return jnp.concatenate([jnp.where(first_head, blk, zero), jnp.where(first_head, zero, blk)], axis=0)

    for c in reversed(range(nsb)):
        r0 = c * KEY_BLOCK
        rr = lax.broadcasted_iota(jnp.int32, (tq - r0, LANES), 0)
        ll = lax.broadcasted_iota(jnp.int32, (tq - r0, LANES), 1)
        mask = (rr >= KEY_BLOCK) | (ll < rr)
        z = _dot_nt(q_ref[r0:, :], stacked(k_ref, it * nsb + c)) + bias
        a, tot = _sb_weights(z, carry_ref[r0:, :], tri, mask)
        acc_ref[r0:, :] += _dot(a, stacked(v_ref, it * nsb + c))
        carry_ref[r0:, :] += tot

    n_full = it * nsb

    def first_half(block_index, slot):
        z = _dot_nt(q_ref[...], stacked(k_ref, block_index)) + bias
        log_beta, drop_b = _sb_logits(z, None)
        lb_scr[slot] = log_beta
        drop_scr[slot] = drop_b

    def second_half(block_index, slot):
        a, tot = _sb_finish(lb_scr[slot], drop_scr[slot], carry_ref[...], tri, None)
        acc_ref[...] += _dot(a, stacked(v_ref, block_index))
        carry_ref[...] += tot

    @pl.when(n_full > 0)
    def _():
        first_half(n_full - 1, 0)

        def body(jj, c):
            blk = n_full - 1 - 2 * jj
            first_half(blk - 1, 1)
            second_half(blk, 0)
            first_half(blk - 2, 0)
            second_half(blk - 1, 1)
            return c

        lax.fori_loop(0, n_full // 2 - 1, body, 0)
        first_half(0, 1)
        second_half(1, 0)
        second_half(0, 1)

    o_ref[...] = acc_ref[...].astype(o_ref.dtype)


def attn_prompt(q, kb, vb, bias, nsb):
    n, aw = q.shape
    tq = nsb * KEY_BLOCK
    n_pairs = aw // LANES
    bias2 = jnp.repeat(bias.astype(F32), KEY_BLOCK).reshape(n_pairs, 1, 2 * KEY_BLOCK)
    tri = _tri_ones()
    return pl.pallas_call(
        functools.partial(_attn_prompt_kernel, nsb=nsb),
        grid=(n_pairs, n // tq),
        in_specs=[pl.BlockSpec((1, 1, 2 * KEY_BLOCK), lambda hp, i: (hp, 0, 0)),
                  pl.BlockSpec((tq, LANES), lambda hp, i: (i, hp)),
                  pl.BlockSpec((n, LANES), lambda hp, i: (0, hp)),
                  pl.BlockSpec((n, LANES), lambda hp, i: (0, hp)),
                  pl.BlockSpec(tri.shape, lambda hp, i: (0, 0))],
        out_specs=pl.BlockSpec((tq, LANES), lambda hp, i: (i, hp)),
        out_shape=jax.ShapeDtypeStruct((n, aw), BF16),
        scratch_shapes=[pltpu.VMEM((tq, LANES), F32), pltpu.VMEM((tq, 2 * KEY_BLOCK), F32),
                        pltpu.VMEM((2, tq, 2 * KEY_BLOCK), F32), pltpu.VMEM((2, tq, 2 * KEY_BLOCK), BF16)],
        compiler_params=_cparams(("arbitrary", "arbitrary")),
        name="attn_prompt",
    )(bias2, q, kb, vb, tri)


def _attn_sample_kernel(pt_ref, q_ref, kn_ref, vn_ref, *rest, pages_per_step, n_heads, t_new):
    del pt_ref
    k_refs = rest[:pages_per_step]
    v_refs = rest[pages_per_step:2 * pages_per_step]
    tri_ref, bias_ref, o_ref, qbd_ref, acc_ref, carry_ref = rest[2 * pages_per_step:]
    p = pl.program_id(1)
    m = n_heads * t_new
    aw = n_heads * HEAD_DIM
    tri = tri_ref[...]
    bias = bias_ref[...]

    @pl.when(p == 0)
    def _():
        q = q_ref[...].astype(BF16)
        qt = jnp.concatenate([q] * n_heads, axis=0)
        rw = lax.broadcasted_iota(jnp.int32, (m, aw), 0)
        ln = lax.broadcasted_iota(jnp.int32, (m, aw), 1)
        qbd = jnp.where(ln // HEAD_DIM == rw // t_new, qt, jnp.zeros_like(qt))
        qbd_ref[...] = qbd
        pad = jnp.zeros((KEY_BLOCK - t_new, aw), F32)
        kn = jnp.concatenate([kn_ref[...], pad], axis=0).astype(BF16)
        vn = jnp.concatenate([vn_ref[...], pad], axis=0).astype(BF16)
        r2 = lax.broadcasted_iota(jnp.int32, (m, KEY_BLOCK), 0)
        l2 = lax.broadcasted_iota(jnp.int32, (m, KEY_BLOCK), 1)
        mask = l2 < (r2 % t_new)
        z = _dot_nt(qbd, kn) + bias
        a, tot = _sb_weights(z, 0.0, tri, mask)
        acc_ref[...] = _dot(a, vn)
        carry_ref[...] = tot

    kt = jnp.concatenate([r[0].astype(BF16) for r in k_refs], axis=1)
    vt = jnp.concatenate([r[0].astype(BF16) for r in v_refs], axis=1)
    z = _dot(qbd_ref[...], kt) + jnp.concatenate([bias] * pages_per_step, axis=1)
    log_beta, drop_b = _sb_logits(z, None)
    carry = carry_ref[...]
    shift = []
    for i in range(pages_per_step):
        cs = _dot(drop_b[:, i * KEY_BLOCK:(i + 1) * KEY_BLOCK], tri)
        shift.append(cs[:, :KEY_BLOCK] + carry)
        carry = carry + cs[:, KEY_BLOCK:]
    a = jnp.exp(log_beta - jnp.concatenate(shift, axis=1)).astype(BF16)
    acc_ref[...] += _dot_nt(a, vt)
    carry_ref[...] = carry

    @pl.when(p == pl.num_programs(1) - 1)
    def _():
        acc = acc_ref[...]
        ln = lax.broadcasted_iota(jnp.int32, (t_new, aw), 1)
        o = jnp.zeros((t_new, aw), F32)
        for h in range(n_heads):
            o = o + jnp.where(ln // HEAD_DIM == h, acc[h * t_new:(h + 1) * t_new, :], 0.0)
        o_ref[...] = o.astype(o_ref.dtype)


def attn_sample(q, k_new, v_new, cache_k, cache_v, page_table, bias, t_new, pages_per_step):
    n, aw = q.shape
    bsz, n_pages = page_table.shape
    n_heads = aw // HEAD_DIM
    page = cache_k.shape[2]
    assert page == KEY_BLOCK and t_new == SUBLANES and n_pages % pages_per_step == 0
    m = n_heads * t_new
    bias_rows = jnp.broadcast_to(jnp.repeat(bias, t_new)[:, None], (m, KEY_BLOCK)).astype(F32)

    def page_spec(i):
        return pl.BlockSpec((1, aw, page),
                            lambda b, p, pt: (pt[b, n_pages - 1 - (p * pages_per_step + i)], 0, 0))

    tok = pl.BlockSpec((t_new, aw), lambda b, p, pt: (b, 0))
    grid_spec = pltpu.PrefetchScalarGridSpec(
        num_scalar_prefetch=1,
        grid=(bsz, n_pages // pages_per_step),
        in_specs=[tok, tok, tok]
        + [page_spec(i) for i in range(pages_per_step)]
        + [page_spec(i) for i in range(pages_per_step)]
        + [pl.BlockSpec((KEY_BLOCK, 2 * KEY_BLOCK), lambda b, p, pt: (0, 0)),
           pl.BlockSpec((m, KEY_BLOCK), lambda b, p, pt: (0, 0))],
        out_specs=tok,
        scratch_shapes=[pltpu.VMEM((m, aw), BF16), pltpu.VMEM((m, aw), F32), pltpu.VMEM((m, KEY_BLOCK), F32)],
    )
    return pl.pallas_call(
        functools.partial(_attn_sample_kernel, pages_per_step=pages_per_step, n_heads=n_heads, t_new=t_new),
        grid_spec=grid_spec,
        out_shape=jax.ShapeDtypeStruct((n, aw), BF16),
        compiler_params=_cparams(("arbitrary", "arbitrary")),
        name="attn_sample",
    )(page_table, q, k_new, v_new, *([cache_k] * pages_per_step), *([cache_v] * pages_per_step),
      _tri_ones(), bias_rows)


def _ssm_mats(lam_re, lam_im, log_dt, b_re, b_im, c_re, c_im):
    g, p = lam_re.shape
    hh = b_re.shape[-1]
    el = SSM_CHUNK
    nb = g // GROUP_BLOCK
    dt = jnp.exp(log_dt)[:, None]
    ar, ai = lam_re * dt, lam_im * dt
    lbr, lbi = jnp.exp(ar) * jnp.cos(ai), jnp.exp(ar) * jnp.sin(ai)
    den = lam_re * lam_re + lam_im * lam_im
    fr = ((lbr - 1.0) * lam_re + lbi * lam_im) / den
    fi = (lbi * lam_re - (lbr - 1.0) * lam_im) / den
    bbr = fr[..., None] * b_re - fi[..., None] * b_im
    bbi = fr[..., None] * b_im + fi[..., None] * b_re
    n = jnp.arange(el + 1, dtype=F32)[:, None, None]
    pr = jnp.exp(ar[None] * n) * jnp.cos(ai[None] * n)
    pi = jnp.exp(ar[None] * n) * jnp.sin(ai[None] * n)
    eye = jnp.eye(GROUP_BLOCK, dtype=F32)

    cpr = c_re[None] * pr[:el, :, None, :] - c_im[None] * pi[:el, :, None, :]
    cpi = c_re[None] * pi[:el, :, None, :] + c_im[None] * pr[:el, :, None, :]
    kd = jnp.einsum('dgip,gpj->dgij', cpr, bbr) - jnp.einsum('dgip,gpj->dgij', cpi, bbi)
    kmat = jnp.einsum('dkgij,gh->dkgjhi', kd.reshape(el, nb, GROUP_BLOCK, hh, hh), eye)
    kmat = kmat.reshape(el, nb, LANES, LANES)

    rev = pr[el - 1 - jnp.arange(el)], pi[el - 1 - jnp.arange(el)]
    scr = rev[0][..., None] * bbr[None] - rev[1][..., None] * bbi[None]
    sci = rev[0][..., None] * bbi[None] + rev[1][..., None] * bbr[None]

    def blk_s(a):
        a = jnp.einsum('skgpj,gh->skgjhp', a.reshape(el, nb, GROUP_BLOCK, p, hh), eye)
        return a.reshape(el, nb, LANES, GROUP_BLOCK * p)

    smat = jnp.concatenate([blk_s(scr), blk_s(sci)], axis=-1)

    c1r = c_re[None] * pr[1:, :, None, :] - c_im[None] * pi[1:, :, None, :]
    c1i = c_re[None] * pi[1:, :, None, :] + c_im[None] * pr[1:, :, None, :]

    def blk_r(a):
        a = jnp.einsum('tkgip,gh->tkgphi', a.reshape(el, nb, GROUP_BLOCK, hh, p), eye)
        return a.reshape(el, nb, GROUP_BLOCK * p, LANES)

    rmat = jnp.concatenate([blk_r(c1r), -blk_r(c1i)], axis=-2)

    decay_re = pr[el].reshape(1, g * p)
    decay_im = pi[el].reshape(1, g * p)
    return kmat.astype(BF16), smat.astype(BF16), rmat.astype(BF16), decay_re, decay_im


def _state_to_lanes(h_re, h_im, nb):
    b = h_re.shape[0]
    return jnp.stack([h_re.reshape(b, nb, -1), h_im.reshape(b, nb, -1)], axis=2).reshape(b, -1)


def _lanes_to_state(h, g, p):
    b = h.shape[0]
    nb = g // GROUP_BLOCK
    h = h.reshape(b, nb, 2, GROUP_BLOCK, p)
    return h[:, :, 0].reshape(b, g, p), h[:, :, 1].reshape(b, g, p)


def _ssm_state_kernel(*refs, has_h0, nb, sw, half):
    if has_h0:
        u_ref, smat_ref, h0_ref, dre_ref, dim_ref, s_ref = refs
    else:
        u_ref, smat_ref, s_ref = refs
    for k in range(nb):
        acc = None
        for s in range(SSM_CHUNK):
            ub = u_ref[:, s * sw + k * LANES: s * sw + (k + 1) * LANES].astype(BF16)
            d = _dot(ub, smat_ref[s, k])
            acc = d if acc is None else acc + d
        base = 2 * half * k
        if has_h0:
            hr = h0_ref[:, base:base + half]
            hi = h0_ref[:, base + half:base + 2 * half]
            dr = dre_ref[:, k * half:(k + 1) * half]
            di = dim_ref[:, k * half:(k + 1) * half]
            s_ref[:, base:base + half] = acc[:, :half] + dr * hr - di * hi
            s_ref[:, base + half:base + 2 * half] = acc[:, half:] + dr * hi + di * hr
        else:
            s_ref[:, base:base + 2 * half] = acc


def ssm_state(u2, smat, rows, h0=None, decay_re=None, decay_im=None):
    nc, width = u2.shape
    sw = width // SSM_CHUNK
    nb = sw // LANES
    half = smat.shape[-1] // 2
    sl = nb * 2 * half
    has_h0 = h0 is not None
    row = lambda w: pl.BlockSpec((rows, w), lambda i: (i, 0))
    in_specs = [row(width), pl.BlockSpec(smat.shape, lambda i: (0, 0, 0, 0))]
    args = [u2, smat]
    if has_h0:
        in_specs += [row(sl), pl.BlockSpec((1, nb * half), lambda i: (0, 0)),
                     pl.BlockSpec((1, nb * half), lambda i: (0, 0))]
        args += [h0, decay_re, decay_im]
    return pl.pallas_call(
        functools.partial(_ssm_state_kernel, has_h0=has_h0, nb=nb, sw=sw, half=half),
        grid=(nc // rows,),
        in_specs=in_specs,
        out_specs=row(sl),
        out_shape=jax.ShapeDtypeStruct((nc, sl), F32),
        compiler_params=_cparams(("arbitrary",)),
        name="ssm_state",
    )(*args)


def _ssm_scan_kernel(s_ref, dre_ref, dim_ref, hs_ref, hend_ref, h_scr, *, nb, half):
    @pl.when(pl.program_id(0) == 0)
    def _():
        h_scr[...] = jnp.zeros_like(h_scr)

    dr = dre_ref[...]
    di = dim_ref[...]

    def body(r, h):
        hs_ref[pl.ds(r, 1), :] = h
        s = s_ref[pl.ds(r, 1), :]
        parts = []
        for k in range(nb):
            base = 2 * half * k
            hr, hi = h[:, base:base + half], h[:, base + half:base + 2 * half]
            ar, ai = dr[:, k * half:(k + 1) * half], di[:, k * half:(k + 1) * half]
            parts.append(ar * hr - ai * hi + s[:, base:base + half])
            parts.append(ar * hi + ai * hr + s[:, base + half:base + 2 * half])
        return jnp.concatenate(parts, axis=1)

    h = lax.fori_loop(0, s_ref.shape[0], body, h_scr[...])
    h_scr[...] = h
    hend_ref[...] = h


def ssm_scan(s, decay_re, decay_im, rows, nb):
    nc, sl = s.shape
    n_half_total = decay_re.shape[1]
    half = n_half_total // nb
    row = pl.BlockSpec((rows, sl), lambda i: (i, 0))
    vec = pl.BlockSpec((1, n_half_total), lambda i: (0, 0))
    return pl.pallas_call(
        functools.partial(_ssm_scan_kernel, nb=nb, half=half),
        grid=(nc // rows,),
        in_specs=[row, vec, vec],
        out_specs=[row, pl.BlockSpec((1, sl), lambda i: (0, 0))],
        out_shape=[jax.ShapeDtypeStruct((nc, sl), F32), jax.ShapeDtypeStruct((1, sl), F32)],
        scratch_shapes=[pltpu.VMEM((1, sl), F32)],
        compiler_params=_cparams(("arbitrary",)),
        name="ssm_scan",
    )(s, decay_re, decay_im)


def _gelu_exact(y):
    return 0.5 * y * (1.0 + lax.erf(y * (0.5 ** 0.5)))


def _ssm_out_kernel(u_ref, hs_ref, kmat_ref, rmat_ref, d_ref, wglu_ref, bglu_ref, y_ref, *, nb, sw, half):
    hb = [hs_ref[:, 2 * half * k:2 * half * (k + 1)].astype(BF16) for k in range(nb)]
    ub = [[u_ref[:, s * sw + k * LANES: s * sw + (k + 1) * LANES].astype(BF16) for k in range(nb)]
          for s in range(SSM_CHUNK)]
    wglu = wglu_ref[...]
    for t in range(SSM_CHUNK):
        cols = []
        for k in range(nb):
            acc = _dot(hb[k], rmat_ref[t, k])
            for s in range(t + 1):
                acc = acc + _dot(ub[s][k], kmat_ref[t - s, k])
            cols.append(acc)
        y = jnp.concatenate(cols, axis=1) + d_ref[...] * u_ref[:, t * sw:(t + 1) * sw]
        g = _gelu_exact(y)
        gate = jax.nn.sigmoid(_dot(g.astype(BF16), wglu) + bglu_ref[...])
        y_ref[:, t * sw:(t + 1) * sw] = (g * gate).astype(y_ref.dtype)


def ssm_out(u2, hs, kmat, rmat, d_vec, w_glu_b, b_glu, rows):
    nc, width = u2.shape
    sw = width // SSM_CHUNK
    nb = sw // LANES
    sl = hs.shape[1]
    half = sl // (2 * nb)
    row = lambda w: pl.BlockSpec((rows, w), lambda i: (i, 0))
    const = lambda a: pl.BlockSpec(a.shape, lambda i: (0,) * a.ndim)
    return pl.pallas_call(
        functools.partial(_ssm_out_kernel, nb=nb, sw=sw, half=half),
        grid=(nc // rows,),
        in_specs=[row(width), row(sl), const(kmat), const(rmat), const(d_vec), const(w_glu_b), const(b_glu)],
        out_specs=row(width),
        out_shape=jax.ShapeDtypeStruct((nc, width), BF16),
        compiler_params=_cparams(("arbitrary",)),
        name="ssm_out",
    )(u2, hs, kmat, rmat, d_vec, w_glu_b, b_glu)


def _postmix_kernel(x_ref, o_ref, y_ref, wo_ref, gate1_ref, g2_ref, scale2_ref, shift2_ref, wrh_ref, wrl_ref,
                    br_ref, tril_ref, x1_ref, hn2_ref, topi_ref, topg_ref, topr_ref, cnt_ref, carry_ref, *, aw):
    @pl.when(pl.program_id(0) == 0)
    def _():
        carry_ref[...] = jnp.zeros_like(carry_ref)

    mix = _dot(o_ref[...], wo_ref[:aw, :]) + _dot(y_ref[...], wo_ref[aw:, :])
    x1 = x_ref[...] + gate1_ref[...] * mix
    x1_ref[...] = x1
    ms = jnp.mean(x1 * x1, axis=-1, keepdims=True)
    hn2 = x1 * lax.rsqrt(ms + RMS_EPS) * g2_ref[...]
    hn2 = hn2 * (1.0 + scale2_ref[...]) + shift2_ref[...]
    hn2_ref[...] = hn2

    hh, hl = _split_bf16(hn2)
    wrh = wrh_ref[...]
    logits = _dot(hh, wrh) + _dot(hl, wrh) + _dot(hh, wrl_ref[...]) + br_ref[...]

    rows = logits.shape[0]
    lane = lax.broadcasted_iota(jnp.int32, (rows, ROUTER_PAD), 1)
    work = logits
    sel = jnp.zeros((rows, ROUTER_PAD), F32)
    picks, vals, idxs = [], [], []
    for _ in range(TOP_K):
        m = jnp.max(work, axis=-1, keepdims=True)
        idx = jnp.min(jnp.where(work == m, lane, ROUTER_PAD), axis=-1, keepdims=True)
        pick = lane == idx
        picks.append(pick)
        vals.append(m)
        idxs.append(idx)
        sel = jnp.where(pick, 1.0, sel)
        work = jnp.where(pick, -jnp.inf, work)

    rank = _dot(tril_ref[...], sel.astype(BF16)) + carry_ref[...]
    carry_ref[...] = rank[rows - 1:rows, :] + sel[rows - 1:rows, :]
    cnt_ref[...] = carry_ref[...]

    es = [jnp.exp(v - vals[0]) for v in vals]
    den = es[0]
    for e in es[1:]:
        den = den + e
    topi = jnp.zeros((rows, ROUTER_PAD), F32)
    topg = jnp.zeros((rows, ROUTER_PAD), F32)
    topr = jnp.zeros((rows, ROUTER_PAD), F32)
    for r in range(TOP_K):
        rk = jnp.sum(jnp.where(picks[r], rank, 0.0), axis=-1, keepdims=True)
        topi = jnp.where(lane == r, idxs[r].astype(F32), topi)
        topg = jnp.where(lane == r, es[r] / den, topg)
        topr = jnp.where(lane == r, rk, topr)
    topi_ref[...] = topi
    topg_ref[...] = topg
    topr_ref[...] = topr


def postmix(x, o_attn, y_ssm, w_out_b, gate1, g2, scale2, shift2, wr_hi, wr_lo, br, rows):
    n, d = x.shape
    aw = o_attn.shape[1]
    tok = lambda width: pl.BlockSpec((rows, width), lambda i: (i, 0))
    const = lambda a: pl.BlockSpec(a.shape, lambda i: (0,) * a.ndim)
    r = lax.broadcasted_iota(jnp.int32, (rows, rows), 0)
    c = lax.broadcasted_iota(jnp.int32, (rows, rows), 1)
    tril = jnp.where(c < r, 1.0, 0.0).astype(BF16)
    lanes_out = jax.ShapeDtypeStruct((n, ROUTER_PAD), F32)
    return pl.pallas_call(
        functools.partial(_postmix_kernel, aw=aw),
        grid=(n // rows,),
        in_specs=[tok(d), tok(aw), tok(y_ssm.shape[1]), const(w_out_b),
                  _row_spec(gate1.shape[0], rows, d), const(g2),
                  _row_spec(scale2.shape[0], rows, d), _row_spec(shift2.shape[0], rows, d),
                  const(wr_hi), const(wr_lo), const(br), const(tril)],
        out_specs=[tok(d), tok(d), tok(ROUTER_PAD), tok(ROUTER_PAD), tok(ROUTER_PAD),
                   pl.BlockSpec((1, ROUTER_PAD), lambda i: (0, 0))],
        out_shape=[jax.ShapeDtypeStruct((n, d), F32), jax.ShapeDtypeStruct((n, d), F32),
                   lanes_out, lanes_out, lanes_out, jax.ShapeDtypeStruct((1, ROUTER_PAD), F32)],
        scratch_shapes=[pltpu.VMEM((1, ROUTER_PAD), F32)],
        compiler_params=_cparams(("arbitrary",)),
        name="postmix",
    )(x, o_attn, y_ssm, w_out_b, gate1, g2, scale2, shift2, wr_hi, wr_lo, br, tril)


def _wait_rows(ref, n_copy, sem):
    span = ref.at[pl.ds(0, n_copy)]
    pltpu.make_async_copy(span, span, sem).wait()


def _dispatch_kernel(dest_ref, hn_ref, xr_in, xr_hbm, stage, sems, *, tile):
    del xr_in
    i = pl.program_id(0)
    slot = i % 2
    n_copy = tile * TOP_K
    stage[slot] = hn_ref[...]

    def start(t, c):
        for k in range(TOP_K):
            pltpu.make_async_copy(stage.at[slot, t], xr_hbm.at[dest_ref[t * TOP_K + k]], sems.at[slot]).start()
        return c

    lax.fori_loop(0, tile, start, 0)

    @pl.when(i > 0)
    def _():
        _wait_rows(xr_hbm, n_copy, sems.at[1 - slot])

    @pl.when(i == pl.num_programs(0) - 1)
    def _():
        _wait_rows(xr_hbm, n_copy, sems.at[slot])


def dispatch(hn3, dest_flat, n_rows, tile):
    n = hn3.shape[0]
    xr0 = jnp.zeros((n_rows,) + hn3.shape[1:], hn3.dtype)
    return pl.pallas_call(
        functools.partial(_dispatch_kernel, tile=tile),
        grid=(n // tile,),
        in_specs=[pl.BlockSpec((tile * TOP_K,), lambda i: (i,), memory_space=pltpu.SMEM),
                  pl.BlockSpec((tile,) + hn3.shape[1:], lambda i: (i, 0, 0)),
                  pl.BlockSpec(memory_space=pl.ANY)],
        out_specs=pl.BlockSpec(memory_space=pl.ANY),
        out_shape=jax.ShapeDtypeStruct(xr0.shape, xr0.dtype),
        scratch_shapes=[pltpu.VMEM((2, tile) + hn3.shape[1:], hn3.dtype), pltpu.SemaphoreType.DMA((2,))],
        input_output_aliases={2: 0},
        compiler_params=_cparams(("arbitrary",)),
        name="moe_dispatch",
    )(dest_flat, hn3, xr0)


def _expert_kernel(be_ref, nb_ref, x_ref, wgu_ref, bgu_ref, wd_ref, bd_ref, y_ref, wgu_b, wd_b, *, ff):
    i = pl.program_id(0)

    @pl.when(i < nb_ref[0])
    def _():
        prev = be_ref[jnp.maximum(i - 1, 0)]

        @pl.when((i == 0) | (be_ref[i] != prev))
        def _():
            wgu_b[...] = wgu_ref[0].astype(BF16)
            wd_b[...] = wd_ref[0].astype(BF16)

        gu = _dot(x_ref[...].astype(BF16), wgu_b[...]) + bgu_ref[0]
        gate = jnp.minimum(gu[:, :ff], SWIGLU_LIMIT)
        up = jnp.clip(gu[:, ff:], -SWIGLU_LIMIT, SWIGLU_LIMIT)
        act = (up + 1.0) * (gate * jax.nn.sigmoid(SWIGLU_ALPHA * gate))
        y_ref[...] = _dot(act.astype(BF16), wd_b[...]) + bd_ref[0]

    @pl.when(i >= nb_ref[0])
    def _():
        y_ref[...] = jnp.zeros_like(y_ref)


def experts(xr, block_e, n_used, w_gu, b_gu, w_down, b_down, bm):
    n_rows, d = xr.shape
    ne, _, ff2 = w_gu.shape
    ff = ff2 // 2
    grid_spec = pltpu.PrefetchScalarGridSpec(
        num_scalar_prefetch=2,
        grid=(n_rows // bm,),
        in_specs=[pl.BlockSpec((bm, d), lambda i, be, nb: (i, 0)),
                  pl.BlockSpec((1, d, ff2), lambda i, be, nb: (be[i], 0, 0)),
                  pl.BlockSpec((1, 1, ff2), lambda i, be, nb: (be[i], 0, 0)),
                  pl.BlockSpec((1, ff, d), lambda i, be, nb: (be[i], 0, 0)),
                  pl.BlockSpec((1, 1, d), lambda i, be, nb: (be[i], 0, 0))],
        out_specs=pl.BlockSpec((bm, d), lambda i, be, nb: (i, 0)),
        scratch_shapes=[pltpu.VMEM((d, ff2), BF16), pltpu.VMEM((ff, d), BF16)],
    )
    return pl.pallas_call(
        functools.partial(_expert_kernel, ff=ff),
        grid_spec=grid_spec,
        out_shape=jax.ShapeDtypeStruct((n_rows, d), F32),
        compiler_params=_cparams(("arbitrary",)),
        name="moe_experts",
    )(block_e, n_used, xr, w_gu, b_gu.reshape(ne, 1, ff2), w_down, b_down.reshape(ne, 1, d))


def _combine_kernel(dest_ref, dest_next_ref, yr_hbm, gx_ref, x1_ref, gate2_ref, gf_ref, out_ref, buf, sems,
                    *, tile, final_norm):
    i = pl.program_id(0)
    slot = i % 2
    n_copy = tile * TOP_K

    def gather(idx_ref, s):
        def start(t, c):
            for k in range(TOP_K):
                pltpu.make_async_copy(yr_hbm.at[idx_ref[t * TOP_K + k]], buf.at[s, t, k], sems.at[s]).start()
            return c

        lax.fori_loop(0, tile, start, 0)

    @pl.when(i == 0)
    def _():
        gather(dest_ref, slot)

    @pl.when(i + 1 < pl.num_programs(0))
    def _():
        gather(dest_next_ref, 1 - slot)

    rows = buf.at[slot]
    _wait_rows(yr_hbm, n_copy, sems.at[slot])

    y = None
    for k in range(TOP_K):
        term = gx_ref[:, k:k + 1, :] * rows[:, k]
        y = term if y is None else y + term
    x2 = x1_ref[...] + gate2_ref[...] * y
    if final_norm:
        ssq = jnp.sum(jnp.sum(x2 * x2, axis=2, keepdims=True), axis=1, keepdims=True)
        inv = lax.rsqrt(ssq / (SUBLANES * LANES) + RMS_EPS)
        x2 = x2 * inv * gf_ref[...]
    out_ref[...] = x2


def combine(yr3, dest_flat, gx, x1_3, gate2_3, gf3, tile, final_norm):
    n = x1_3.shape[0]
    tok = pl.BlockSpec((tile, SUBLANES, LANES), lambda i: (i, 0, 0))
    gate_spec = tok if gate2_3.shape[0] == n else pl.BlockSpec((1, SUBLANES, LANES), lambda i: (0, 0, 0))
    steps = n // tile
    return pl.pallas_call(
        functools.partial(_combine_kernel, tile=tile, final_norm=final_norm),
        grid=(steps,),
        in_specs=[pl.BlockSpec((tile * TOP_K,), lambda i: (i,), memory_space=pltpu.SMEM),
                  pl.BlockSpec((tile * TOP_K,), lambda i: (jnp.minimum(i + 1, steps - 1),), memory_space=pltpu.SMEM),
                  pl.BlockSpec(memory_space=pl.ANY),
                  pl.BlockSpec((tile, TOP_K, LANES), lambda i: (i, 0, 0)),
                  tok, gate_spec, pl.BlockSpec((1, SUBLANES, LANES), lambda i: (0, 0, 0))],
        out_specs=tok,
        out_shape=jax.ShapeDtypeStruct(x1_3.shape, F32),
        scratch_shapes=[pltpu.VMEM((2, tile, TOP_K, SUBLANES, LANES), F32), pltpu.SemaphoreType.DMA((2,))],
        compiler_params=_cparams(("arbitrary",)),
        name="moe_combine",
    )(dest_flat, dest_flat, yr3, gx, x1_3, gate2_3, gf3)


def moe_block(x1, hn2, topi, topg, topr, cnt, gate2, norm_f_g, final_norm, w_gu, b_gu, w_down, b_down, bm, tile):
    n, d = x1.shape
    ne = w_gu.shape[0]
    assert d == SUBLANES * LANES
    counts = cnt[0, :ne].astype(jnp.int32)
    padded = ((counts + bm - 1) // bm) * bm
    pend = jnp.cumsum(padded)
    pstart = pend - padded
    idx = topi[:, :TOP_K].astype(jnp.int32)
    onehot = (idx[..., None] == jnp.arange(ne, dtype=jnp.int32)).astype(jnp.int32)
    dest = jnp.sum(onehot * pstart, axis=-1) + topr[:, :TOP_K].astype(jnp.int32)
    dest_flat = dest.reshape(-1)
    n_rows = -(-(n * TOP_K) // bm) * bm + ne * bm
    n_blocks = n_rows // bm
    starts = jnp.arange(n_blocks, dtype=jnp.int32) * bm
    block_e = jnp.minimum(jnp.sum((starts[:, None] >= pend[None, :]).astype(jnp.int32), axis=1), ne - 1)
    n_used = (pend[-1:] // bm).astype(jnp.int32)
    last_e = block_e[jnp.maximum(n_used[0] - 1, 0)]
    block_e = jnp.where(jnp.arange(n_blocks) < n_used[0], block_e, last_e).astype(jnp.int32)

    xr = dispatch(hn2.reshape(n, SUBLANES, LANES), dest_flat, n_rows, tile)
    yr = experts(xr.reshape(n_rows, d), block_e, n_used, w_gu, b_gu, w_down, b_down, bm)
    gx = jnp.broadcast_to(topg[:, :TOP_K, None], (n, TOP_K, LANES))
    gate2_3 = gate2.reshape(gate2.shape[0], SUBLANES, LANES)
    out = combine(yr.reshape(n_rows, SUBLANES, LANES), dest_flat, gx, x1.reshape(n, SUBLANES, LANES),
                  gate2_3, norm_f_g.reshape(1, SUBLANES, LANES), tile, final_norm)
    return out.reshape(n, d)


def _group_forward(x, mod, attend, h0_lanes, t_per_seq, lw, mats, norm_f_g, final_norm, rows):
    n, d = x.shape
    rows = min(rows, n)
    (norm1_g, norm2_g, w_in_b, w_out_b, w_glu_b, b_glu, d_vec, wr_hi, wr_lo, br, w_gu, b_gu, w_down, b_down) = lw
    kmat, smat, rmat, decay_re, decay_im = mats
    shift1, scale1, gate1, shift2, scale2, gate2 = [mod[:, j * d:(j + 1) * d] for j in range(6)]
    aw = (w_in_b.shape[1] - w_glu_b.shape[0]) // 3
    q_dtype = BF16 if h0_lanes is None else F32
    q, k, v, kb, vb, u = inproj(x, shift1, scale1, norm1_g, w_in_b, aw, q_dtype, rows)
    o_attn = attend(q, k, v, kb, vb)

    sw = u.shape[1]
    nb = sw // LANES
    u2 = u.reshape(n // SSM_CHUNK, SSM_CHUNK * sw)
    nc = u2.shape[0]
    crow = min(256, nc)
    if h0_lanes is None:
        s = ssm_state(u2, smat, crow)
        hs, hend = ssm_scan(s, decay_re, decay_im, crow, nb)
    else:
        assert t_per_seq == SSM_CHUNK
        hs = h0_lanes
        hend = ssm_state(u2, smat, crow, h0_lanes, decay_re, decay_im)
    y_ssm = ssm_out(u2, hs, kmat, rmat, d_vec, w_glu_b, b_glu, crow).reshape(n, sw)

    x1, hn2, topi, topg, topr, cnt = postmix(x, o_attn, y_ssm, w_out_b, gate1, norm2_g, scale2, shift2,
                                             wr_hi, wr_lo, br, rows)
    y = moe_block(x1, hn2, topi, topg, topr, cnt, gate2, norm_f_g, final_norm, w_gu, b_gu, w_down, b_down,
                  bm=256, tile=256)
    return y, k, v, hend


def kernel(x_prompt, x_sample, c_prompt, c_sample, cache_k, cache_v, state_ssm_re, state_ssm_im, page_table, norm1_g, norm2_g, w_ada, b_ada, w_in, w_out, sb_bias, lam_re, lam_im, log_dt, ssm_b_re, ssm_b_im, ssm_c_re, ssm_c_im, ssm_d, w_glu, b_glu, w_router, b_router, w_gu, b_gu, w_down, b_down, norm_f_g):
    depth = w_in.shape[0]
    bp, tp, d = x_prompt.shape
    bs, ts, _ = x_sample.shape
    assert bp == 1, "the prompt group is handled as one long sequence"
    n_heads = cache_k.shape[3]
    aw = n_heads * HEAD_DIM
    g, p = lam_re.shape[1:]
    nb = g // GROUP_BLOCK
    ne = w_router.shape[-1]
    n_pool, page = cache_k.shape[1:3]
    rows = 512

    xp = x_prompt.reshape(bp * tp, d)
    xs = x_sample.reshape(bs * ts, d)
    n_c = bp + bs
    c_all = jnp.concatenate([c_prompt, c_sample, jnp.zeros((-n_c % SUBLANES, d), F32)], axis=0)

    outs = {name: [] for name in ("kp", "vp", "rp", "ip", "ks", "vs", "rs", "is")}
    for l in range(depth):
        mod = ada_mod(c_all, w_ada[l], b_ada[l])
        mod_p = mod[:bp]
        mod_s = jnp.repeat(mod[bp:n_c], ts, axis=0)
        mats = _ssm_mats(lam_re[l], lam_im[l], log_dt[l], ssm_b_re[l], ssm_b_im[l], ssm_c_re[l], ssm_c_im[l])
        wr = jnp.pad(w_router[l], ((0, 0), (0, ROUTER_PAD - ne)))
        wr_hi = wr.astype(BF16)
        wr_lo = (wr - wr_hi.astype(F32)).astype(BF16)
        br = jnp.concatenate([b_router[l], jnp.full((ROUTER_PAD - ne,), NEG_BIG, F32)]).reshape(1, ROUTER_PAD)
        lw = (norm1_g[l].reshape(1, d), norm2_g[l].reshape(1, d), w_in[l].astype(BF16), w_out[l].astype(BF16),
              w_glu[l].astype(BF16), b_glu[l].reshape(1, -1), ssm_d[l].reshape(1, -1), wr_hi, wr_lo, br,
              w_gu[l], b_gu[l], w_down[l], b_down[l])
        bias = sb_bias[l]

        def attend_p(q, k, v, kb, vb):
            return attn_prompt(q, kb, vb, bias, nsb=min(8, tp // KEY_BLOCK))

        ck = cache_k[l].transpose(0, 2, 3, 1).reshape(n_pool, aw, page)
        cv = cache_v[l].transpose(0, 2, 3, 1).reshape(n_pool, aw, page)

        def attend_s(q, k, v, kb, vb):
            return attn_sample(q, k, v, ck, cv, page_table, bias, ts, pages_per_step=8)

        h0 = _state_to_lanes(state_ssm_re[l], state_ssm_im[l], nb)
        xp, kp, vp, hp = _group_forward(xp, mod_p, attend_p, None, tp, lw, mats, norm_f_g, l == depth - 1, rows)
        xs, ks, vs, hs = _group_forward(xs, mod_s, attend_s, h0, ts, lw, mats, norm_f_g, l == depth - 1, rows)
        rp, ip = _lanes_to_state(hp, g, p)
        rs, is_ = _lanes_to_state(hs, g, p)
        outs["kp"].append(kp.reshape(bp, tp, n_heads, HEAD_DIM))
        outs["vp"].append(vp.reshape(bp, tp, n_heads, HEAD_DIM))
        outs["rp"].append(rp)
        outs["ip"].append(ip)
        outs["ks"].append(ks.reshape(bs, ts, n_heads, HEAD_DIM))
        outs["vs"].append(vs.reshape(bs, ts, n_heads, HEAD_DIM))
        outs["rs"].append(rs)
        outs["is"].append(is_)
    st = lambda name: jnp.stack(outs[name])
    return (xp.reshape(bp, tp, d), xs.reshape(bs, ts, d), st("kp"), st("vp"), st("rp"), st("ip"),
            st("ks"), st("vs"), st("rs"), st("is"))
```

```python
import functools
import math

import jax
import jax.numpy as jnp
from jax import lax
from jax.experimental import pallas as pl
from jax.experimental.pallas import tpu as pltpu

F32 = jnp.float32
BF16 = jnp.bfloat16

HEAD_DIM = 64
SSM_GROUP = 16
TOP_K = 4
SWIGLU_LIMIT = 7.0
SWIGLU_ALPHA = 1.702
RMS_EPS = 1e-6

LANES = 128
SUBLANES = 8
VMEM_LIMIT_BYTES = 56 * 1024 * 1024

KEY_BLOCK = LANES
SSM_CHUNK = 8
GROUP_BLOCK = LANES // SSM_GROUP
ROUTER_PAD = LANES
NEG_BIG = -1e30
EXP_CAP = 1e30


def _cparams(semantics):
    return pltpu.CompilerParams(dimension_semantics=semantics, vmem_limit_bytes=VMEM_LIMIT_BYTES)


def _dot(a, b):
    return jnp.dot(a, b, preferred_element_type=F32)


def _dot_nt(a, b):
    return lax.dot_general(a, b, (((1,), (1,)), ((), ())), preferred_element_type=F32)


def _split_bf16(x):
    hi = x.astype(BF16)
    lo = (x - hi.astype(F32)).astype(BF16)
    return hi, lo


def _ada_kernel(c_ref, w_ref, b_ref, o_ref):
    o_ref[...] = _dot(c_ref[...].astype(BF16), w_ref[...].astype(BF16)) + b_ref[...]


def ada_mod(c, w_ada, b_ada):
    n, d = c.shape
    n_out = w_ada.shape[1]
    return pl.pallas_call(
        _ada_kernel,
        grid=(n_out // d,),
        in_specs=[pl.BlockSpec((n, d), lambda j: (0, 0)),
                  pl.BlockSpec((d, d), lambda j: (0, j)),
                  pl.BlockSpec((1, d), lambda j: (0, j))],
        out_specs=pl.BlockSpec((n, d), lambda j: (0, j)),
        out_shape=jax.ShapeDtypeStruct((n, n_out), F32),
        compiler_params=_cparams(("arbitrary",)),
        name="ada_mod",
    )(c, w_ada, b_ada.reshape(1, n_out))


def _inproj_kernel(x_ref, shift_ref, scale_ref, g_ref, w_ref, q_ref, k_ref, v_ref, kb_ref, vb_ref, u_ref, *, aw):
    x = x_ref[...]
    ms = jnp.mean(x * x, axis=-1, keepdims=True)
    hn = x * lax.rsqrt(ms + RMS_EPS) * g_ref[...]
    hn = hn * (1.0 + scale_ref[...]) + shift_ref[...]
    proj = _dot(hn.astype(BF16), w_ref[...])
    q_ref[...] = (proj[:, :aw] * (HEAD_DIM ** -0.5)).astype(q_ref.dtype)
    k = proj[:, aw:2 * aw]
    v = proj[:, 2 * aw:3 * aw]
    k_ref[...] = k
    v_ref[...] = v
    kb_ref[...] = k.astype(BF16)
    vb_ref[...] = v.astype(BF16)
    u_ref[...] = proj[:, 3 * aw:]


def _row_spec(n_mod_rows, rows, d):
    if n_mod_rows == 1:
        return pl.BlockSpec((1, d), lambda i: (0, 0))
    return pl.BlockSpec((rows, d), lambda i: (i, 0))


def inproj(x, shift, scale, g, w_in_b, aw, q_dtype, rows):
    n, d = x.shape
    pw = w_in_b.shape[1]
    sw = pw - 3 * aw
    tok = lambda width: pl.BlockSpec((rows, width), lambda i: (i, 0))
    return pl.pallas_call(
        functools.partial(_inproj_kernel, aw=aw),
        grid=(n // rows,),
        in_specs=[tok(d), _row_spec(shift.shape[0], rows, d), _row_spec(scale.shape[0], rows, d),
                  pl.BlockSpec((1, d), lambda i: (0, 0)),
                  pl.BlockSpec((d, pw), lambda i: (0, 0))],
        out_specs=[tok(aw), tok(aw), tok(aw), tok(aw), tok(aw), tok(sw)],
        out_shape=[jax.ShapeDtypeStruct((n, aw), q_dtype),
                   jax.ShapeDtypeStruct((n, aw), F32), jax.ShapeDtypeStruct((n, aw), F32),
                   jax.ShapeDtypeStruct((n, aw), BF16), jax.ShapeDtypeStruct((n, aw), BF16),
                   jax.ShapeDtypeStruct((n, sw), F32)],
        compiler_params=_cparams(("arbitrary",)),
        name="inproj",
    )(x, shift, scale, g, w_in_b)


def _tri_ones():
    j = lax.broadcasted_iota(jnp.int32, (KEY_BLOCK, 2 * KEY_BLOCK), 0)
    s = lax.broadcasted_iota(jnp.int32, (KEY_BLOCK, 2 * KEY_BLOCK), 1)
    return jnp.where((j > s) | (s >= KEY_BLOCK), 1.0, 0.0).astype(BF16)


def _sb_logits(z, mask):
    drop = jnp.maximum(jnp.log(1.0 + jnp.minimum(jnp.exp(z), EXP_CAP)), z)
    if mask is not None:
        drop = jnp.where(mask, drop, 0.0)
    return z - drop, drop.astype(BF16)


def _sb_finish(log_beta, drop_b, carry, tri, mask):
    groups = drop_b.shape[1] // KEY_BLOCK
    later, total = [], []
    for g in range(groups):
        cs = _dot(drop_b[:, g * KEY_BLOCK:(g + 1) * KEY_BLOCK], tri)
        later.append(cs[:, :KEY_BLOCK])
        total.append(cs[:, KEY_BLOCK:])
    later = jnp.concatenate(later, axis=1) if groups > 1 else later[0]
    total = jnp.concatenate(total, axis=1) if groups > 1 else total[0]
    a = jnp.exp(log_beta - (later + carry))
    if mask is not None:
        a = jnp.where(mask, a, 0.0)
    return a.astype(BF16), total


def _sb_weights(z, carry, tri, mask):
    if mask is not None:
        mask = jnp.concatenate([mask] * (z.shape[1] // KEY_BLOCK), axis=1)
    log_beta, drop_b = _sb_logits(z, mask)
    return _sb_finish(log_beta, drop_b, carry, tri, mask)


def _attn_prompt_kernel(kbias_ref, q_ref, k_ref, v_ref, tri_ref, o_ref, acc_ref, carry_ref, lb_scr, drop_scr,
                        qext_scr, *, nsb):
    it = pl.program_id(1)
    tq = nsb * KEY_BLOCK
    tri = tri_ref[...]
    ones_lanes = lax.broadcasted_iota(jnp.int32, (tq, LANES), 1) < 2
    qext_scr[:, :LANES] = q_ref[...]
    qext_scr[:, LANES:] = jnp.where(ones_lanes, 1.0, 0.0).astype(BF16)
    kbias = kbias_ref[0]
    acc_ref[...] = jnp.zeros_like(acc_ref)
    carry_ref[...] = jnp.zeros_like(carry_ref)
    lane = lax.broadcasted_iota(jnp.int32, (KEY_BLOCK, LANES), 1)
    first_head = lane < HEAD_DIM

    def stacked(ref, block_index):
        st = pl.multiple_of(block_index * KEY_BLOCK, KEY_BLOCK)
        blk = ref[pl.ds(st, KEY_BLOCK), :]
        zero = jnp.zeros_like(blk)
        return jnp.concatenate([jnp.where(first_head, blk, zero), jnp.where(first_head, zero, blk)], axis=0)

    for c in reversed(range(nsb)):
        r0 = c * KEY_BLOCK
        rr = lax.broadcasted_iota(jnp.int32, (tq - r0, LANES), 0)
        ll = lax.broadcasted_iota(jnp.int32, (tq - r0, LANES), 1)
        mask = (rr >= KEY_BLOCK) | (ll < rr)
        z = _dot_nt(qext_scr[r0:, :], jnp.concatenate([stacked(k_ref, it * nsb + c), kbias], axis=1))
        a, tot = _sb_weights(z, carry_ref[r0:, :], tri, mask)
        acc_ref[r0:, :] += _dot(a, stacked(v_ref, it * nsb + c))
        carry_ref[r0:, :] += tot

    n_full = it * nsb

    def first_half(block_index, slot):
        z = _dot_nt(qext_scr[...], jnp.concatenate([stacked(k_ref, block_index), kbias], axis=1))
        log_beta, drop_b = _sb_logits(z, None)
        lb_scr[slot] = log_beta
        drop_scr[slot] = drop_b

    def second_half(block_index, slot):
        a, tot = _sb_finish(lb_scr[slot], drop_scr[slot], carry_ref[...], tri, None)
        acc_ref[...] += _dot(a, stacked(v_ref, block_index))
        carry_ref[...] += tot

    @pl.when(n_full > 0)
    def _():
        per_trip = math.gcd(nsb, 4)
        first_half(n_full - 1, 0)

        def body(jj, c):
            blk = n_full - 1 - per_trip * jj
            for u in range(per_trip):
                first_half(blk - u - 1, (u + 1) % 2)
                second_half(blk - u, u % 2)
            return c

        lax.fori_loop(0, n_full // per_trip - 1, body, 0)
        for u in range(per_trip - 1):
            first_half(per_trip - 2 - u, (u + 1) % 2)
            second_half(per_trip - 1 - u, u % 2)
        second_half(0, (per_trip - 1) % 2)

    o_ref[...] = acc_ref[...].astype(o_ref.dtype)


def attn_prompt(q, kb, vb, bias, nsb):
    n, aw = q.shape
    tq = nsb * KEY_BLOCK
    n_pairs = aw // LANES
    b_hi, b_lo = _split_bf16(bias.astype(F32))
    lane = jnp.arange(LANES)
    kbias = jnp.where(lane == 0, b_hi[:, None, None], jnp.where(lane == 1, b_lo[:, None, None], 0)).astype(BF16)
    kbias = jnp.broadcast_to(kbias, (bias.shape[0], KEY_BLOCK, LANES)).reshape(n_pairs, 2 * KEY_BLOCK, LANES)
    tri = _tri_ones()
    return pl.pallas_call(
        functools.partial(_attn_prompt_kernel, nsb=nsb),
        grid=(n_pairs, n // tq),
        in_specs=[pl.BlockSpec((1, 2 * KEY_BLOCK, LANES), lambda hp, i: (hp, 0, 0)),
                  pl.BlockSpec((tq, LANES), lambda hp, i: (i, hp)),
                  pl.BlockSpec((n, LANES), lambda hp, i: (0, hp)),
                  pl.BlockSpec((n, LANES), lambda hp, i: (0, hp)),
                  pl.BlockSpec(tri.shape, lambda hp, i: (0, 0))],
        out_specs=pl.BlockSpec((tq, LANES), lambda hp, i: (i, hp)),
        out_shape=jax.ShapeDtypeStruct((n, aw), BF16),
        scratch_shapes=[pltpu.VMEM((tq, LANES), F32), pltpu.VMEM((tq, 2 * KEY_BLOCK), F32),
                        pltpu.VMEM((2, tq, 2 * KEY_BLOCK), F32), pltpu.VMEM((2, tq, 2 * KEY_BLOCK), BF16),
                        pltpu.VMEM((tq, 2 * LANES), BF16)],
        compiler_params=_cparams(("arbitrary", "arbitrary")),
        name="attn_prompt",
    )(kbias, q, kb, vb, tri)


def _attn_sample_kernel(pt_ref, q_ref, kn_ref, vn_ref, *rest, pages_per_step, n_heads, t_new):
    del pt_ref
    k_refs = rest[:pages_per_step]
    v_refs = rest[pages_per_step:2 * pages_per_step]
    tri_ref, bias_ref, o_ref, qbd_ref, acc_ref, carry_ref = rest[2 * pages_per_step:]
    p = pl.program_id(1)
    m = n_heads * t_new
    aw = n_heads * HEAD_DIM
    tri = tri_ref[...]
    bias = bias_ref[...]

    @pl.when(p == 0)
    def _():
        q = q_ref[...].astype(BF16)
        qt = jnp.concatenate([q] * n_heads, axis=0)
        rw = lax.broadcasted_iota(jnp.int32, (m, aw), 0)
        ln = lax.broadcasted_iota(jnp.int32, (m, aw), 1)
        qbd = jnp.where(ln // HEAD_DIM == rw // t_new, qt, jnp.zeros_like(qt))
        qbd_ref[...] = qbd
        pad = jnp.zeros((KEY_BLOCK - t_new, aw), F32)
        kn = jnp.concatenate([kn_ref[...], pad], axis=0).astype(BF16)
        vn = jnp.concatenate([vn_ref[...], pad], axis=0).astype(BF16)
        r2 = lax.broadcasted_iota(jnp.int32, (m, KEY_BLOCK), 0)
        l2 = lax.broadcasted_iota(jnp.int32, (m, KEY_BLOCK), 1)
        mask = l2 < (r2 % t_new)
        z = _dot_nt(qbd, kn) + bias
        a, tot = _sb_weights(z, 0.0, tri, mask)
        acc_ref[...] = _dot(a, vn)
        carry_ref[...] = tot

    kt = jnp.concatenate([r[0].astype(BF16) for r in k_refs], axis=1)
    vt = jnp.concatenate([r[0].astype(BF16) for r in v_refs], axis=1)
    z = _dot(qbd_ref[...], kt) + jnp.concatenate([bias] * pages_per_step, axis=1)
    log_beta, drop_b = _sb_logits(z, None)
    carry = carry_ref[...]
    shift = []
    for i in range(pages_per_step):
        cs = _dot(drop_b[:, i * KEY_BLOCK:(i + 1) * KEY_BLOCK], tri)
        shift.append(cs[:, :KEY_BLOCK] + carry)
        carry = carry + cs[:, KEY_BLOCK:]
    a = jnp.exp(log_beta - jnp.concatenate(shift, axis=1)).astype(BF16)
    acc_ref[...] += _dot_nt(a, vt)
    carry_ref[...] = carry

    @pl.when(p == pl.num_programs(1) - 1)
    def _():
        acc = acc_ref[...]
        ln = lax.broadcasted_iota(jnp.int32, (t_new, aw), 1)
        o = jnp.zeros((t_new, aw), F32)
        for h in range(n_heads):
            o = o + jnp.where(ln // HEAD_DIM == h, acc[h * t_new:(h + 1) * t_new, :], 0.0)
        o_ref[...] = o.astype(o_ref.dtype)


def attn_sample(q, k_new, v_new, cache_k, cache_v, page_table, bias, t_new, pages_per_step):
    n, aw = q.shape
    bsz, n_pages = page_table.shape
    n_heads = aw // HEAD_DIM
    page = cache_k.shape[2]
    assert page == KEY_BLOCK and t_new == SUBLANES and n_pages % pages_per_step == 0
    m = n_heads * t_new
    bias_rows = jnp.broadcast_to(jnp.repeat(bias, t_new)[:, None], (m, KEY_BLOCK)).astype(F32)

    def page_spec(i):
        return pl.BlockSpec((1, aw, page),
                            lambda b, p, pt: (pt[b, n_pages - 1 - (p * pages_per_step + i)], 0, 0))

    tok = pl.BlockSpec((t_new, aw), lambda b, p, pt: (b, 0))
    grid_spec = pltpu.PrefetchScalarGridSpec(
        num_scalar_prefetch=1,
        grid=(bsz, n_pages // pages_per_step),
        in_specs=[tok, tok, tok]
        + [page_spec(i) for i in range(pages_per_step)]
        + [page_spec(i) for i in range(pages_per_step)]
        + [pl.BlockSpec((KEY_BLOCK, 2 * KEY_BLOCK), lambda b, p, pt: (0, 0)),
           pl.BlockSpec((m, KEY_BLOCK), lambda b, p, pt: (0, 0))],
        out_specs=tok,
        scratch_shapes=[pltpu.VMEM((m, aw), BF16), pltpu.VMEM((m, aw), F32), pltpu.VMEM((m, KEY_BLOCK), F32)],
    )
    return pl.pallas_call(
        functools.partial(_attn_sample_kernel, pages_per_step=pages_per_step, n_heads=n_heads, t_new=t_new),
        grid_spec=grid_spec,
        out_shape=jax.ShapeDtypeStruct((n, aw), BF16),
        compiler_params=_cparams(("arbitrary", "arbitrary")),
        name="attn_sample",
    )(page_table, q, k_new, v_new, *([cache_k] * pages_per_step), *([cache_v] * pages_per_step),
      _tri_ones(), bias_rows)


def _ssm_mats(lam_re, lam_im, log_dt, b_re, b_im, c_re, c_im):
    g, p = lam_re.shape
    hh = b_re.shape[-1]
    el = SSM_CHUNK
    nb = g // GROUP_BLOCK
    dt = jnp.exp(log_dt)[:, None]
    ar, ai = lam_re * dt, lam_im * dt
    lbr, lbi = jnp.exp(ar) * jnp.cos(ai), jnp.exp(ar) * jnp.sin(ai)
    den = lam_re * lam_re + lam_im * lam_im
    fr = ((lbr - 1.0) * lam_re + lbi * lam_im) / den
    fi = (lbi * lam_re - (lbr - 1.0) * lam_im) / den
    bbr = fr[..., None] * b_re - fi[..., None] * b_im
    bbi = fr[..., None] * b_im + fi[..., None] * b_re
    n = jnp.arange(el + 1, dtype=F32)[:, None, None]
    pr = jnp.exp(ar[None] * n) * jnp.cos(ai[None] * n)
    pi = jnp.exp(ar[None] * n) * jnp.sin(ai[None] * n)
    eye = jnp.eye(GROUP_BLOCK, dtype=F32)

    cpr = c_re[None] * pr[:el, :, None, :] - c_im[None] * pi[:el, :, None, :]
    cpi = c_re[None] * pi[:el, :, None, :] + c_im[None] * pr[:el, :, None, :]
    kd = jnp.einsum('dgip,gpj->dgij', cpr, bbr) - jnp.einsum('dgip,gpj->dgij', cpi, bbi)
    kmat = jnp.einsum('dkgij,gh->dkgjhi', kd.reshape(el, nb, GROUP_BLOCK, hh, hh), eye)
    kmat = kmat.reshape(el, nb, LANES, LANES)

    rev = pr[el - 1 - jnp.arange(el)], pi[el - 1 - jnp.arange(el)]
    scr = rev[0][..., None] * bbr[None] - rev[1][..., None] * bbi[None]
    sci = rev[0][..., None] * bbi[None] + rev[1][..., None] * bbr[None]

    def blk_s(a):
        a = jnp.einsum('skgpj,gh->skgjhp', a.reshape(el, nb, GROUP_BLOCK, p, hh), eye)
        return a.reshape(el, nb, LANES, GROUP_BLOCK * p)

    smat = jnp.concatenate([blk_s(scr), blk_s(sci)], axis=-1)

    c1r = c_re[None] * pr[1:, :, None, :] - c_im[None] * pi[1:, :, None, :]
    c1i = c_re[None] * pi[1:, :, None, :] + c_im[None] * pr[1:, :, None, :]

    def blk_r(a):
        a = jnp.einsum('tkgip,gh->tkgphi', a.reshape(el, nb, GROUP_BLOCK, hh, p), eye)
        return a.reshape(el, nb, GROUP_BLOCK * p, LANES)

    rmat = jnp.concatenate([blk_r(c1r), -blk_r(c1i)], axis=-2)

    decay_re = pr[el].reshape(1, g * p)
    decay_im = pi[el].reshape(1, g * p)
    return kmat.astype(BF16), smat.astype(BF16), rmat.astype(BF16), decay_re, decay_im


def _state_to_lanes(h_re, h_im, nb):
    b = h_re.shape[0]
    return jnp.stack([h_re.reshape(b, nb, -1), h_im.reshape(b, nb, -1)], axis=2).reshape(b, -1)


def _lanes_to_state(h, g, p):
    b = h.shape[0]
    nb = g // GROUP_BLOCK
    h = h.reshape(b, nb, 2, GROUP_BLOCK, p)
    return h[:, :, 0].reshape(b, g, p), h[:, :, 1].reshape(b, g, p)


def _ssm_state_kernel(*refs, has_h0, nb, sw, half):
    if has_h0:
        u_ref, smat_ref, h0_ref, dre_ref, dim_ref, s_ref = refs
    else:
        u_ref, smat_ref, s_ref = refs
    for k in range(nb):
        acc = None
        for s in range(SSM_CHUNK):
            ub = u_ref[:, s * sw + k * LANES: s * sw + (k + 1) * LANES].astype(BF16)
            d = _dot(ub, smat_ref[s, k])
            acc = d if acc is None else acc + d
        base = 2 * half * k
        if has_h0:
            hr = h0_ref[:, base:base + half]
            hi = h0_ref[:, base + half:base + 2 * half]
            dr = dre_ref[:, k * half:(k + 1) * half]
            di = dim_ref[:, k * half:(k + 1) * half]
            s_ref[:, base:base + half] = acc[:, :half] + dr * hr - di * hi
            s_ref[:, base + half:base + 2 * half] = acc[:, half:] + dr * hi + di * hr
        else:
            s_ref[:, base:base + 2 * half] = acc


def ssm_state(u2, smat, rows, h0=None, decay_re=None, decay_im=None):
    nc, width = u2.shape
    sw = width // SSM_CHUNK
    nb = sw // LANES
    half = smat.shape[-1] // 2
    sl = nb * 2 * half
    has_h0 = h0 is not None
    row = lambda w: pl.BlockSpec((rows, w), lambda i: (i, 0))
    in_specs = [row(width), pl.BlockSpec(smat.shape, lambda i: (0, 0, 0, 0))]
    args = [u2, smat]
    if has_h0:
        in_specs += [row(sl), pl.BlockSpec((1, nb * half), lambda i: (0, 0)),
                     pl.BlockSpec((1, nb * half), lambda i: (0, 0))]
        args += [h0, decay_re, decay_im]
    return pl.pallas_call(
        functools.partial(_ssm_state_kernel, has_h0=has_h0, nb=nb, sw=sw, half=half),
        grid=(nc // rows,),
        in_specs=in_specs,
        out_specs=row(sl),
        out_shape=jax.ShapeDtypeStruct((nc, sl), F32),
        compiler_params=_cparams(("arbitrary",)),
        name="ssm_state",
    )(*args)


def _ssm_scan_kernel(s_ref, dre_ref, dim_ref, hs_ref, hend_ref, h_scr, *, nb, half):
    @pl.when(pl.program_id(0) == 0)
    def _():
        h_scr[...] = jnp.zeros_like(h_scr)

    dr = dre_ref[...]
    di = dim_ref[...]

    def body(r, h):
        hs_ref[pl.ds(r, 1), :] = h
        s = s_ref[pl.ds(r, 1), :]
        parts = []
        for k in range(nb):
            base = 2 * half * k
            hr, hi = h[:, base:base + half], h[:, base + half:base + 2 * half]
            ar, ai = dr[:, k * half:(k + 1) * half], di[:, k * half:(k + 1) * half]
            parts.append(ar * hr - ai * hi + s[:, base:base + half])
            parts.append(ar * hi + ai * hr + s[:, base + half:base + 2 * half])
        return jnp.concatenate(parts, axis=1)

    h = lax.fori_loop(0, s_ref.shape[0], body, h_scr[...])
    h_scr[...] = h
    hend_ref[...] = h


def ssm_scan(s, decay_re, decay_im, rows, nb):
    nc, sl = s.shape
    n_half_total = decay_re.shape[1]
    half = n_half_total // nb
    row = pl.BlockSpec((rows, sl), lambda i: (i, 0))
    vec = pl.BlockSpec((1, n_half_total), lambda i: (0, 0))
    return pl.pallas_call(
        functools.partial(_ssm_scan_kernel, nb=nb, half=half),
        grid=(nc // rows,),
        in_specs=[row, vec, vec],
        out_specs=[row, pl.BlockSpec((1, sl), lambda i: (0, 0))],
        out_shape=[jax.ShapeDtypeStruct((nc, sl), F32), jax.ShapeDtypeStruct((1, sl), F32)],
        scratch_shapes=[pltpu.VMEM((1, sl), F32)],
        compiler_params=_cparams(("arbitrary",)),
        name="ssm_scan",
    )(s, decay_re, decay_im)


def _gelu_exact(y):
    return 0.5 * y * (1.0 + lax.erf(y * (0.5 ** 0.5)))


def _ssm_out_kernel(u_ref, hs_ref, kmat_ref, rmat_ref, d_ref, wglu_ref, bglu_ref, y_ref, *, nb, sw, half):
    hb = [hs_ref[:, 2 * half * k:2 * half * (k + 1)].astype(BF16) for k in range(nb)]
    ub = [[u_ref[:, s * sw + k * LANES: s * sw + (k + 1) * LANES].astype(BF16) for k in range(nb)]
          for s in range(SSM_CHUNK)]
    wglu = wglu_ref[...]
    for t in range(SSM_CHUNK):
        cols = []
        for k in range(nb):
            acc = _dot(hb[k], rmat_ref[t, k])
            for s in range(t + 1):
                acc = acc + _dot(ub[s][k], kmat_ref[t - s, k])
            cols.append(acc)
        y = jnp.concatenate(cols, axis=1) + d_ref[...] * u_ref[:, t * sw:(t + 1) * sw]
        g = _gelu_exact(y)
        gate = jax.nn.sigmoid(_dot(g.astype(BF16), wglu) + bglu_ref[...])
        y_ref[:, t * sw:(t + 1) * sw] = (g * gate).astype(y_ref.dtype)


def ssm_out(u2, hs, kmat, rmat, d_vec, w_glu_b, b_glu, rows):
    nc, width = u2.shape
    sw = width // SSM_CHUNK
    nb = sw // LANES
    sl = hs.shape[1]
    half = sl // (2 * nb)
    row = lambda w: pl.BlockSpec((rows, w), lambda i: (i, 0))
    const = lambda a: pl.BlockSpec(a.shape, lambda i: (0,) * a.ndim)
    return pl.pallas_call(
        functools.partial(_ssm_out_kernel, nb=nb, sw=sw, half=half),
        grid=(nc // rows,),
        in_specs=[row(width), row(sl), const(kmat), const(rmat), const(d_vec), const(w_glu_b), const(b_glu)],
        out_specs=row(width),
        out_shape=jax.ShapeDtypeStruct((nc, width), BF16),
        compiler_params=_cparams(("arbitrary",)),
        name="ssm_out",
    )(u2, hs, kmat, rmat, d_vec, w_glu_b, b_glu)


def _postmix_kernel(x_ref, o_ref, y_ref, wo_ref, gate1_ref, g2_ref, scale2_ref, shift2_ref, wrh_ref, wrl_ref,
                    br_ref, tril_ref, x1_ref, hn2_ref, topi_ref, topg_ref, topr_ref, cnt_ref, carry_ref, *, aw):
    @pl.when(pl.program_id(0) == 0)
    def _():
        carry_ref[...] = jnp.zeros_like(carry_ref)

    mix = _dot(o_ref[...], wo_ref[:aw, :]) + _dot(y_ref[...], wo_ref[aw:, :])
    x1 = x_ref[...] + gate1_ref[...] * mix
    x1_ref[...] = x1
    ms = jnp.mean(x1 * x1, axis=-1, keepdims=True)
    hn2 = x1 * lax.rsqrt(ms + RMS_EPS) * g2_ref[...]
    hn2 = hn2 * (1.0 + scale2_ref[...]) + shift2_ref[...]
    hn2_ref[...] = hn2

    hh, hl = _split_bf16(hn2)
    wrh = wrh_ref[...]
    logits = _dot(hh, wrh) + _dot(hl, wrh) + _dot(hh, wrl_ref[...]) + br_ref[...]

    rows = logits.shape[0]
    lane = lax.broadcasted_iota(jnp.int32, (rows, ROUTER_PAD), 1)
    work = logits
    sel = jnp.zeros((rows, ROUTER_PAD), F32)
    picks, vals, idxs = [], [], []
    for _ in range(TOP_K):
        m = jnp.max(work, axis=-1, keepdims=True)
        idx = jnp.min(jnp.where(work == m, lane, ROUTER_PAD), axis=-1, keepdims=True)
        pick = lane == idx
        picks.append(pick)
        vals.append(m)
        idxs.append(idx)
        sel = jnp.where(pick, 1.0, sel)
        work = jnp.where(pick, -jnp.inf, work)

    rank = _dot(tril_ref[...], sel.astype(BF16)) + carry_ref[...]
    carry_ref[...] = rank[rows - 1:rows, :] + sel[rows - 1:rows, :]
    cnt_ref[...] = carry_ref[...]

    es = [jnp.exp(v - vals[0]) for v in vals]
    den = es[0]
    for e in es[1:]:
        den = den + e
    topi = jnp.zeros((rows, ROUTER_PAD), F32)
    topg = jnp.zeros((rows, ROUTER_PAD), F32)
    topr = jnp.zeros((rows, ROUTER_PAD), F32)
    for r in range(TOP_K):
        rk = jnp.sum(jnp.where(picks[r], rank, 0.0), axis=-1, keepdims=True)
        topi = jnp.where(lane == r, idxs[r].astype(F32), topi)
        topg = jnp.where(lane == r, es[r] / den, topg)
        topr = jnp.where(lane == r, rk, topr)
    topi_ref[...] = topi
    topg_ref[...] = topg
    topr_ref[...] = topr


def postmix(x, o_attn, y_ssm, w_out_b, gate1, g2, scale2, shift2, wr_hi, wr_lo, br, rows):
    n, d = x.shape
    aw = o_attn.shape[1]
    tok = lambda width: pl.BlockSpec((rows, width), lambda i: (i, 0))
    const = lambda a: pl.BlockSpec(a.shape, lambda i: (0,) * a.ndim)
    r = lax.broadcasted_iota(jnp.int32, (rows, rows), 0)
    c = lax.broadcasted_iota(jnp.int32, (rows, rows), 1)
    tril = jnp.where(c < r, 1.0, 0.0).astype(BF16)
    lanes_out = jax.ShapeDtypeStruct((n, ROUTER_PAD), F32)
    return pl.pallas_call(
        functools.partial(_postmix_kernel, aw=aw),
        grid=(n // rows,),
        in_specs=[tok(d), tok(aw), tok(y_ssm.shape[1]), const(w_out_b),
                  _row_spec(gate1.shape[0], rows, d), const(g2),
                  _row_spec(scale2.shape[0], rows, d), _row_spec(shift2.shape[0], rows, d),
                  const(wr_hi), const(wr_lo), const(br), const(tril)],
        out_specs=[tok(d), tok(d), tok(ROUTER_PAD), tok(ROUTER_PAD), tok(ROUTER_PAD),
                   pl.BlockSpec((1, ROUTER_PAD), lambda i: (0, 0))],
        out_shape=[jax.ShapeDtypeStruct((n, d), F32), jax.ShapeDtypeStruct((n, d), F32),
                   lanes_out, lanes_out, lanes_out, jax.ShapeDtypeStruct((1, ROUTER_PAD), F32)],
        scratch_shapes=[pltpu.VMEM((1, ROUTER_PAD), F32)],
        compiler_params=_cparams(("arbitrary",)),
        name="postmix",
    )(x, o_attn, y_ssm, w_out_b, gate1, g2, scale2, shift2, wr_hi, wr_lo, br, tril)


def _wait_rows(ref, n_copy, sem):
    span = ref.at[pl.ds(0, n_copy)]
    pltpu.make_async_copy(span, span, sem).wait()


def _dispatch_kernel(dest_ref, hn_ref, xr_in, xr_hbm, stage, sems, *, tile):
    del xr_in
    i = pl.program_id(0)
    slot = i % 2
    n_copy = tile * TOP_K
    stage[slot] = hn_ref[...]

    def start(t, c):
        for k in range(TOP_K):
            pltpu.make_async_copy(stage.at[slot, pl.ds(t, 1)], xr_hbm.at[pl.ds(dest_ref[t * TOP_K + k], 1)],
                                  sems.at[slot]).start()
        return c

    lax.fori_loop(0, tile, start, 0)

    @pl.when(i > 0)
    def _():
        _wait_rows(xr_hbm, n_copy, sems.at[1 - slot])

    @pl.when(i == pl.num_programs(0) - 1)
    def _():
        _wait_rows(xr_hbm, n_copy, sems.at[slot])


def dispatch(hn, dest_flat, n_rows, tile):
    n, d = hn.shape
    xr0 = jnp.zeros((n_rows, d), hn.dtype)
    return pl.pallas_call(
        functools.partial(_dispatch_kernel, tile=tile),
        grid=(n // tile,),
        in_specs=[pl.BlockSpec((tile * TOP_K,), lambda i: (i,), memory_space=pltpu.SMEM),
                  pl.BlockSpec((tile, d), lambda i: (i, 0)),
                  pl.BlockSpec(memory_space=pl.ANY)],
        out_specs=pl.BlockSpec(memory_space=pl.ANY),
        out_shape=jax.ShapeDtypeStruct(xr0.shape, xr0.dtype),
        scratch_shapes=[pltpu.VMEM((2, tile, d), hn.dtype), pltpu.SemaphoreType.DMA((2,))],
        input_output_aliases={2: 0},
        compiler_params=_cparams(("arbitrary",)),
        name="moe_dispatch",
    )(dest_flat, hn, xr0)


def _expert_kernel(be_ref, nb_ref, x_ref, wgu_ref, bgu_ref, wd_ref, bd_ref, y_ref, wgu_b, wd_b, *, ff):
    i = pl.program_id(0)

    @pl.when(i < nb_ref[0])
    def _():
        prev = be_ref[jnp.maximum(i - 1, 0)]

        @pl.when((i == 0) | (be_ref[i] != prev))
        def _():
            wgu_b[...] = wgu_ref[0].astype(BF16)
            wd_b[...] = wd_ref[0].astype(BF16)

        gu = _dot(x_ref[...].astype(BF16), wgu_b[...]) + bgu_ref[0]
        gate = jnp.minimum(gu[:, :ff], SWIGLU_LIMIT)
        up = jnp.clip(gu[:, ff:], -SWIGLU_LIMIT, SWIGLU_LIMIT)
        act = (up + 1.0) * (gate * jax.nn.sigmoid(SWIGLU_ALPHA * gate))
        y_ref[...] = _dot(act.astype(BF16), wd_b[...]) + bd_ref[0]

    @pl.when(i >= nb_ref[0])
    def _():
        y_ref[...] = jnp.zeros_like(y_ref)


def experts(xr, block_e, n_used, w_gu, b_gu, w_down, b_down, bm):
    n_rows, d = xr.shape
    ne, _, ff2 = w_gu.shape
    ff = ff2 // 2
    grid_spec = pltpu.PrefetchScalarGridSpec(
        num_scalar_prefetch=2,
        grid=(n_rows // bm,),
        in_specs=[pl.BlockSpec((bm, d), lambda i, be, nb: (i, 0)),
                  pl.BlockSpec((1, d, ff2), lambda i, be, nb: (be[i], 0, 0)),
                  pl.BlockSpec((1, 1, ff2), lambda i, be, nb: (be[i], 0, 0)),
                  pl.BlockSpec((1, ff, d), lambda i, be, nb: (be[i], 0, 0)),
                  pl.BlockSpec((1, 1, d), lambda i, be, nb: (be[i], 0, 0))],
        out_specs=pl.BlockSpec((bm, d), lambda i, be, nb: (i, 0)),
        scratch_shapes=[pltpu.VMEM((d, ff2), BF16), pltpu.VMEM((ff, d), BF16)],
    )
    return pl.pallas_call(
        functools.partial(_expert_kernel, ff=ff),
        grid_spec=grid_spec,
        out_shape=jax.ShapeDtypeStruct((n_rows, d), F32),
        compiler_params=_cparams(("arbitrary",)),
        name="moe_experts",
    )(block_e, n_used, xr, w_gu, b_gu.reshape(ne, 1, ff2), w_down, b_down.reshape(ne, 1, d))


def _combine_kernel(dest_ref, dest_next_ref, yr_hbm, g_ref, x1_ref, gate2_ref, gf_ref, out_ref, buf, sems,
                    *, tile, final_norm):
    i = pl.program_id(0)
    slot = i % 2
    n_copy = tile * TOP_K

    def gather(idx_ref, s):
        def start(t, c):
            for k in range(TOP_K):
                pltpu.make_async_copy(yr_hbm.at[pl.ds(idx_ref[t * TOP_K + k], 1)], buf.at[s, k, pl.ds(t, 1)],
                                      sems.at[s]).start()
            return c

        lax.fori_loop(0, tile, start, 0)

    @pl.when(i == 0)
    def _():
        gather(dest_ref, slot)

    @pl.when(i + 1 < pl.num_programs(0))
    def _():
        gather(dest_next_ref, 1 - slot)

    _wait_rows(yr_hbm, n_copy, sems.at[slot])

    y = None
    for k in range(TOP_K):
        term = g_ref[:, k:k + 1] * buf[slot, k]
        y = term if y is None else y + term
    x2 = x1_ref[...] + gate2_ref[...] * y
    if final_norm:
        ms = jnp.mean(x2 * x2, axis=-1, keepdims=True)
        x2 = x2 * lax.rsqrt(ms + RMS_EPS) * gf_ref[...]
    out_ref[...] = x2


def combine(yr, dest_flat, topg, x1, gate2, gf, tile, final_norm):
    n, d = x1.shape
    tok = pl.BlockSpec((tile, d), lambda i: (i, 0))
    steps = n // tile
    return pl.pallas_call(
        functools.partial(_combine_kernel, tile=tile, final_norm=final_norm),
        grid=(steps,),
        in_specs=[pl.BlockSpec((tile * TOP_K,), lambda i: (i,), memory_space=pltpu.SMEM),
                  pl.BlockSpec((tile * TOP_K,), lambda i: (jnp.minimum(i + 1, steps - 1),), memory_space=pltpu.SMEM),
                  pl.BlockSpec(memory_space=pl.ANY),
                  pl.BlockSpec((tile, topg.shape[1]), lambda i: (i, 0)),
                  tok, _row_spec(gate2.shape[0], tile, d), pl.BlockSpec((1, d), lambda i: (0, 0))],
        out_specs=tok,
        out_shape=jax.ShapeDtypeStruct(x1.shape, F32),
        scratch_shapes=[pltpu.VMEM((2, TOP_K, tile, d), F32), pltpu.SemaphoreType.DMA((2,))],
        compiler_params=_cparams(("arbitrary",)),
        name="moe_combine",
    )(dest_flat, dest_flat, yr, topg, x1, gate2, gf)


def moe_block(x1, hn2, topi, topg, topr, cnt, gate2, norm_f_g, final_norm, w_gu, b_gu, w_down, b_down, bm, tile):
    n, d = x1.shape
    ne = w_gu.shape[0]
    assert d == SUBLANES * LANES
    counts = cnt[0, :ne].astype(jnp.int32)
    padded = ((counts + bm - 1) // bm) * bm
    pend = jnp.cumsum(padded)
    pstart = pend - padded
    idx = topi[:, :TOP_K].astype(jnp.int32)
    onehot = (idx[..., None] == jnp.arange(ne, dtype=jnp.int32)).astype(jnp.int32)
    dest = jnp.sum(onehot * pstart, axis=-1) + topr[:, :TOP_K].astype(jnp.int32)
    dest_flat = dest.reshape(-1)
    n_rows = -(-(n * TOP_K) // bm) * bm + ne * bm
    n_blocks = n_rows // bm
    starts = jnp.arange(n_blocks, dtype=jnp.int32) * bm
    block_e = jnp.minimum(jnp.sum((starts[:, None] >= pend[None, :]).astype(jnp.int32), axis=1), ne - 1)
    n_used = (pend[-1:] // bm).astype(jnp.int32)
    last_e = block_e[jnp.maximum(n_used[0] - 1, 0)]
    block_e = jnp.where(jnp.arange(n_blocks) < n_used[0], block_e, last_e).astype(jnp.int32)

    xr = dispatch(hn2, dest_flat, n_rows, tile)
    yr = experts(xr, block_e, n_used, w_gu, b_gu, w_down, b_down, bm)
    return combine(yr, dest_flat, topg, x1, gate2, norm_f_g.reshape(1, d), tile, final_norm)


def _group_forward(x, mod, attend, h0_lanes, t_per_seq, lw, mats, norm_f_g, final_norm, rows):
    n, d = x.shape
    rows = min(rows, n)
    (norm1_g, norm2_g, w_in_b, w_out_b, w_glu_b, b_glu, d_vec, wr_hi, wr_lo, br, w_gu, b_gu, w_down, b_down) = lw
    kmat, smat, rmat, decay_re, decay_im = mats
    shift1, scale1, gate1, shift2, scale2, gate2 = [mod[:, j * d:(j + 1) * d] for j in range(6)]
    aw = (w_in_b.shape[1] - w_glu_b.shape[0]) // 3
    q_dtype = BF16 if h0_lanes is None else F32
    q, k, v, kb, vb, u = inproj(x, shift1, scale1, norm1_g, w_in_b, aw, q_dtype, rows)
    o_attn = attend(q, k, v, kb, vb)

    sw = u.shape[1]
    nb = sw // LANES
    u2 = u.reshape(n // SSM_CHUNK, SSM_CHUNK * sw)
    nc = u2.shape[0]
    crow = min(256, nc)
    if h0_lanes is None:
        s = ssm_state(u2, smat, crow)
        hs, hend = ssm_scan(s, decay_re, decay_im, crow, nb)
    else:
        assert t_per_seq == SSM_CHUNK
        hs = h0_lanes
        hend = ssm_state(u2, smat, crow, h0_lanes, decay_re, decay_im)
    y_ssm = ssm_out(u2, hs, kmat, rmat, d_vec, w_glu_b, b_glu, crow).reshape(n, sw)

    x1, hn2, topi, topg, topr, cnt = postmix(x, o_attn, y_ssm, w_out_b, gate1, norm2_g, scale2, shift2,
                                             wr_hi, wr_lo, br, rows)
    y = moe_block(x1, hn2, topi, topg, topr, cnt, gate2, norm_f_g, final_norm, w_gu, b_gu, w_down, b_down,
                  bm=256, tile=256)
    return y, k, v, hend


def kernel(x_prompt, x_sample, c_prompt, c_sample, cache_k, cache_v, state_ssm_re, state_ssm_im, page_table, norm1_g, norm2_g, w_ada, b_ada, w_in, w_out, sb_bias, lam_re, lam_im, log_dt, ssm_b_re, ssm_b_im, ssm_c_re, ssm_c_im, ssm_d, w_glu, b_glu, w_router, b_router, w_gu, b_gu, w_down, b_down, norm_f_g):
    depth = w_in.shape[0]
    bp, tp, d = x_prompt.shape
    bs, ts, _ = x_sample.shape
    assert bp == 1, "the prompt group is handled as one long sequence"
    n_heads = cache_k.shape[3]
    aw = n_heads * HEAD_DIM
    g, p = lam_re.shape[1:]
    nb = g // GROUP_BLOCK
    ne = w_router.shape[-1]
    n_pool, page = cache_k.shape[1:3]
    rows = 512

    xp = x_prompt.reshape(bp * tp, d)
    xs = x_sample.reshape(bs * ts, d)
    n_c = bp + bs
    c_all = jnp.concatenate([c_prompt, c_sample, jnp.zeros((-n_c % SUBLANES, d), F32)], axis=0)

    outs = {name: [] for name in ("kp", "vp", "rp", "ip", "ks", "vs", "rs", "is")}
    for l in range(depth):
        mod = ada_mod(c_all, w_ada[l], b_ada[l])
        mod_p = mod[:bp]
        mod_s = jnp.repeat(mod[bp:n_c], ts, axis=0)
        mats = _ssm_mats(lam_re[l], lam_im[l], log_dt[l], ssm_b_re[l], ssm_b_im[l], ssm_c_re[l], ssm_c_im[l])
        wr = jnp.pad(w_router[l], ((0, 0), (0, ROUTER_PAD - ne)))
        wr_hi = wr.astype(BF16)
        wr_lo = (wr - wr_hi.astype(F32)).astype(BF16)
        br = jnp.concatenate([b_router[l], jnp.full((ROUTER_PAD - ne,), NEG_BIG, F32)]).reshape(1, ROUTER_PAD)
        lw = (norm1_g[l].reshape(1, d), norm2_g[l].reshape(1, d), w_in[l].astype(BF16), w_out[l].astype(BF16),
              w_glu[l].astype(BF16), b_glu[l].reshape(1, -1), ssm_d[l].reshape(1, -1), wr_hi, wr_lo, br,
              w_gu[l], b_gu[l], w_down[l], b_down[l])
        bias = sb_bias[l]

        def attend_p(q, k, v, kb, vb):
            return attn_prompt(q, kb, vb, bias, nsb=min(8, tp // KEY_BLOCK))

        ck = cache_k[l].transpose(0, 2, 3, 1).reshape(n_pool, aw, page)
        cv = cache_v[l].transpose(0, 2, 3, 1).reshape(n_pool, aw, page)

        def attend_s(q, k, v, kb, vb):
            return attn_sample(q, k, v, ck, cv, page_table, bias, ts, pages_per_step=math.gcd(8, page_table.shape[1]))

        h0 = _state_to_lanes(state_ssm_re[l], state_ssm_im[l], nb)
        xp, kp, vp, hp = _group_forward(xp, mod_p, attend_p, None, tp, lw, mats, norm_f_g, l == depth - 1, rows)
        xs, ks, vs, hs = _group_forward(xs, mod_s, attend_s, h0, ts, lw, mats, norm_f_g, l == depth - 1, rows)
        rp, ip = _lanes_to_state(hp, g, p)
        rs, is_ = _lanes_to_state(hs, g, p)
        outs["kp"].append(kp.reshape(bp, tp, n_heads, HEAD_DIM))
        outs["vp"].append(vp.reshape(bp, tp, n_heads, HEAD_DIM))
        outs["rp"].append(rp)
        outs["ip"].append(ip)
        outs["ks"].append(ks.reshape(bs, ts, n_heads, HEAD_DIM))
        outs["vs"].append(vs.reshape(bs, ts, n_heads, HEAD_DIM))
        outs["rs"].append(rs)
        outs["is"].append(is_)
    st = lambda name: jnp.stack(outs[name])
    return (xp.reshape(bp, tp, d), xs.reshape(bs, ts, d), st("kp"), st("vp"), st("rp"), st("ip"),
            st("ks"), st("vs"), st("rs"), st("is"))
```

```python
import functools
import math

import jax
import jax.numpy as jnp
from jax import lax
from jax.experimental import pallas as pl
from jax.experimental.pallas import tpu as pltpu

F32 = jnp.float32
BF16 = jnp.bfloat16

HEAD_DIM = 64
SSM_GROUP = 16
TOP_K = 4
SWIGLU_LIMIT = 7.0
SWIGLU_ALPHA = 1.702
RMS_EPS = 1e-6

LANES = 128
SUBLANES = 8
VMEM_LIMIT_BYTES = 56 * 1024 * 1024

KEY_BLOCK = LANES
SSM_CHUNK = 8
GROUP_BLOCK = LANES // SSM_GROUP
ROUTER_PAD = LANES
NEG_BIG = -1e30
EXP_CAP = 1e30
LOG2E = math.log2(math.e)
PAGE_SLOTS = 3

def _cparams(semantics):
    return pltpu.CompilerParams(dimension_semantics=semantics, vmem_limit_bytes=VMEM_LIMIT_BYTES)


def _dot(a, b):
    return jnp.dot(a, b, preferred_element_type=F32)


def _dot_nt(a, b):
    return lax.dot_general(a, b, (((1,), (1,)), ((), ())), preferred_element_type=F32)


def _split_bf16(x):
    hi = x.astype(BF16)
    lo = (x - hi.astype(F32)).astype(BF16)
    return hi, lo


def _ada_kernel(c_ref, w_ref, b_ref, o_ref):
    o_ref[...] = _dot(c_ref[...].astype(BF16), w_ref[...].astype(BF16)) + b_ref[...]


def ada_mod(c, w_ada, b_ada):
    n, d = c.shape
    n_out = w_ada.shape[1]
    return pl.pallas_call(
        _ada_kernel,
        grid=(n_out // d,),
        in_specs=[pl.BlockSpec((n, d), lambda j: (0, 0)),
                  pl.BlockSpec((d, d), lambda j: (0, j)),
                  pl.BlockSpec((1, d), lambda j: (0, j))],
        out_specs=pl.BlockSpec((n, d), lambda j: (0, j)),
        out_shape=jax.ShapeDtypeStruct((n, n_out), F32),
        compiler_params=_cparams(("arbitrary",)),
        name="ada_mod",
    )(c, w_ada, b_ada.reshape(1, n_out))


def _inproj_kernel(x_ref, shift_ref, scale_ref, g_ref, w_ref, q_ref, k_ref, v_ref, kb_ref, vb_ref, u_ref, *, aw):
    x = x_ref[...]
    ms = jnp.mean(x * x, axis=-1, keepdims=True)
    hn = x * lax.rsqrt(ms + RMS_EPS) * g_ref[...]
    hn = hn * (1.0 + scale_ref[...]) + shift_ref[...]
    proj = _dot(hn.astype(BF16), w_ref[...])
    q_ref[...] = (proj[:, :aw] * (HEAD_DIM ** -0.5 * LOG2E)).astype(q_ref.dtype)
    k = proj[:, aw:2 * aw]
    v = proj[:, 2 * aw:3 * aw]
    k_ref[...] = k
    v_ref[...] = v
    kb_ref[...] = k.astype(BF16)
    vb_ref[...] = v.astype(BF16)
    u_ref[...] = proj[:, 3 * aw:]


def _row_spec(n_mod_rows, rows, d):
    if n_mod_rows == 1:
        return pl.BlockSpec((1, d), lambda i: (0, 0))
    return pl.BlockSpec((rows, d), lambda i: (i, 0))


def inproj(x, shift, scale, g, w_in_b, aw, q_dtype, rows):
    n, d = x.shape
    pw = w_in_b.shape[1]
    sw = pw - 3 * aw
    tok = lambda width: pl.BlockSpec((rows, width), lambda i: (i, 0))
    return pl.pallas_call(
        functools.partial(_inproj_kernel, aw=aw),
        grid=(n // rows,),
        in_specs=[tok(d), _row_spec(shift.shape[0], rows, d), _row_spec(scale.shape[0], rows, d),
                  pl.BlockSpec((1, d), lambda i: (0, 0)),
                  pl.BlockSpec((d, pw), lambda i: (0, 0))],
        out_specs=[tok(aw), tok(aw), tok(aw), tok(aw), tok(aw), tok(sw)],
        out_shape=[jax.ShapeDtypeStruct((n, aw), q_dtype),
                   jax.ShapeDtypeStruct((n, aw), F32), jax.ShapeDtypeStruct((n, aw), F32),
                   jax.ShapeDtypeStruct((n, aw), BF16), jax.ShapeDtypeStruct((n, aw), BF16),
                   jax.ShapeDtypeStruct((n, sw), F32)],
        compiler_params=_cparams(("arbitrary",)),
        name="inproj",
    )(x, shift, scale, g, w_in_b)


def _tri_ones():
    j = lax.broadcasted_iota(jnp.int32, (KEY_BLOCK, 2 * KEY_BLOCK), 0)
    s = lax.broadcasted_iota(jnp.int32, (KEY_BLOCK, 2 * KEY_BLOCK), 1)
    return jnp.where((j > s) | (s >= KEY_BLOCK), 1.0, 0.0).astype(BF16)


def _sb_logits(z, mask):
    drop = jnp.maximum(jnp.log(1.0 + jnp.minimum(jnp.exp2(z), EXP_CAP)) * LOG2E, z)
    if mask is not None:
        drop = jnp.where(mask, drop, 0.0)
    return z - drop, drop.astype(BF16)


def _weights_from_log2(x):
    return jnp.exp2(x).astype(BF16)


def _sb_finish(log_beta, drop_b, carry, tri, mask):
    groups = drop_b.shape[1] // KEY_BLOCK
    later, total = [], []
    for g in range(groups):
        cs = _dot(drop_b[:, g * KEY_BLOCK:(g + 1) * KEY_BLOCK], tri)
        later.append(cs[:, :KEY_BLOCK])
        total.append(cs[:, KEY_BLOCK:])
    later = jnp.concatenate(later, axis=1) if groups > 1 else later[0]
    total = jnp.concatenate(total, axis=1) if groups > 1 else total[0]
    a = _weights_from_log2(log_beta - (later + carry))
    if mask is not None:
        a = jnp.where(mask, a, jnp.zeros_like(a))
    return a, total


def _sb_weights(z, carry, tri, mask):
    if mask is not None:
        mask = jnp.concatenate([mask] * (z.shape[1] // KEY_BLOCK), axis=1)
    log_beta, drop_b = _sb_logits(z, mask)
    return _sb_finish(log_beta, drop_b, carry, tri, mask)


def _attn_prompt_kernel(kbias_ref, q_ref, k_ref, v_ref, tri_ref, o_ref, acc_ref, carry_ref, lb_scr, drop_scr,
                        qext_scr, *, nsb):
    it = pl.program_id(1)
    tq = nsb * KEY_BLOCK
    tri = tri_ref[...]
    ones_lanes = lax.broadcasted_iota(jnp.int32, (tq, LANES), 1) < 2
    qext_scr[:, :LANES] = q_ref[...]
    qext_scr[:, LANES:] = jnp.where(ones_lanes, 1.0, 0.0).astype(BF16)
    kbias = kbias_ref[0]
    acc_ref[...] = jnp.zeros_like(acc_ref)
    carry_ref[...] = jnp.zeros_like(carry_ref)
    lane = lax.broadcasted_iota(jnp.int32, (KEY_BLOCK, LANES), 1)
    first_head = lane < HEAD_DIM

    def stacked(ref, block_index):
        st = pl.multiple_of(block_index * KEY_BLOCK, KEY_BLOCK)
        blk = ref[pl.ds(st, KEY_BLOCK), :]
        zero = jnp.zeros_like(blk)
        return jnp.concatenate([jnp.where(first_head, blk, zero), jnp.where(first_head, zero, blk)], axis=0)

    for c in reversed(range(nsb)):
        r0 = c * KEY_BLOCK
        rr = lax.broadcasted_iota(jnp.int32, (tq - r0, LANES), 0)
        ll = lax.broadcasted_iota(jnp.int32, (tq - r0, LANES), 1)
        mask = (rr >= KEY_BLOCK) | (ll < rr)
        z = _dot_nt(qext_scr[r0:, :], jnp.concatenate([stacked(k_ref, it * nsb + c), kbias], axis=1))
        a, tot = _sb_weights(z, carry_ref[r0:, :], tri, mask)
        acc_ref[r0:, :] += _dot(a, stacked(v_ref, it * nsb + c))
        carry_ref[r0:, :] += tot

    n_full = it * nsb

    def first_half(block_index, slot):
        z = _dot_nt(qext_scr[...], jnp.concatenate([stacked(k_ref, block_index), kbias], axis=1))
        log_beta, drop_b = _sb_logits(z, None)
        lb_scr[slot] = log_beta
        drop_scr[slot] = drop_b

    def second_half(block_index, slot):
        a, tot = _sb_finish(lb_scr[slot], drop_scr[slot], carry_ref[...], tri, None)
        acc_ref[...] += _dot(a, stacked(v_ref, block_index))
        carry_ref[...] += tot

    @pl.when(n_full > 0)
    def _():
        per_trip = math.gcd(nsb, 4)
        first_half(n_full - 1, 0)

        def body(jj, c):
            blk = n_full - 1 - per_trip * jj
            for u in range(per_trip):
                first_half(blk - u - 1, (u + 1) % 2)
                second_half(blk - u, u % 2)
            return c

        lax.fori_loop(0, n_full // per_trip - 1, body, 0)
        for u in range(per_trip - 1):
            first_half(per_trip - 2 - u, (u + 1) % 2)
            second_half(per_trip - 1 - u, u % 2)
        second_half(0, (per_trip - 1) % 2)

    o_ref[...] = acc_ref[...].astype(o_ref.dtype)


def attn_prompt(q, kb, vb, bias, nsb):
    n, aw = q.shape
    tq = nsb * KEY_BLOCK
    n_pairs = aw // LANES
    b_hi, b_lo = _split_bf16(bias.astype(F32) * LOG2E)
    lane = jnp.arange(LANES)
    kbias = jnp.where(lane == 0, b_hi[:, None, None], jnp.where(lane == 1, b_lo[:, None, None], 0)).astype(BF16)
    kbias = jnp.broadcast_to(kbias, (bias.shape[0], KEY_BLOCK, LANES)).reshape(n_pairs, 2 * KEY_BLOCK, LANES)
    tri = _tri_ones()
    return pl.pallas_call(
        functools.partial(_attn_prompt_kernel, nsb=nsb),
        grid=(n_pairs, n // tq),
        in_specs=[pl.BlockSpec((1, 2 * KEY_BLOCK, LANES), lambda hp, i: (hp, 0, 0)),
                  pl.BlockSpec((tq, LANES), lambda hp, i: (i, hp)),
                  pl.BlockSpec((n, LANES), lambda hp, i: (0, hp)),
                  pl.BlockSpec((n, LANES), lambda hp, i: (0, hp)),
                  pl.BlockSpec(tri.shape, lambda hp, i: (0, 0))],
        out_specs=pl.BlockSpec((tq, LANES), lambda hp, i: (i, hp)),
        out_shape=jax.ShapeDtypeStruct((n, aw), BF16),
        scratch_shapes=[pltpu.VMEM((tq, LANES), F32), pltpu.VMEM((tq, 2 * KEY_BLOCK), F32),
                        pltpu.VMEM((2, tq, 2 * KEY_BLOCK), F32), pltpu.VMEM((2, tq, 2 * KEY_BLOCK), BF16),
                        pltpu.VMEM((tq, 2 * LANES), BF16)],
        compiler_params=_cparams(("arbitrary", "arbitrary")),
        name="attn_prompt",
    )(kbias, q, kb, vb, tri)


def _attn_sample_kernel(pt_ref, q_ref, kn_ref, vn_ref, ck_hbm, cv_hbm, tri_ref, bias_ref, o_ref,
                        kbuf, vbuf, sems, qbd_ref, acc_ref, carry_ref, *, group, n_groups, n_heads, t_new):
    b = pl.program_id(0)
    total = pl.num_programs(0) * n_groups
    m = n_heads * t_new
    aw = n_heads * HEAD_DIM
    tri = tri_ref[...]
    bias = bias_ref[...]

    def request(s):
        slot = s % PAGE_SLOTS
        for i in range(group):
            page = pt_ref[s * group + i]
            pltpu.make_async_copy(ck_hbm.at[page], kbuf.at[slot, i], sems.at[slot, 0]).start()
            pltpu.make_async_copy(cv_hbm.at[page], vbuf.at[slot, i], sems.at[slot, 1]).start()

    def wait_group(slot):
        pltpu.make_async_copy(kbuf.at[slot], kbuf.at[slot], sems.at[slot, 0]).wait()
        pltpu.make_async_copy(vbuf.at[slot], vbuf.at[slot], sems.at[slot, 1]).wait()

    @pl.when(b == 0)
    def _():
        for s in range(PAGE_SLOTS - 1):
            request(s)

    q = q_ref[...].astype(BF16)
    qt = jnp.concatenate([q] * n_heads, axis=0)
    rw = lax.broadcasted_iota(jnp.int32, (m, aw), 0)
    ln = lax.broadcasted_iota(jnp.int32, (m, aw), 1)
    qbd_ref[...] = jnp.where(ln // HEAD_DIM == rw // t_new, qt, jnp.zeros_like(qt))
    pad = jnp.zeros((KEY_BLOCK - t_new, aw), F32)
    kn = jnp.concatenate([kn_ref[...], pad], axis=0).astype(BF16)
    vn = jnp.concatenate([vn_ref[...], pad], axis=0).astype(BF16)
    r2 = lax.broadcasted_iota(jnp.int32, (m, KEY_BLOCK), 0)
    l2 = lax.broadcasted_iota(jnp.int32, (m, KEY_BLOCK), 1)
    z = _dot_nt(qbd_ref[...], kn) + bias
    a, tot = _sb_weights(z, 0.0, tri, l2 < (r2 % t_new))
    acc_ref[...] = _dot(a, vn)
    carry_ref[...] = tot

    def body(g, c):
        s = b * n_groups + g

        @pl.when(s + PAGE_SLOTS - 1 < total)
        def _():
            request(s + PAGE_SLOTS - 1)

        slot = s % PAGE_SLOTS
        wait_group(slot)
        kt = jnp.concatenate([kbuf[slot, i].astype(BF16) for i in range(group)], axis=1)
        vt = jnp.concatenate([vbuf[slot, i].astype(BF16) for i in range(group)], axis=1)
        z = _dot(qbd_ref[...], kt) + jnp.concatenate([bias] * group, axis=1)
        log_beta, drop_b = _sb_logits(z, None)
        carry = carry_ref[...]
        shift = []
        for i in range(group):
            cs = _dot(drop_b[:, i * KEY_BLOCK:(i + 1) * KEY_BLOCK], tri)
            shift.append(cs[:, :KEY_BLOCK] + carry)
            carry = carry + cs[:, KEY_BLOCK:]
        a = _weights_from_log2(log_beta - jnp.concatenate(shift, axis=1))
        acc_ref[...] += _dot_nt(a, vt)
        carry_ref[...] = carry
        return c

    lax.fori_loop(0, n_groups, body, 0)

    acc = acc_ref[...]
    lo = lax.broadcasted_iota(jnp.int32, (t_new, aw), 1)
    o = jnp.zeros((t_new, aw), F32)
    for h in range(n_heads):
        o = o + jnp.where(lo // HEAD_DIM == h, acc[h * t_new:(h + 1) * t_new, :], 0.0)
    o_ref[...] = o.astype(o_ref.dtype)


def attn_sample(q, k_new, v_new, cache_k, cache_v, page_table, bias, t_new, group):
    n, aw = q.shape
    bsz, n_pages = page_table.shape
    n_heads = aw // HEAD_DIM
    page = cache_k.shape[2]
    n_groups = n_pages // group
    assert page == KEY_BLOCK and t_new == SUBLANES and n_pages % group == 0 and bsz * n_groups >= PAGE_SLOTS
    m = n_heads * t_new
    bias_rows = jnp.broadcast_to(jnp.repeat(bias * LOG2E, t_new)[:, None], (m, KEY_BLOCK)).astype(F32)
    pages_recent_first = page_table[:, ::-1].reshape(-1)
    tok = pl.BlockSpec((t_new, aw), lambda b, pt: (b, 0))
    ring = pltpu.VMEM((PAGE_SLOTS, group, aw, page), F32)
    grid_spec = pltpu.PrefetchScalarGridSpec(
        num_scalar_prefetch=1,
        grid=(bsz,),
        in_specs=[tok, tok, tok, pl.BlockSpec(memory_space=pl.ANY), pl.BlockSpec(memory_space=pl.ANY),
                  pl.BlockSpec((KEY_BLOCK, 2 * KEY_BLOCK), lambda b, pt: (0, 0)),
                  pl.BlockSpec((m, KEY_BLOCK), lambda b, pt: (0, 0))],
        out_specs=tok,
        scratch_shapes=[ring, ring, pltpu.SemaphoreType.DMA((PAGE_SLOTS, 2)),
                        pltpu.VMEM((m, aw), BF16), pltpu.VMEM((m, aw), F32), pltpu.VMEM((m, KEY_BLOCK), F32)],
    )
    return pl.pallas_call(
        functools.partial(_attn_sample_kernel, group=group, n_groups=n_groups, n_heads=n_heads, t_new=t_new),
        grid_spec=grid_spec,
        out_shape=jax.ShapeDtypeStruct((n, aw), BF16),
        compiler_params=_cparams(("arbitrary",)),
        name="attn_sample",
    )(pages_recent_first, q, k_new, v_new, cache_k, cache_v, _tri_ones(), bias_rows)


def _ssm_mats(lam_re, lam_im, log_dt, b_re, b_im, c_re, c_im):
    g, p = lam_re.shape
    hh = b_re.shape[-1]
    el = SSM_CHUNK
    nb = g // GROUP_BLOCK
    dt = jnp.exp(log_dt)[:, None]
    ar, ai = lam_re * dt, lam_im * dt
    lbr, lbi = jnp.exp(ar) * jnp.cos(ai), jnp.exp(ar) * jnp.sin(ai)
    den = lam_re * lam_re + lam_im * lam_im
    fr = ((lbr - 1.0) * lam_re + lbi * lam_im) / den
    fi = (lbi * lam_re - (lbr - 1.0) * lam_im) / den
    bbr = fr[..., None] * b_re - fi[..., None] * b_im
    bbi = fr[..., None] * b_im + fi[..., None] * b_re
    n = jnp.arange(el + 1, dtype=F32)[:, None, None]
    pr = jnp.exp(ar[None] * n) * jnp.cos(ai[None] * n)
    pi = jnp.exp(ar[None] * n) * jnp.sin(ai[None] * n)
    eye = jnp.eye(GROUP_BLOCK, dtype=F32)

    cpr = c_re[None] * pr[:el, :, None, :] - c_im[None] * pi[:el, :, None, :]
    cpi = c_re[None] * pi[:el, :, None, :] + c_im[None] * pr[:el, :, None, :]
    kd = jnp.einsum('dgip,gpj->dgij', cpr, bbr) - jnp.einsum('dgip,gpj->dgij', cpi, bbi)
    kmat = jnp.einsum('dkgij,gh->dkgjhi', kd.reshape(el, nb, GROUP_BLOCK, hh, hh), eye)
    kmat = kmat.reshape(el, nb, LANES, LANES)

    rev = pr[el - 1 - jnp.arange(el)], pi[el - 1 - jnp.arange(el)]
    scr = rev[0][..., None] * bbr[None] - rev[1][..., None] * bbi[None]
    sci = rev[0][..., None] * bbi[None] + rev[1][..., None] * bbr[None]

    def blk_s(a):
        a = jnp.einsum('skgpj,gh->skgjhp', a.reshape(el, nb, GROUP_BLOCK, p, hh), eye)
        return a.reshape(el, nb, LANES, GROUP_BLOCK * p)

    smat = jnp.concatenate([blk_s(scr), blk_s(sci)], axis=-1)

    c1r = c_re[None] * pr[1:, :, None, :] - c_im[None] * pi[1:, :, None, :]
    c1i = c_re[None] * pi[1:, :, None, :] + c_im[None] * pr[1:, :, None, :]

    def blk_r(a):
        a = jnp.einsum('tkgip,gh->tkgphi', a.reshape(el, nb, GROUP_BLOCK, hh, p), eye)
        return a.reshape(el, nb, GROUP_BLOCK * p, LANES)

    rmat = jnp.concatenate([blk_r(c1r), -blk_r(c1i)], axis=-2)

    decay_re = pr[el].reshape(1, g * p)
    decay_im = pi[el].reshape(1, g * p)
    return kmat.astype(BF16), smat.astype(BF16), rmat.astype(BF16), decay_re, decay_im


def _state_to_lanes(h_re, h_im, nb):
    b = h_re.shape[0]
    return jnp.stack([h_re.reshape(b, nb, -1), h_im.reshape(b, nb, -1)], axis=2).reshape(b, -1)


def _lanes_to_state(h, g, p):
    b = h.shape[0]
    nb = g // GROUP_BLOCK
    h = h.reshape(b, nb, 2, GROUP_BLOCK, p)
    return h[:, :, 0].reshape(b, g, p), h[:, :, 1].reshape(b, g, p)


def _ssm_state_kernel(*refs, has_h0, nb, sw, half):
    if has_h0:
        u_ref, smat_ref, h0_ref, dre_ref, dim_ref, s_ref = refs
    else:
        u_ref, smat_ref, s_ref = refs
    for k in range(nb):
        acc = None
        for s in range(SSM_CHUNK):
            ub = u_ref[:, s * sw + k * LANES: s * sw + (k + 1) * LANES].astype(BF16)
            d = _dot(ub, smat_ref[s, k])
            acc = d if acc is None else acc + d
        base = 2 * half * k
        if has_h0:
            hr = h0_ref[:, base:base + half]
            hi = h0_ref[:, base + half:base + 2 * half]
            dr = dre_ref[:, k * half:(k + 1) * half]
            di = dim_ref[:, k * half:(k + 1) * half]
            s_ref[:, base:base + half] = acc[:, :half] + dr * hr - di * hi
            s_ref[:, base + half:base + 2 * half] = acc[:, half:] + dr * hi + di * hr
        else:
            s_ref[:, base:base + 2 * half] = acc


def ssm_state(u2, smat, rows, h0=None, decay_re=None, decay_im=None):
    nc, width = u2.shape
    sw = width // SSM_CHUNK
    nb = sw // LANES
    half = smat.shape[-1] // 2
    sl = nb * 2 * half
    has_h0 = h0 is not None
    row = lambda w: pl.BlockSpec((rows, w), lambda i: (i, 0))
    in_specs = [row(width), pl.BlockSpec(smat.shape, lambda i: (0, 0, 0, 0))]
    args = [u2, smat]
    if has_h0:
        in_specs += [row(sl), pl.BlockSpec((1, nb * half), lambda i: (0, 0)),
                     pl.BlockSpec((1, nb * half), lambda i: (0, 0))]
        args += [h0, decay_re, decay_im]
    return pl.pallas_call(
        functools.partial(_ssm_state_kernel, has_h0=has_h0, nb=nb, sw=sw, half=half),
        grid=(nc // rows,),
        in_specs=in_specs,
        out_specs=row(sl),
        out_shape=jax.ShapeDtypeStruct((nc, sl), F32),
        compiler_params=_cparams(("arbitrary",)),
        name="ssm_state",
    )(*args)


def _ssm_scan_kernel(s_ref, dre_ref, dim_ref, hs_ref, hend_ref, h_scr, *, nb, half):
    @pl.when(pl.program_id(0) == 0)
    def _():
        h_scr[...] = jnp.zeros_like(h_scr)

    dr = dre_ref[...]
    di = dim_ref[...]

    def body(r, h):
        hs_ref[pl.ds(r, 1), :] = h
        s = s_ref[pl.ds(r, 1), :]
        parts = []
        for k in range(nb):
            base = 2 * half * k
            hr, hi = h[:, base:base + half], h[:, base + half:base + 2 * half]
            ar, ai = dr[:, k * half:(k + 1) * half], di[:, k * half:(k + 1) * half]
            parts.append(ar * hr - ai * hi + s[:, base:base + half])
            parts.append(ar * hi + ai * hr + s[:, base + half:base + 2 * half])
        return jnp.concatenate(parts, axis=1)

    h = lax.fori_loop(0, s_ref.shape[0], body, h_scr[...])
    h_scr[...] = h
    hend_ref[...] = h


def ssm_scan(s, decay_re, decay_im, rows, nb):
    nc, sl = s.shape
    n_half_total = decay_re.shape[1]
    half = n_half_total // nb
    row = pl.BlockSpec((rows, sl), lambda i: (i, 0))
    vec = pl.BlockSpec((1, n_half_total), lambda i: (0, 0))
    return pl.pallas_call(
        functools.partial(_ssm_scan_kernel, nb=nb, half=half),
        grid=(nc // rows,),
        in_specs=[row, vec, vec],
        out_specs=[row, pl.BlockSpec((1, sl), lambda i: (0, 0))],
        out_shape=[jax.ShapeDtypeStruct((nc, sl), F32), jax.ShapeDtypeStruct((1, sl), F32)],
        scratch_shapes=[pltpu.VMEM((1, sl), F32)],
        compiler_params=_cparams(("arbitrary",)),
        name="ssm_scan",
    )(s, decay_re, decay_im)


def _gelu_exact(y):
    return 0.5 * y * (1.0 + lax.erf(y * (0.5 ** 0.5)))


def _ssm_out_kernel(u_ref, hs_ref, kmat_ref, rmat_ref, d_ref, wglu_ref, bglu_ref, y_ref, *, nb, sw, half):
    hb = [hs_ref[:, 2 * half * k:2 * half * (k + 1)].astype(BF16) for k in range(nb)]
    ub = [[u_ref[:, s * sw + k * LANES: s * sw + (k + 1) * LANES].astype(BF16) for k in range(nb)]
          for s in range(SSM_CHUNK)]
    wglu = wglu_ref[...]
    for t in range(SSM_CHUNK):
        cols = []
        for k in range(nb):
            acc = _dot(hb[k], rmat_ref[t, k])
            for s in range(t + 1):
                acc = acc + _dot(ub[s][k], kmat_ref[t - s, k])
            cols.append(acc)
        y = jnp.concatenate(cols, axis=1) + d_ref[...] * u_ref[:, t * sw:(t + 1) * sw]
        g = _gelu_exact(y)
        gate = jax.nn.sigmoid(_dot(g.astype(BF16), wglu) + bglu_ref[...])
        y_ref[:, t * sw:(t + 1) * sw] = (g * gate).astype(y_ref.dtype)


def ssm_out(u2, hs, kmat, rmat, d_vec, w_glu_b, b_glu, rows):
    nc, width = u2.shape
    sw = width // SSM_CHUNK
    nb = sw // LANES
    sl = hs.shape[1]
    half = sl // (2 * nb)
    row = lambda w: pl.BlockSpec((rows, w), lambda i: (i, 0))
    const = lambda a: pl.BlockSpec(a.shape, lambda i: (0,) * a.ndim)
    return pl.pallas_call(
        functools.partial(_ssm_out_kernel, nb=nb, sw=sw, half=half),
        grid=(nc // rows,),
        in_specs=[row(width), row(sl), const(kmat), const(rmat), const(d_vec), const(w_glu_b), const(b_glu)],
        out_specs=row(width),
        out_shape=jax.ShapeDtypeStruct((nc, width), BF16),
        compiler_params=_cparams(("arbitrary",)),
        name="ssm_out",
    )(u2, hs, kmat, rmat, d_vec, w_glu_b, b_glu)


def _postmix_kernel(x_ref, o_ref, y_ref, wo_ref, gate1_ref, g2_ref, scale2_ref, shift2_ref, wrh_ref, wrl_ref,
                    br_ref, tril_ref, x1_ref, hn2_ref, topi_ref, topg_ref, topr_ref, cnt_ref, carry_ref, *, aw):
    @pl.when(pl.program_id(0) == 0)
    def _():
        carry_ref[...] = jnp.zeros_like(carry_ref)

    mix = _dot(o_ref[...], wo_ref[:aw, :]) + _dot(y_ref[...], wo_ref[aw:, :])
    x1 = x_ref[...] + gate1_ref[...] * mix
    x1_ref[...] = x1
    ms = jnp.mean(x1 * x1, axis=-1, keepdims=True)
    hn2 = x1 * lax.rsqrt(ms + RMS_EPS) * g2_ref[...]
    hn2 = hn2 * (1.0 + scale2_ref[...]) + shift2_ref[...]
    hn2_ref[...] = hn2

    hh, hl = _split_bf16(hn2)
    wrh = wrh_ref[...]
    logits = _dot(hh, wrh) + _dot(hl, wrh) + _dot(hh, wrl_ref[...]) + br_ref[...]

    rows = logits.shape[0]
    lane = lax.broadcasted_iota(jnp.int32, (rows, ROUTER_PAD), 1)
    work = logits
    sel = jnp.zeros((rows, ROUTER_PAD), F32)
    picks, vals, idxs = [], [], []
    for _ in range(TOP_K):
        m = jnp.max(work, axis=-1, keepdims=True)
        idx = jnp.min(jnp.where(work == m, lane, ROUTER_PAD), axis=-1, keepdims=True)
        pick = lane == idx
        picks.append(pick)
        vals.append(m)
        idxs.append(idx)
        sel = jnp.where(pick, 1.0, sel)
        work = jnp.where(pick, -jnp.inf, work)

    rank = _dot(tril_ref[...], sel.astype(BF16)) + carry_ref[...]
    carry_ref[...] = rank[rows - 1:rows, :] + sel[rows - 1:rows, :]
    cnt_ref[...] = carry_ref[...]

    es = [jnp.exp(v - vals[0]) for v in vals]
    den = es[0]
    for e in es[1:]:
        den = den + e
    topi = jnp.zeros((rows, ROUTER_PAD), F32)
    topg = jnp.zeros((rows, ROUTER_PAD), F32)
    topr = jnp.zeros((rows, ROUTER_PAD), F32)
    for r in range(TOP_K):
        rk = jnp.sum(jnp.where(picks[r], rank, 0.0), axis=-1, keepdims=True)
        topi = jnp.where(lane == r, idxs[r].astype(F32), topi)
        topg = jnp.where(lane == r, es[r] / den, topg)
        topr = jnp.where(lane == r, rk, topr)
    topi_ref[...] = topi
    topg_ref[...] = topg
    topr_ref[...] = topr


def postmix(x, o_attn, y_ssm, w_out_b, gate1, g2, scale2, shift2, wr_hi, wr_lo, br, rows):
    n, d = x.shape
    aw = o_attn.shape[1]
    tok = lambda width: pl.BlockSpec((rows, width), lambda i: (i, 0))
    const = lambda a: pl.BlockSpec(a.shape, lambda i: (0,) * a.ndim)
    r = lax.broadcasted_iota(jnp.int32, (rows, rows), 0)
    c = lax.broadcasted_iota(jnp.int32, (rows, rows), 1)
    tril = jnp.where(c < r, 1.0, 0.0).astype(BF16)
    lanes_out = jax.ShapeDtypeStruct((n, ROUTER_PAD), F32)
    return pl.pallas_call(
        functools.partial(_postmix_kernel, aw=aw),
        grid=(n // rows,),
        in_specs=[tok(d), tok(aw), tok(y_ssm.shape[1]), const(w_out_b),
                  _row_spec(gate1.shape[0], rows, d), const(g2),
                  _row_spec(scale2.shape[0], rows, d), _row_spec(shift2.shape[0], rows, d),
                  const(wr_hi), const(wr_lo), const(br), const(tril)],
        out_specs=[tok(d), tok(d), tok(ROUTER_PAD), tok(ROUTER_PAD), tok(ROUTER_PAD),
                   pl.BlockSpec((1, ROUTER_PAD), lambda i: (0, 0))],
        out_shape=[jax.ShapeDtypeStruct((n, d), F32), jax.ShapeDtypeStruct((n, d), F32),
                   lanes_out, lanes_out, lanes_out, jax.ShapeDtypeStruct((1, ROUTER_PAD), F32)],
        scratch_shapes=[pltpu.VMEM((1, ROUTER_PAD), F32)],
        compiler_params=_cparams(("arbitrary",)),
        name="postmix",
    )(x, o_attn, y_ssm, w_out_b, gate1, g2, scale2, shift2, wr_hi, wr_lo, br, tril)


def _wait_rows(ref, n_copy, sem):
    span = ref.at[pl.ds(0, n_copy)]
    pltpu.make_async_copy(span, span, sem).wait()


def _dispatch_kernel(dest_ref, hn_ref, xr_in, xr_hbm, stage, sems, *, tile):
    del xr_in
    i = pl.program_id(0)
    slot = i % 2
    n_copy = tile * TOP_K
    stage[slot] = hn_ref[...]

    def start(t, c):
        for k in range(TOP_K):
            pltpu.make_async_copy(stage.at[slot, pl.ds(t, 1)], xr_hbm.at[pl.ds(dest_ref[t * TOP_K + k], 1)],
                                  sems.at[slot]).start()
        return c

    lax.fori_loop(0, tile, start, 0)

    @pl.when(i > 0)
    def _():
        _wait_rows(xr_hbm, n_copy, sems.at[1 - slot])

    @pl.when(i == pl.num_programs(0) - 1)
    def _():
        _wait_rows(xr_hbm, n_copy, sems.at[slot])


def dispatch(hn, dest_flat, n_rows, tile):
    n, d = hn.shape
    xr0 = jnp.zeros((n_rows, d), hn.dtype)
    return pl.pallas_call(
        functools.partial(_dispatch_kernel, tile=tile),
        grid=(n // tile,),
        in_specs=[pl.BlockSpec((tile * TOP_K,), lambda i: (i,), memory_space=pltpu.SMEM),
                  pl.BlockSpec((tile, d), lambda i: (i, 0)),
                  pl.BlockSpec(memory_space=pl.ANY)],
        out_specs=pl.BlockSpec(memory_space=pl.ANY),
        out_shape=jax.ShapeDtypeStruct(xr0.shape, xr0.dtype),
        scratch_shapes=[pltpu.VMEM((2, tile, d), hn.dtype), pltpu.SemaphoreType.DMA((2,))],
        input_output_aliases={2: 0},
        compiler_params=_cparams(("arbitrary",)),
        name="moe_dispatch",
    )(dest_flat, hn, xr0)


def _expert_kernel(be_ref, nb_ref, x_ref, wgu_ref, bgu_ref, wd_ref, bd_ref, y_ref, wgu_b, wd_b, *, ff):
    i = pl.program_id(0)

    @pl.when(i < nb_ref[0])
    def _():
        prev = be_ref[jnp.maximum(i - 1, 0)]

        @pl.when((i == 0) | (be_ref[i] != prev))
        def _():
            wgu_b[...] = wgu_ref[0].astype(BF16)
            wd_b[...] = wd_ref[0].astype(BF16)

        gu = _dot(x_ref[...].astype(BF16), wgu_b[...]) + bgu_ref[0]
        gate = jnp.minimum(gu[:, :ff], SWIGLU_LIMIT)
        up = jnp.clip(gu[:, ff:], -SWIGLU_LIMIT, SWIGLU_LIMIT)
        act = (up + 1.0) * (gate * jax.nn.sigmoid(SWIGLU_ALPHA * gate))
        y_ref[...] = _dot(act.astype(BF16), wd_b[...]) + bd_ref[0]

    @pl.when(i >= nb_ref[0])
    def _():
        y_ref[...] = jnp.zeros_like(y_ref)


def experts(xr, block_e, n_used, w_gu, b_gu, w_down, b_down, bm):
    n_rows, d = xr.shape
    ne, _, ff2 = w_gu.shape
    ff = ff2 // 2
    grid_spec = pltpu.PrefetchScalarGridSpec(
        num_scalar_prefetch=2,
        grid=(n_rows // bm,),
        in_specs=[pl.BlockSpec((bm, d), lambda i, be, nb: (i, 0)),
                  pl.BlockSpec((1, d, ff2), lambda i, be, nb: (be[i], 0, 0)),
                  pl.BlockSpec((1, 1, ff2), lambda i, be, nb: (be[i], 0, 0)),
                  pl.BlockSpec((1, ff, d), lambda i, be, nb: (be[i], 0, 0)),
                  pl.BlockSpec((1, 1, d), lambda i, be, nb: (be[i], 0, 0))],
        out_specs=pl.BlockSpec((bm, d), lambda i, be, nb: (i, 0)),
        scratch_shapes=[pltpu.VMEM((d, ff2), BF16), pltpu.VMEM((ff, d), BF16)],
    )
    return pl.pallas_call(
        functools.partial(_expert_kernel, ff=ff),
        grid_spec=grid_spec,
        out_shape=jax.ShapeDtypeStruct((n_rows, d), F32),
        compiler_params=_cparams(("arbitrary",)),
        name="moe_experts",
    )(block_e, n_used, xr, w_gu, b_gu.reshape(ne, 1, ff2), w_down, b_down.reshape(ne, 1, d))


def _combine_kernel(dest_ref, dest_next_ref, yr_hbm, g_ref, x1_ref, gate2_ref, gf_ref, out_ref, buf, sems,
                    *, tile, final_norm):
    i = pl.program_id(0)
    slot = i % 2
    n_copy = tile * TOP_K

    def gather(idx_ref, s):
        def start(t, c):
            for k in range(TOP_K):
                pltpu.make_async_copy(yr_hbm.at[pl.ds(idx_ref[t * TOP_K + k], 1)], buf.at[s, k, pl.ds(t, 1)],
                                      sems.at[s]).start()
            return c

        lax.fori_loop(0, tile, start, 0)

    @pl.when(i == 0)
    def _():
        gather(dest_ref, slot)

    @pl.when(i + 1 < pl.num_programs(0))
    def _():
        gather(dest_next_ref, 1 - slot)

    _wait_rows(yr_hbm, n_copy, sems.at[slot])

    y = None
    for k in range(TOP_K):
        term = g_ref[:, k:k + 1] * buf[slot, k]
        y = term if y is None else y + term
    x2 = x1_ref[...] + gate2_ref[...] * y
    if final_norm:
        ms = jnp.mean(x2 * x2, axis=-1, keepdims=True)
        x2 = x2 * lax.rsqrt(ms + RMS_EPS) * gf_ref[...]
    out_ref[...] = x2


def combine(yr, dest_flat, topg, x1, gate2, gf, tile, final_norm):
    n, d = x1.shape
    tok = pl.BlockSpec((tile, d), lambda i: (i, 0))
    steps = n // tile
    return pl.pallas_call(
        functools.partial(_combine_kernel, tile=tile, final_norm=final_norm),
        grid=(steps,),
        in_specs=[pl.BlockSpec((tile * TOP_K,), lambda i: (i,), memory_space=pltpu.SMEM),
                  pl.BlockSpec((tile * TOP_K,), lambda i: (jnp.minimum(i + 1, steps - 1),), memory_space=pltpu.SMEM),
                  pl.BlockSpec(memory_space=pl.ANY),
                  pl.BlockSpec((tile, topg.shape[1]), lambda i: (i, 0)),
                  tok, _row_spec(gate2.shape[0], tile, d), pl.BlockSpec((1, d), lambda i: (0, 0))],
        out_specs=tok,
        out_shape=jax.ShapeDtypeStruct(x1.shape, F32),
        scratch_shapes=[pltpu.VMEM((2, TOP_K, tile, d), F32), pltpu.SemaphoreType.DMA((2,))],
        compiler_params=_cparams(("arbitrary",)),
        name="moe_combine",
    )(dest_flat, dest_flat, yr, topg, x1, gate2, gf)


def moe_block(x1, hn2, topi, topg, topr, cnt, gate2, norm_f_g, final_norm, w_gu, b_gu, w_down, b_down, bm, tile):
    n, d = x1.shape
    ne = w_gu.shape[0]
    assert d == SUBLANES * LANES
    counts = cnt[0, :ne].astype(jnp.int32)
    padded = ((counts + bm - 1) // bm) * bm
    pend = jnp.cumsum(padded)
    pstart = pend - padded
    idx = topi[:, :TOP_K].astype(jnp.int32)
    onehot = (idx[..., None] == jnp.arange(ne, dtype=jnp.int32)).astype(jnp.int32)
    dest = jnp.sum(onehot * pstart, axis=-1) + topr[:, :TOP_K].astype(jnp.int32)
    dest_flat = dest.reshape(-1)
    n_rows = -(-(n * TOP_K) // bm) * bm + ne * bm
    n_blocks = n_rows // bm
    starts = jnp.arange(n_blocks, dtype=jnp.int32) * bm
    block_e = jnp.minimum(jnp.sum((starts[:, None] >= pend[None, :]).astype(jnp.int32), axis=1), ne - 1)
    n_used = (pend[-1:] // bm).astype(jnp.int32)
    last_e = block_e[jnp.maximum(n_used[0] - 1, 0)]
    block_e = jnp.where(jnp.arange(n_blocks) < n_used[0], block_e, last_e).astype(jnp.int32)

    xr = dispatch(hn2, dest_flat, n_rows, tile)
    yr = experts(xr, block_e, n_used, w_gu, b_gu, w_down, b_down, bm)
    return combine(yr, dest_flat, topg, x1, gate2, norm_f_g.reshape(1, d), tile, final_norm)


def _group_forward(x, mod, attend, h0_lanes, t_per_seq, lw, mats, norm_f_g, final_norm, rows):
    n, d = x.shape
    rows = min(rows, n)
    (norm1_g, norm2_g, w_in_b, w_out_b, w_glu_b, b_glu, d_vec, wr_hi, wr_lo, br, w_gu, b_gu, w_down, b_down) = lw
    kmat, smat, rmat, decay_re, decay_im = mats
    shift1, scale1, gate1, shift2, scale2, gate2 = [mod[:, j * d:(j + 1) * d] for j in range(6)]
    aw = (w_in_b.shape[1] - w_glu_b.shape[0]) // 3
    q_dtype = BF16 if h0_lanes is None else F32
    q, k, v, kb, vb, u = inproj(x, shift1, scale1, norm1_g, w_in_b, aw, q_dtype, rows)
    o_attn = attend(q, k, v, kb, vb)

    sw = u.shape[1]
    nb = sw // LANES
    u2 = u.reshape(n // SSM_CHUNK, SSM_CHUNK * sw)
    nc = u2.shape[0]
    crow = min(256, nc)
    if h0_lanes is None:
        s = ssm_state(u2, smat, crow)
        hs, hend = ssm_scan(s, decay_re, decay_im, crow, nb)
    else:
        assert t_per_seq == SSM_CHUNK
        hs = h0_lanes
        hend = ssm_state(u2, smat, crow, h0_lanes, decay_re, decay_im)
    y_ssm = ssm_out(u2, hs, kmat, rmat, d_vec, w_glu_b, b_glu, crow).reshape(n, sw)

    x1, hn2, topi, topg, topr, cnt = postmix(x, o_attn, y_ssm, w_out_b, gate1, norm2_g, scale2, shift2,
                                             wr_hi, wr_lo, br, rows)
    y = moe_block(x1, hn2, topi, topg, topr, cnt, gate2, norm_f_g, final_norm, w_gu, b_gu, w_down, b_down,
                  bm=256, tile=256)
    return y, k, v, hend


def kernel(x_prompt, x_sample, c_prompt, c_sample, cache_k, cache_v, state_ssm_re, state_ssm_im, page_table, norm1_g, norm2_g, w_ada, b_ada, w_in, w_out, sb_bias, lam_re, lam_im, log_dt, ssm_b_re, ssm_b_im, ssm_c_re, ssm_c_im, ssm_d, w_glu, b_glu, w_router, b_router, w_gu, b_gu, w_down, b_down, norm_f_g):
    depth = w_in.shape[0]
    bp, tp, d = x_prompt.shape
    bs, ts, _ = x_sample.shape
    assert bp == 1, "the prompt group is handled as one long sequence"
    n_heads = cache_k.shape[3]
    aw = n_heads * HEAD_DIM
    g, p = lam_re.shape[1:]
    nb = g // GROUP_BLOCK
    ne = w_router.shape[-1]
    n_pool, page = cache_k.shape[1:3]
    rows = 512

    xp = x_prompt.reshape(bp * tp, d)
    xs = x_sample.reshape(bs * ts, d)
    n_c = bp + bs
    c_all = jnp.concatenate([c_prompt, c_sample, jnp.zeros((-n_c % SUBLANES, d), F32)], axis=0)

    outs = {name: [] for name in ("kp", "vp", "rp", "ip", "ks", "vs", "rs", "is")}
    for l in range(depth):
        mod = ada_mod(c_all, w_ada[l], b_ada[l])
        mod_p = mod[:bp]
        mod_s = jnp.repeat(mod[bp:n_c], ts, axis=0)
        mats = _ssm_mats(lam_re[l], lam_im[l], log_dt[l], ssm_b_re[l], ssm_b_im[l], ssm_c_re[l], ssm_c_im[l])
        wr = jnp.pad(w_router[l], ((0, 0), (0, ROUTER_PAD - ne)))
        wr_hi = wr.astype(BF16)
        wr_lo = (wr - wr_hi.astype(F32)).astype(BF16)
        br = jnp.concatenate([b_router[l], jnp.full((ROUTER_PAD - ne,), NEG_BIG, F32)]).reshape(1, ROUTER_PAD)
        lw = (norm1_g[l].reshape(1, d), norm2_g[l].reshape(1, d), w_in[l].astype(BF16), w_out[l].astype(BF16),
              w_glu[l].astype(BF16), b_glu[l].reshape(1, -1), ssm_d[l].reshape(1, -1), wr_hi, wr_lo, br,
              w_gu[l], b_gu[l], w_down[l], b_down[l])
        bias = sb_bias[l]

        def attend_p(q, k, v, kb, vb):
            return attn_prompt(q, kb, vb, bias, nsb=min(8, tp // KEY_BLOCK))

        ck = cache_k[l].transpose(0, 2, 3, 1).reshape(n_pool, aw, page)
        cv = cache_v[l].transpose(0, 2, 3, 1).reshape(n_pool, aw, page)

        def attend_s(q, k, v, kb, vb):
            return attn_sample(q, k, v, ck, cv, page_table, bias, ts, group=math.gcd(8, page_table.shape[1]))

        h0 = _state_to_lanes(state_ssm_re[l], state_ssm_im[l], nb)
        xp, kp, vp, hp = _group_forward(xp, mod_p, attend_p, None, tp, lw, mats, norm_f_g, l == depth - 1, rows)
        xs, ks, vs, hs = _group_forward(xs, mod_s, attend_s, h0, ts, lw, mats, norm_f_g, l == depth - 1, rows)
        rp, ip = _lanes_to_state(hp, g, p)
        rs, is_ = _lanes_to_state(hs, g, p)
        outs["kp"].append(kp.reshape(bp, tp, n_heads, HEAD_DIM))
        outs["vp"].append(vp.reshape(bp, tp, n_heads, HEAD_DIM))
        outs["rp"].append(rp)
        outs["ip"].append(ip)
        outs["ks"].append(ks.reshape(bs, ts, n_heads, HEAD_DIM))
        outs["vs"].append(vs.reshape(bs, ts, n_heads, HEAD_DIM))
        outs["rs"].append(rs)
        outs["is"].append(is_)
    st = lambda name: jnp.stack(outs[name])
    return (xp.reshape(bp, tp, d), xs.reshape(bs, ts, d), st("kp"), st("vp"), st("rp"), st("ip"),
            st("ks"), st("vs"), st("rs"), st("is"))
```

```python
import functools
import math

import jax
import jax.numpy as jnp
from jax import lax
from jax.experimental import pallas as pl
from jax.experimental.pallas import tpu as pltpu

F32 = jnp.float32
BF16 = jnp.bfloat16

HEAD_DIM = 64
SSM_GROUP = 16
TOP_K = 4
SWIGLU_LIMIT = 7.0
SWIGLU_ALPHA = 1.702
RMS_EPS = 1e-6

LANES = 128
SUBLANES = 8
VMEM_LIMIT_BYTES = 56 * 1024 * 1024

KEY_BLOCK = LANES
SSM_CHUNK = 8
GROUP_BLOCK = LANES // SSM_GROUP
ROUTER_PAD = LANES
EXPERT_ROWS = 512
NEG_BIG = -1e30
EXP_CAP = 1e30
LOG2E = math.log2(math.e)
PAGE_SLOTS = 3

def _cparams(semantics):
    return pltpu.CompilerParams(dimension_semantics=semantics, vmem_limit_bytes=VMEM_LIMIT_BYTES)


def _dot(a, b):
    return jnp.dot(a, b, preferred_element_type=F32)


def _dot_nt(a, b):
    return lax.dot_general(a, b, (((1,), (1,)), ((), ())), preferred_element_type=F32)


def _split_bf16(x):
    hi = x.astype(BF16)
    lo = (x - hi.astype(F32)).astype(BF16)
    return hi, lo


def _ada_kernel(c_ref, w_ref, b_ref, o_ref):
    o_ref[...] = _dot(c_ref[...].astype(BF16), w_ref[...].astype(BF16)) + b_ref[...]


def ada_mod(c, w_ada, b_ada):
    n, d = c.shape
    n_out = w_ada.shape[1]
    return pl.pallas_call(
        _ada_kernel,
        grid=(n_out // d,),
        in_specs=[pl.BlockSpec((n, d), lambda j: (0, 0)),
                  pl.BlockSpec((d, d), lambda j: (0, j)),
                  pl.BlockSpec((1, d), lambda j: (0, j))],
        out_specs=pl.BlockSpec((n, d), lambda j: (0, j)),
        out_shape=jax.ShapeDtypeStruct((n, n_out), F32),
        compiler_params=_cparams(("arbitrary",)),
        name="ada_mod",
    )(c, w_ada, b_ada.reshape(1, n_out))


def _inproj_kernel(x_ref, shift_ref, scale_ref, g_ref, w_ref, q_ref, k_ref, v_ref, kb_ref, vb_ref, u_ref, *, aw):
    x = x_ref[...]
    ms = jnp.mean(x * x, axis=-1, keepdims=True)
    hn = x * lax.rsqrt(ms + RMS_EPS) * g_ref[...]
    hn = hn * (1.0 + scale_ref[...]) + shift_ref[...]
    proj = _dot(hn.astype(BF16), w_ref[...])
    q_ref[...] = (proj[:, :aw] * (HEAD_DIM ** -0.5 * LOG2E)).astype(q_ref.dtype)
    k = proj[:, aw:2 * aw]
    v = proj[:, 2 * aw:3 * aw]
    k_ref[...] = k
    v_ref[...] = v
    kb_ref[...] = k.astype(BF16)
    vb_ref[...] = v.astype(BF16)
    u_ref[...] = proj[:, 3 * aw:]


def _row_spec(n_mod_rows, rows, d):
    if n_mod_rows == 1:
        return pl.BlockSpec((1, d), lambda i: (0, 0))
    return pl.BlockSpec((rows, d), lambda i: (i, 0))


def inproj(x, shift, scale, g, w_in_b, aw, q_dtype, rows):
    n, d = x.shape
    pw = w_in_b.shape[1]
    sw = pw - 3 * aw
    tok = lambda width: pl.BlockSpec((rows, width), lambda i: (i, 0))
    return pl.pallas_call(
        functools.partial(_inproj_kernel, aw=aw),
        grid=(n // rows,),
        in_specs=[tok(d), _row_spec(shift.shape[0], rows, d), _row_spec(scale.shape[0], rows, d),
                  pl.BlockSpec((1, d), lambda i: (0, 0)),
                  pl.BlockSpec((d, pw), lambda i: (0, 0))],
        out_specs=[tok(aw), tok(aw), tok(aw), tok(aw), tok(aw), tok(sw)],
        out_shape=[jax.ShapeDtypeStruct((n, aw), q_dtype),
                   jax.ShapeDtypeStruct((n, aw), F32), jax.ShapeDtypeStruct((n, aw), F32),
                   jax.ShapeDtypeStruct((n, aw), BF16), jax.ShapeDtypeStruct((n, aw), BF16),
                   jax.ShapeDtypeStruct((n, sw), F32)],
        compiler_params=_cparams(("arbitrary",)),
        name="inproj",
    )(x, shift, scale, g, w_in_b)


def _tri_ones():
    j = lax.broadcasted_iota(jnp.int32, (KEY_BLOCK, 2 * KEY_BLOCK), 0)
    s = lax.broadcasted_iota(jnp.int32, (KEY_BLOCK, 2 * KEY_BLOCK), 1)
    return jnp.where((j > s) | (s >= KEY_BLOCK), 1.0, 0.0).astype(BF16)


def _sb_logits(z, mask):
    drop = jnp.maximum(jnp.log(1.0 + jnp.minimum(jnp.exp2(z), EXP_CAP)) * LOG2E, z)
    if mask is not None:
        drop = jnp.where(mask, drop, 0.0)
    return z - drop, drop.astype(BF16)


def _weights_from_log2(x):
    return jnp.exp2(x).astype(BF16)


def _sb_finish(log_beta, drop_b, carry, tri, mask):
    groups = drop_b.shape[1] // KEY_BLOCK
    later, total = [], []
    for g in range(groups):
        cs = _dot(drop_b[:, g * KEY_BLOCK:(g + 1) * KEY_BLOCK], tri)
        later.append(cs[:, :KEY_BLOCK])
        total.append(cs[:, KEY_BLOCK:])
    later = jnp.concatenate(later, axis=1) if groups > 1 else later[0]
    total = jnp.concatenate(total, axis=1) if groups > 1 else total[0]
    a = _weights_from_log2(log_beta - (later + carry))
    if mask is not None:
        a = jnp.where(mask, a, jnp.zeros_like(a))
    return a, total


def _sb_weights(z, carry, tri, mask):
    if mask is not None:
        mask = jnp.concatenate([mask] * (z.shape[1] // KEY_BLOCK), axis=1)
    log_beta, drop_b = _sb_logits(z, mask)
    return _sb_finish(log_beta, drop_b, carry, tri, mask)


def _attn_prompt_kernel(kbias_ref, q_ref, k_ref, v_ref, tri_ref, o_ref, acc_ref, carry_ref, lb_scr, drop_scr,
                        qext_scr, *, nsb):
    it = pl.program_id(1)
    tq = nsb * KEY_BLOCK
    tri = tri_ref[...]
    ones_lanes = lax.broadcasted_iota(jnp.int32, (tq, LANES), 1) < 2
    qext_scr[:, :LANES] = q_ref[...]
    qext_scr[:, LANES:] = jnp.where(ones_lanes, 1.0, 0.0).astype(BF16)
    kbias = kbias_ref[0]
    acc_ref[...] = jnp.zeros_like(acc_ref)
    carry_ref[...] = jnp.zeros_like(carry_ref)
    lane = lax.broadcasted_iota(jnp.int32, (KEY_BLOCK, LANES), 1)
    first_head = lane < HEAD_DIM

    def stacked(ref, block_index):
        st = pl.multiple_of(block_index * KEY_BLOCK, KEY_BLOCK)
        blk = ref[pl.ds(st, KEY_BLOCK), :]
        zero = jnp.zeros_like(blk)
        return jnp.concatenate([jnp.where(first_head, blk, zero), jnp.where(first_head, zero, blk)], axis=0)

    for c in reversed(range(nsb)):
        r0 = c * KEY_BLOCK
        rr = lax.broadcasted_iota(jnp.int32, (tq - r0, LANES), 0)
        ll = lax.broadcasted_iota(jnp.int32, (tq - r0, LANES), 1)
        mask = (rr >= KEY_BLOCK) | (ll < rr)
        z = _dot_nt(qext_scr[r0:, :], jnp.concatenate([stacked(k_ref, it * nsb + c), kbias], axis=1))
        a, tot = _sb_weights(z, carry_ref[r0:, :], tri, mask)
        acc_ref[r0:, :] += _dot(a, stacked(v_ref, it * nsb + c))
        carry_ref[r0:, :] += tot

    n_full = it * nsb

    def first_half(block_index, slot):
        z = _dot_nt(qext_scr[...], jnp.concatenate([stacked(k_ref, block_index), kbias], axis=1))
        log_beta, drop_b = _sb_logits(z, None)
        lb_scr[slot] = log_beta
        drop_scr[slot] = drop_b

    def second_half(block_index, slot):
        a, tot = _sb_finish(lb_scr[slot], drop_scr[slot], carry_ref[...], tri, None)
        acc_ref[...] += _dot(a, stacked(v_ref, block_index))
        carry_ref[...] += tot

    @pl.when(n_full > 0)
    def _():
        per_trip = math.gcd(nsb, 4)
        first_half(n_full - 1, 0)

        def body(jj, c):
            blk = n_full - 1 - per_trip * jj
            for u in range(per_trip):
                first_half(blk - u - 1, (u + 1) % 2)
                second_half(blk - u, u % 2)
            return c

        lax.fori_loop(0, n_full // per_trip - 1, body, 0)
        for u in range(per_trip - 1):
            first_half(per_trip - 2 - u, (u + 1) % 2)
            second_half(per_trip - 1 - u, u % 2)
        second_half(0, (per_trip - 1) % 2)

    o_ref[...] = acc_ref[...].astype(o_ref.dtype)


def attn_prompt(q, kb, vb, bias, nsb):
    n, aw = q.shape
    tq = nsb * KEY_BLOCK
    n_pairs = aw // LANES
    b_hi, b_lo = _split_bf16(bias.astype(F32) * LOG2E)
    lane = jnp.arange(LANES)
    kbias = jnp.where(lane == 0, b_hi[:, None, None], jnp.where(lane == 1, b_lo[:, None, None], 0)).astype(BF16)
    kbias = jnp.broadcast_to(kbias, (bias.shape[0], KEY_BLOCK, LANES)).reshape(n_pairs, 2 * KEY_BLOCK, LANES)
    tri = _tri_ones()
    return pl.pallas_call(
        functools.partial(_attn_prompt_kernel, nsb=nsb),
        grid=(n_pairs, n // tq),
        in_specs=[pl.BlockSpec((1, 2 * KEY_BLOCK, LANES), lambda hp, i: (hp, 0, 0)),
                  pl.BlockSpec((tq, LANES), lambda hp, i: (i, hp)),
                  pl.BlockSpec((n, LANES), lambda hp, i: (0, hp)),
                  pl.BlockSpec((n, LANES), lambda hp, i: (0, hp)),
                  pl.BlockSpec(tri.shape, lambda hp, i: (0, 0))],
        out_specs=pl.BlockSpec((tq, LANES), lambda hp, i: (i, hp)),
        out_shape=jax.ShapeDtypeStruct((n, aw), BF16),
        scratch_shapes=[pltpu.VMEM((tq, LANES), F32), pltpu.VMEM((tq, 2 * KEY_BLOCK), F32),
                        pltpu.VMEM((2, tq, 2 * KEY_BLOCK), F32), pltpu.VMEM((2, tq, 2 * KEY_BLOCK), BF16),
                        pltpu.VMEM((tq, 2 * LANES), BF16)],
        compiler_params=_cparams(("arbitrary", "arbitrary")),
        name="attn_prompt",
    )(kbias, q, kb, vb, tri)


def _attn_sample_kernel(pt_ref, q_ref, kn_ref, vn_ref, ck_hbm, cv_hbm, tri_ref, bias_ref, o_ref,
                        kbuf, vbuf, sems, qbd_ref, acc_ref, carry_ref, *, group, n_groups, n_heads, t_new):
    b = pl.program_id(0)
    total = pl.num_programs(0) * n_groups
    m = n_heads * t_new
    aw = n_heads * HEAD_DIM
    tri = tri_ref[...]
    bias = bias_ref[...]

    def request(s):
        slot = s % PAGE_SLOTS
        for i in range(group):
            page = pt_ref[s * group + i]
            pltpu.make_async_copy(ck_hbm.at[page], kbuf.at[slot, i], sems.at[slot, 0]).start()
            pltpu.make_async_copy(cv_hbm.at[page], vbuf.at[slot, i], sems.at[slot, 1]).start()

    def wait_group(slot):
        pltpu.make_async_copy(kbuf.at[slot], kbuf.at[slot], sems.at[slot, 0]).wait()
        pltpu.make_async_copy(vbuf.at[slot], vbuf.at[slot], sems.at[slot, 1]).wait()

    @pl.when(b == 0)
    def _():
        for s in range(PAGE_SLOTS - 1):
            request(s)

    q = q_ref[...].astype(BF16)
    qt = jnp.concatenate([q] * n_heads, axis=0)
    rw = lax.broadcasted_iota(jnp.int32, (m, aw), 0)
    ln = lax.broadcasted_iota(jnp.int32, (m, aw), 1)
    qbd_ref[...] = jnp.where(ln // HEAD_DIM == rw // t_new, qt, jnp.zeros_like(qt))
    pad = jnp.zeros((KEY_BLOCK - t_new, aw), F32)
    kn = jnp.concatenate([kn_ref[...], pad], axis=0).astype(BF16)
    vn = jnp.concatenate([vn_ref[...], pad], axis=0).astype(BF16)
    r2 = lax.broadcasted_iota(jnp.int32, (m, KEY_BLOCK), 0)
    l2 = lax.broadcasted_iota(jnp.int32, (m, KEY_BLOCK), 1)
    z = _dot_nt(qbd_ref[...], kn) + bias
    a, tot = _sb_weights(z, 0.0, tri, l2 < (r2 % t_new))
    acc_ref[...] = _dot(a, vn)
    carry_ref[...] = tot

    def body(g, c):
        s = b * n_groups + g

        @pl.when(s + PAGE_SLOTS - 1 < total)
        def _():
            request(s + PAGE_SLOTS - 1)

        slot = s % PAGE_SLOTS
        wait_group(slot)
        kt = jnp.concatenate([kbuf[slot, i].astype(BF16) for i in range(group)], axis=1)
        vt = jnp.concatenate([vbuf[slot, i].astype(BF16) for i in range(group)], axis=1)
        z = _dot(qbd_ref[...], kt) + jnp.concatenate([bias] * group, axis=1)
        log_beta, drop_b = _sb_logits(z, None)
        carry = carry_ref[...]
        shift = []
        for i in range(group):
            cs = _dot(drop_b[:, i * KEY_BLOCK:(i + 1) * KEY_BLOCK], tri)
            shift.append(cs[:, :KEY_BLOCK] + carry)
            carry = carry + cs[:, KEY_BLOCK:]
        a = _weights_from_log2(log_beta - jnp.concatenate(shift, axis=1))
        acc_ref[...] += _dot_nt(a, vt)
        carry_ref[...] = carry
        return c

    lax.fori_loop(0, n_groups, body, 0)

    acc = acc_ref[...]
    lo = lax.broadcasted_iota(jnp.int32, (t_new, aw), 1)
    o = jnp.zeros((t_new, aw), F32)
    for h in range(n_heads):
        o = o + jnp.where(lo // HEAD_DIM == h, acc[h * t_new:(h + 1) * t_new, :], 0.0)
    o_ref[...] = o.astype(o_ref.dtype)


def attn_sample(q, k_new, v_new, cache_k, cache_v, page_table, bias, t_new, group):
    n, aw = q.shape
    bsz, n_pages = page_table.shape
    n_heads = aw // HEAD_DIM
    page = cache_k.shape[2]
    n_groups = n_pages // group
    assert page == KEY_BLOCK and t_new == SUBLANES and n_pages % group == 0 and bsz * n_groups >= PAGE_SLOTS
    m = n_heads * t_new
    bias_rows = jnp.broadcast_to(jnp.repeat(bias * LOG2E, t_new)[:, None], (m, KEY_BLOCK)).astype(F32)
    pages_recent_first = page_table[:, ::-1].reshape(-1)
    tok = pl.BlockSpec((t_new, aw), lambda b, pt: (b, 0))
    ring = pltpu.VMEM((PAGE_SLOTS, group, aw, page), F32)
    grid_spec = pltpu.PrefetchScalarGridSpec(
        num_scalar_prefetch=1,
        grid=(bsz,),
        in_specs=[tok, tok, tok, pl.BlockSpec(memory_space=pl.ANY), pl.BlockSpec(memory_space=pl.ANY),
                  pl.BlockSpec((KEY_BLOCK, 2 * KEY_BLOCK), lambda b, pt: (0, 0)),
                  pl.BlockSpec((m, KEY_BLOCK), lambda b, pt: (0, 0))],
        out_specs=tok,
        scratch_shapes=[ring, ring, pltpu.SemaphoreType.DMA((PAGE_SLOTS, 2)),
                        pltpu.VMEM((m, aw), BF16), pltpu.VMEM((m, aw), F32), pltpu.VMEM((m, KEY_BLOCK), F32)],
    )
    return pl.pallas_call(
        functools.partial(_attn_sample_kernel, group=group, n_groups=n_groups, n_heads=n_heads, t_new=t_new),
        grid_spec=grid_spec,
        out_shape=jax.ShapeDtypeStruct((n, aw), BF16),
        compiler_params=_cparams(("arbitrary",)),
        name="attn_sample",
    )(pages_recent_first, q, k_new, v_new, cache_k, cache_v, _tri_ones(), bias_rows)


def _ssm_mats(lam_re, lam_im, log_dt, b_re, b_im, c_re, c_im):
    g, p = lam_re.shape
    hh = b_re.shape[-1]
    el = SSM_CHUNK
    nb = g // GROUP_BLOCK
    dt = jnp.exp(log_dt)[:, None]
    ar, ai = lam_re * dt, lam_im * dt
    lbr, lbi = jnp.exp(ar) * jnp.cos(ai), jnp.exp(ar) * jnp.sin(ai)
    den = lam_re * lam_re + lam_im * lam_im
    fr = ((lbr - 1.0) * lam_re + lbi * lam_im) / den
    fi = (lbi * lam_re - (lbr - 1.0) * lam_im) / den
    bbr = fr[..., None] * b_re - fi[..., None] * b_im
    bbi = fr[..., None] * b_im + fi[..., None] * b_re
    n = jnp.arange(el + 1, dtype=F32)[:, None, None]
    pr = jnp.exp(ar[None] * n) * jnp.cos(ai[None] * n)
    pi = jnp.exp(ar[None] * n) * jnp.sin(ai[None] * n)
    eye = jnp.eye(GROUP_BLOCK, dtype=F32)

    cpr = c_re[None] * pr[:el, :, None, :] - c_im[None] * pi[:el, :, None, :]
    cpi = c_re[None] * pi[:el, :, None, :] + c_im[None] * pr[:el, :, None, :]
    kd = jnp.einsum('dgip,gpj->dgij', cpr, bbr) - jnp.einsum('dgip,gpj->dgij', cpi, bbi)
    kmat = jnp.einsum('dkgij,gh->dkgjhi', kd.reshape(el, nb, GROUP_BLOCK, hh, hh), eye)
    kmat = kmat.reshape(el, nb, LANES, LANES)

    rev = pr[el - 1 - jnp.arange(el)], pi[el - 1 - jnp.arange(el)]
    scr = rev[0][..., None] * bbr[None] - rev[1][..., None] * bbi[None]
    sci = rev[0][..., None] * bbi[None] + rev[1][..., None] * bbr[None]

    def blk_s(a):
        a = jnp.einsum('skgpj,gh->skgjhp', a.reshape(el, nb, GROUP_BLOCK, p, hh), eye)
        return a.reshape(el, nb, LANES, GROUP_BLOCK * p)

    smat = jnp.concatenate([blk_s(scr), blk_s(sci)], axis=-1)

    c1r = c_re[None] * pr[1:, :, None, :] - c_im[None] * pi[1:, :, None, :]
    c1i = c_re[None] * pi[1:, :, None, :] + c_im[None] * pr[1:, :, None, :]

    def blk_r(a):
        a = jnp.einsum('tkgip,gh->tkgphi', a.reshape(el, nb, GROUP_BLOCK, hh, p), eye)
        return a.reshape(el, nb, GROUP_BLOCK * p, LANES)

    rmat = jnp.concatenate([blk_r(c1r), -blk_r(c1i)], axis=-2)

    decay_re = pr[el].reshape(1, g * p)
    decay_im = pi[el].reshape(1, g * p)
    return kmat.astype(BF16), smat.astype(BF16), rmat.astype(BF16), decay_re, decay_im


def _state_to_lanes(h_re, h_im, nb):
    b = h_re.shape[0]
    return jnp.stack([h_re.reshape(b, nb, -1), h_im.reshape(b, nb, -1)], axis=2).reshape(b, -1)


def _lanes_to_state(h, g, p):
    b = h.shape[0]
    nb = g // GROUP_BLOCK
    h = h.reshape(b, nb, 2, GROUP_BLOCK, p)
    return h[:, :, 0].reshape(b, g, p), h[:, :, 1].reshape(b, g, p)


def _ssm_state_kernel(*refs, has_h0, nb, sw, half):
    if has_h0:
        u_ref, smat_ref, h0_ref, dre_ref, dim_ref, s_ref = refs
    else:
        u_ref, smat_ref, s_ref = refs
    for k in range(nb):
        acc = None
        for s in range(SSM_CHUNK):
            ub = u_ref[:, s * sw + k * LANES: s * sw + (k + 1) * LANES].astype(BF16)
            d = _dot(ub, smat_ref[s, k])
            acc = d if acc is None else acc + d
        base = 2 * half * k
        if has_h0:
            hr = h0_ref[:, base:base + half]
            hi = h0_ref[:, base + half:base + 2 * half]
            dr = dre_ref[:, k * half:(k + 1) * half]
            di = dim_ref[:, k * half:(k + 1) * half]
            s_ref[:, base:base + half] = acc[:, :half] + dr * hr - di * hi
            s_ref[:, base + half:base + 2 * half] = acc[:, half:] + dr * hi + di * hr
        else:
            s_ref[:, base:base + 2 * half] = acc


def ssm_state(u2, smat, rows, h0=None, decay_re=None, decay_im=None):
    nc, width = u2.shape
    sw = width // SSM_CHUNK
    nb = sw // LANES
    half = smat.shape[-1] // 2
    sl = nb * 2 * half
    has_h0 = h0 is not None
    row = lambda w: pl.BlockSpec((rows, w), lambda i: (i, 0))
    in_specs = [row(width), pl.BlockSpec(smat.shape, lambda i: (0, 0, 0, 0))]
    args = [u2, smat]
    if has_h0:
        in_specs += [row(sl), pl.BlockSpec((1, nb * half), lambda i: (0, 0)),
                     pl.BlockSpec((1, nb * half), lambda i: (0, 0))]
        args += [h0, decay_re, decay_im]
    return pl.pallas_call(
        functools.partial(_ssm_state_kernel, has_h0=has_h0, nb=nb, sw=sw, half=half),
        grid=(nc // rows,),
        in_specs=in_specs,
        out_specs=row(sl),
        out_shape=jax.ShapeDtypeStruct((nc, sl), F32),
        compiler_params=_cparams(("arbitrary",)),
        name="ssm_state",
    )(*args)


def _ssm_scan_kernel(s_ref, dre_ref, dim_ref, hs_ref, hend_ref, h_scr, *, nb, half):
    @pl.when(pl.program_id(0) == 0)
    def _():
        h_scr[...] = jnp.zeros_like(h_scr)

    dr = dre_ref[...]
    di = dim_ref[...]

    def body(r, h):
        hs_ref[pl.ds(r, 1), :] = h
        s = s_ref[pl.ds(r, 1), :]
        parts = []
        for k in range(nb):
            base = 2 * half * k
            hr, hi = h[:, base:base + half], h[:, base + half:base + 2 * half]
            ar, ai = dr[:, k * half:(k + 1) * half], di[:, k * half:(k + 1) * half]
            parts.append(ar * hr - ai * hi + s[:, base:base + half])
            parts.append(ar * hi + ai * hr + s[:, base + half:base + 2 * half])
        return jnp.concatenate(parts, axis=1)

    h = lax.fori_loop(0, s_ref.shape[0], body, h_scr[...])
    h_scr[...] = h
    hend_ref[...] = h


def ssm_scan(s, decay_re, decay_im, rows, nb):
    nc, sl = s.shape
    n_half_total = decay_re.shape[1]
    half = n_half_total // nb
    row = pl.BlockSpec((rows, sl), lambda i: (i, 0))
    vec = pl.BlockSpec((1, n_half_total), lambda i: (0, 0))
    return pl.pallas_call(
        functools.partial(_ssm_scan_kernel, nb=nb, half=half),
        grid=(nc // rows,),
        in_specs=[row, vec, vec],
        out_specs=[row, pl.BlockSpec((1, sl), lambda i: (0, 0))],
        out_shape=[jax.ShapeDtypeStruct((nc, sl), F32), jax.ShapeDtypeStruct((1, sl), F32)],
        scratch_shapes=[pltpu.VMEM((1, sl), F32)],
        compiler_params=_cparams(("arbitrary",)),
        name="ssm_scan",
    )(s, decay_re, decay_im)


def _gelu_exact(y):
    return 0.5 * y * (1.0 + lax.erf(y * (0.5 ** 0.5)))


def _ssm_out_kernel(u_ref, hs_ref, kmat_ref, rmat_ref, d_ref, wglu_ref, bglu_ref, y_ref, *, nb, sw, half):
    hb = [hs_ref[:, 2 * half * k:2 * half * (k + 1)].astype(BF16) for k in range(nb)]
    ub = [[u_ref[:, s * sw + k * LANES: s * sw + (k + 1) * LANES].astype(BF16) for k in range(nb)]
          for s in range(SSM_CHUNK)]
    wglu = wglu_ref[...]
    for t in range(SSM_CHUNK):
        cols = []
        for k in range(nb):
            acc = _dot(hb[k], rmat_ref[t, k])
            for s in range(t + 1):
                acc = acc + _dot(ub[s][k], kmat_ref[t - s, k])
            cols.append(acc)
        y = jnp.concatenate(cols, axis=1) + d_ref[...] * u_ref[:, t * sw:(t + 1) * sw]
        g = _gelu_exact(y)
        gate = jax.nn.sigmoid(_dot(g.astype(BF16), wglu) + bglu_ref[...])
        y_ref[:, t * sw:(t + 1) * sw] = (g * gate).astype(y_ref.dtype)


def ssm_out(u2, hs, kmat, rmat, d_vec, w_glu_b, b_glu, rows):
    nc, width = u2.shape
    sw = width // SSM_CHUNK
    nb = sw // LANES
    sl = hs.shape[1]
    half = sl // (2 * nb)
    row = lambda w: pl.BlockSpec((rows, w), lambda i: (i, 0))
    const = lambda a: pl.BlockSpec(a.shape, lambda i: (0,) * a.ndim)
    return pl.pallas_call(
        functools.partial(_ssm_out_kernel, nb=nb, sw=sw, half=half),
        grid=(nc // rows,),
        in_specs=[row(width), row(sl), const(kmat), const(rmat), const(d_vec), const(w_glu_b), const(b_glu)],
        out_specs=row(width),
        out_shape=jax.ShapeDtypeStruct((nc, width), BF16),
        compiler_params=_cparams(("arbitrary",)),
        name="ssm_out",
    )(u2, hs, kmat, rmat, d_vec, w_glu_b, b_glu)


def _postmix_kernel(x_ref, o_ref, y_ref, wo_ref, gate1_ref, g2_ref, scale2_ref, shift2_ref, wrh_ref, wrl_ref,
                    br_ref, tril_ref, x1_ref, hn2_ref, topi_ref, topg_ref, topr_ref, cnt_ref, carry_ref, *, aw):
    @pl.when(pl.program_id(0) == 0)
    def _():
        carry_ref[...] = jnp.zeros_like(carry_ref)

    mix = _dot(o_ref[...], wo_ref[:aw, :]) + _dot(y_ref[...], wo_ref[aw:, :])
    x1 = x_ref[...] + gate1_ref[...] * mix
    x1_ref[...] = x1
    ms = jnp.mean(x1 * x1, axis=-1, keepdims=True)
    hn2 = x1 * lax.rsqrt(ms + RMS_EPS) * g2_ref[...]
    hn2 = hn2 * (1.0 + scale2_ref[...]) + shift2_ref[...]
    hn2_ref[...] = hn2

    hh, hl = _split_bf16(hn2)
    wrh = wrh_ref[...]
    logits = _dot(hh, wrh) + _dot(hl, wrh) + _dot(hh, wrl_ref[...]) + br_ref[...]

    rows = logits.shape[0]
    lane = lax.broadcasted_iota(jnp.int32, (rows, ROUTER_PAD), 1)
    work = logits
    sel = jnp.zeros((rows, ROUTER_PAD), F32)
    picks, vals, idxs = [], [], []
    for _ in range(TOP_K):
        m = jnp.max(work, axis=-1, keepdims=True)
        idx = jnp.min(jnp.where(work == m, lane, ROUTER_PAD), axis=-1, keepdims=True)
        pick = lane == idx
        picks.append(pick)
        vals.append(m)
        idxs.append(idx)
        sel = jnp.where(pick, 1.0, sel)
        work = jnp.where(pick, -jnp.inf, work)

    rank = _dot(tril_ref[...], sel.astype(BF16)) + carry_ref[...]
    carry_ref[...] = rank[rows - 1:rows, :] + sel[rows - 1:rows, :]
    cnt_ref[...] = carry_ref[...]

    es = [jnp.exp(v - vals[0]) for v in vals]
    den = es[0]
    for e in es[1:]:
        den = den + e
    topi = jnp.zeros((rows, ROUTER_PAD), F32)
    topg = jnp.zeros((rows, ROUTER_PAD), F32)
    topr = jnp.zeros((rows, ROUTER_PAD), F32)
    for r in range(TOP_K):
        rk = jnp.sum(jnp.where(picks[r], rank, 0.0), axis=-1, keepdims=True)
        topi = jnp.where(lane == r, idxs[r].astype(F32), topi)
        topg = jnp.where(lane == r, es[r] / den, topg)
        topr = jnp.where(lane == r, rk, topr)
    topi_ref[...] = topi
    topg_ref[...] = topg
    topr_ref[...] = topr


def postmix(x, o_attn, y_ssm, w_out_b, gate1, g2, scale2, shift2, wr_hi, wr_lo, br, rows):
    n, d = x.shape
    aw = o_attn.shape[1]
    tok = lambda width: pl.BlockSpec((rows, width), lambda i: (i, 0))
    const = lambda a: pl.BlockSpec(a.shape, lambda i: (0,) * a.ndim)
    r = lax.broadcasted_iota(jnp.int32, (rows, rows), 0)
    c = lax.broadcasted_iota(jnp.int32, (rows, rows), 1)
    tril = jnp.where(c < r, 1.0, 0.0).astype(BF16)
    lanes_out = jax.ShapeDtypeStruct((n, ROUTER_PAD), F32)
    return pl.pallas_call(
        functools.partial(_postmix_kernel, aw=aw),
        grid=(n // rows,),
        in_specs=[tok(d), tok(aw), tok(y_ssm.shape[1]), const(w_out_b),
                  _row_spec(gate1.shape[0], rows, d), const(g2),
                  _row_spec(scale2.shape[0], rows, d), _row_spec(shift2.shape[0], rows, d),
                  const(wr_hi), const(wr_lo), const(br), const(tril)],
        out_specs=[tok(d), tok(d), tok(ROUTER_PAD), tok(ROUTER_PAD), tok(ROUTER_PAD),
                   pl.BlockSpec((1, ROUTER_PAD), lambda i: (0, 0))],
        out_shape=[jax.ShapeDtypeStruct((n, d), F32), jax.ShapeDtypeStruct((n, d), F32),
                   lanes_out, lanes_out, lanes_out, jax.ShapeDtypeStruct((1, ROUTER_PAD), F32)],
        scratch_shapes=[pltpu.VMEM((1, ROUTER_PAD), F32)],
        compiler_params=_cparams(("arbitrary",)),
        name="postmix",
    )(x, o_attn, y_ssm, w_out_b, gate1, g2, scale2, shift2, wr_hi, wr_lo, br, tril)


def _wait_rows(ref, n_copy, sem):
    span = ref.at[pl.ds(0, n_copy)]
    pltpu.make_async_copy(span, span, sem).wait()


def _dispatch_kernel(dest_ref, hn_ref, xr_in, xr_hbm, stage, sems, *, tile):
    del xr_in
    i = pl.program_id(0)
    slot = i % 2
    n_copy = tile * TOP_K
    stage[slot] = hn_ref[...]

    def start(t, c):
        for k in range(TOP_K):
            pltpu.make_async_copy(stage.at[slot, pl.ds(t, 1)], xr_hbm.at[pl.ds(dest_ref[t * TOP_K + k], 1)],
                                  sems.at[slot]).start()
        return c

    lax.fori_loop(0, tile, start, 0)

    @pl.when(i > 0)
    def _():
        _wait_rows(xr_hbm, n_copy, sems.at[1 - slot])

    @pl.when(i == pl.num_programs(0) - 1)
    def _():
        _wait_rows(xr_hbm, n_copy, sems.at[slot])


def dispatch(hn, dest_flat, xr0, tile):
    n, d = hn.shape
    return pl.pallas_call(
        functools.partial(_dispatch_kernel, tile=tile),
        grid=(n // tile,),
        in_specs=[pl.BlockSpec((tile * TOP_K,), lambda i: (i,), memory_space=pltpu.SMEM),
                  pl.BlockSpec((tile, d), lambda i: (i, 0)),
                  pl.BlockSpec(memory_space=pl.ANY)],
        out_specs=pl.BlockSpec(memory_space=pl.ANY),
        out_shape=jax.ShapeDtypeStruct(xr0.shape, xr0.dtype),
        scratch_shapes=[pltpu.VMEM((2, tile, d), hn.dtype), pltpu.SemaphoreType.DMA((2,))],
        input_output_aliases={2: 0},
        compiler_params=_cparams(("arbitrary",)),
        name="moe_dispatch",
    )(dest_flat, hn, xr0)


def _expert_kernel(be_ref, nb_ref, x_ref, wgu_ref, bgu_ref, wd_ref, bd_ref, y_ref, wgu_b, wd_b, *, ff):
    i = pl.program_id(0)

    @pl.when(i < nb_ref[0])
    def _():
        prev = be_ref[jnp.maximum(i - 1, 0)]

        @pl.when((i == 0) | (be_ref[i] != prev))
        def _():
            wgu_b[...] = wgu_ref[0].astype(BF16)
            wd_b[...] = wd_ref[0].astype(BF16)

        gu = _dot(x_ref[...].astype(BF16), wgu_b[...]) + bgu_ref[0]
        gate = jnp.minimum(gu[:, :ff], SWIGLU_LIMIT)
        up = jnp.clip(gu[:, ff:], -SWIGLU_LIMIT, SWIGLU_LIMIT)
        act = (up + 1.0) * (gate * jax.nn.sigmoid(SWIGLU_ALPHA * gate))
        y_ref[...] = _dot(act.astype(BF16), wd_b[...]) + bd_ref[0]

    @pl.when(i >= nb_ref[0])
    def _():
        y_ref[...] = jnp.zeros_like(y_ref)


def experts(xr, block_e, n_used, w_gu, b_gu, w_down, b_down, bm):
    n_rows, d = xr.shape
    ne, _, ff2 = w_gu.shape
    ff = ff2 // 2
    grid_spec = pltpu.PrefetchScalarGridSpec(
        num_scalar_prefetch=2,
        grid=(n_rows // bm,),
        in_specs=[pl.BlockSpec((bm, d), lambda i, be, nb: (i, 0)),
                  pl.BlockSpec((1, d, ff2), lambda i, be, nb: (be[i], 0, 0)),
                  pl.BlockSpec((1, 1, ff2), lambda i, be, nb: (be[i], 0, 0)),
                  pl.BlockSpec((1, ff, d), lambda i, be, nb: (be[i], 0, 0)),
                  pl.BlockSpec((1, 1, d), lambda i, be, nb: (be[i], 0, 0))],
        out_specs=pl.BlockSpec((bm, d), lambda i, be, nb: (i, 0)),
        scratch_shapes=[pltpu.VMEM((d, ff2), BF16), pltpu.VMEM((ff, d), BF16)],
    )
    return pl.pallas_call(
        functools.partial(_expert_kernel, ff=ff),
        grid_spec=grid_spec,
        out_shape=jax.ShapeDtypeStruct((n_rows, d), F32),
        compiler_params=_cparams(("arbitrary",)),
        name="moe_experts",
    )(block_e, n_used, xr, w_gu, b_gu.reshape(ne, 1, ff2), w_down, b_down.reshape(ne, 1, d))


def _combine_kernel(dest_ref, dest_next_ref, yr_hbm, g_ref, x1_ref, gate2_ref, gf_ref, out_ref, buf, sems,
                    *, tile, final_norm):
    i = pl.program_id(0)
    slot = i % 2
    n_copy = tile * TOP_K

    def gather(idx_ref, s):
        def start(t, c):
            for k in range(TOP_K):
                pltpu.make_async_copy(yr_hbm.at[pl.ds(idx_ref[t * TOP_K + k], 1)], buf.at[s, k, pl.ds(t, 1)],
                                      sems.at[s]).start()
            return c

        lax.fori_loop(0, tile, start, 0)

    @pl.when(i == 0)
    def _():
        gather(dest_ref, slot)

    @pl.when(i + 1 < pl.num_programs(0))
    def _():
        gather(dest_next_ref, 1 - slot)

    _wait_rows(yr_hbm, n_copy, sems.at[slot])

    y = None
    for k in range(TOP_K):
        term = g_ref[:, k:k + 1] * buf[slot, k]
        y = term if y is None else y + term
    x2 = x1_ref[...] + gate2_ref[...] * y
    if final_norm:
        ms = jnp.mean(x2 * x2, axis=-1, keepdims=True)
        x2 = x2 * lax.rsqrt(ms + RMS_EPS) * gf_ref[...]
    out_ref[...] = x2


def combine(yr, dest_flat, topg, x1, gate2, gf, tile, final_norm):
    n, d = x1.shape
    tok = pl.BlockSpec((tile, d), lambda i: (i, 0))
    steps = n // tile
    return pl.pallas_call(
        functools.partial(_combine_kernel, tile=tile, final_norm=final_norm),
        grid=(steps,),
        in_specs=[pl.BlockSpec((tile * TOP_K,), lambda i: (i,), memory_space=pltpu.SMEM),
                  pl.BlockSpec((tile * TOP_K,), lambda i: (jnp.minimum(i + 1, steps - 1),), memory_space=pltpu.SMEM),
                  pl.BlockSpec(memory_space=pl.ANY),
                  pl.BlockSpec((tile, topg.shape[1]), lambda i: (i, 0)),
                  tok, _row_spec(gate2.shape[0], tile, d), pl.BlockSpec((1, d), lambda i: (0, 0))],
        out_specs=tok,
        out_shape=jax.ShapeDtypeStruct(x1.shape, F32),
        scratch_shapes=[pltpu.VMEM((2, TOP_K, tile, d), F32), pltpu.SemaphoreType.DMA((2,))],
        compiler_params=_cparams(("arbitrary",)),
        name="moe_combine",
    )(dest_flat, dest_flat, yr, topg, x1, gate2, gf)


def moe_block(groups, norm_f_g, final_norm, w_gu, b_gu, w_down, b_down, bm, tile):
    d = groups[0][0].shape[1]
    ne = w_gu.shape[0]
    experts_iota = jnp.arange(ne, dtype=jnp.int32)
    counts = [grp[5][0, :ne].astype(jnp.int32) for grp in groups]
    total = sum(counts)
    padded = ((total + bm - 1) // bm) * bm
    pend = jnp.cumsum(padded)
    base = pend - padded
    dests = []
    for grp, cnt in zip(groups, counts):
        idx = grp[2][:, :TOP_K].astype(jnp.int32)
        onehot = (idx[..., None] == experts_iota).astype(jnp.int32)
        dests.append((jnp.sum(onehot * base, axis=-1) + grp[4][:, :TOP_K].astype(jnp.int32)).reshape(-1))
        base = base + cnt
    n_assign = sum(grp[0].shape[0] for grp in groups) * TOP_K
    n_rows = -(-n_assign // bm) * bm + ne * bm
    n_blocks = n_rows // bm
    starts = jnp.arange(n_blocks, dtype=jnp.int32) * bm
    block_e = jnp.minimum(jnp.sum((starts[:, None] >= pend[None, :]).astype(jnp.int32), axis=1), ne - 1)
    n_used = (pend[-1:] // bm).astype(jnp.int32)
    last_e = block_e[jnp.maximum(n_used[0] - 1, 0)]
    block_e = jnp.where(jnp.arange(n_blocks) < n_used[0], block_e, last_e).astype(jnp.int32)

    xr = jnp.zeros((n_rows, d), F32)
    for grp, dest in zip(groups, dests):
        xr = dispatch(grp[1], dest, xr, tile)
    yr = experts(xr, block_e, n_used, w_gu, b_gu, w_down, b_down, bm)
    return [combine(yr, dest, grp[3], grp[0], grp[6], norm_f_g.reshape(1, d), tile, final_norm)
            for grp, dest in zip(groups, dests)]


def _group_forward(x, mod, attend, h0_lanes, t_per_seq, lw, mats, rows):
    n, d = x.shape
    rows = min(rows, n)
    (norm1_g, norm2_g, w_in_b, w_out_b, w_glu_b, b_glu, d_vec, wr_hi, wr_lo, br) = lw
    kmat, smat, rmat, decay_re, decay_im = mats
    shift1, scale1, gate1, shift2, scale2, gate2 = [mod[:, j * d:(j + 1) * d] for j in range(6)]
    aw = (w_in_b.shape[1] - w_glu_b.shape[0]) // 3
    q_dtype = BF16 if h0_lanes is None else F32
    q, k, v, kb, vb, u = inproj(x, shift1, scale1, norm1_g, w_in_b, aw, q_dtype, rows)
    o_attn = attend(q, k, v, kb, vb)

    sw = u.shape[1]
    nb = sw // LANES
    u2 = u.reshape(n // SSM_CHUNK, SSM_CHUNK * sw)
    nc = u2.shape[0]
    crow = min(256, nc)
    if h0_lanes is None:
        s = ssm_state(u2, smat, crow)
        hs, hend = ssm_scan(s, decay_re, decay_im, crow, nb)
    else:
        assert t_per_seq == SSM_CHUNK
        hs = h0_lanes
        hend = ssm_state(u2, smat, crow, h0_lanes, decay_re, decay_im)
    y_ssm = ssm_out(u2, hs, kmat, rmat, d_vec, w_glu_b, b_glu, crow).reshape(n, sw)

    routed = postmix(x, o_attn, y_ssm, w_out_b, gate1, norm2_g, scale2, shift2, wr_hi, wr_lo, br, rows)
    return tuple(routed) + (gate2,), k, v, hend


def kernel(x_prompt, x_sample, c_prompt, c_sample, cache_k, cache_v, state_ssm_re, state_ssm_im, page_table, norm1_g, norm2_g, w_ada, b_ada, w_in, w_out, sb_bias, lam_re, lam_im, log_dt, ssm_b_re, ssm_b_im, ssm_c_re, ssm_c_im, ssm_d, w_glu, b_glu, w_router, b_router, w_gu, b_gu, w_down, b_down, norm_f_g):
    depth = w_in.shape[0]
    bp, tp, d = x_prompt.shape
    bs, ts, _ = x_sample.shape
    assert bp == 1, "the prompt group is handled as one long sequence"
    n_heads = cache_k.shape[3]
    aw = n_heads * HEAD_DIM
    g, p = lam_re.shape[1:]
    nb = g // GROUP_BLOCK
    ne = w_router.shape[-1]
    n_pool, page = cache_k.shape[1:3]
    rows = 512

    xp = x_prompt.reshape(bp * tp, d)
    xs = x_sample.reshape(bs * ts, d)
    n_c = bp + bs
    c_all = jnp.concatenate([c_prompt, c_sample, jnp.zeros((-n_c % SUBLANES, d), F32)], axis=0)

    outs = {name: [] for name in ("kp", "vp", "rp", "ip", "ks", "vs", "rs", "is")}
    for l in range(depth):
        mod = ada_mod(c_all, w_ada[l], b_ada[l])
        mod_p = mod[:bp]
        mod_s = jnp.repeat(mod[bp:n_c], ts, axis=0)
        mats = _ssm_mats(lam_re[l], lam_im[l], log_dt[l], ssm_b_re[l], ssm_b_im[l], ssm_c_re[l], ssm_c_im[l])
        wr = jnp.pad(w_router[l], ((0, 0), (0, ROUTER_PAD - ne)))
        wr_hi = wr.astype(BF16)
        wr_lo = (wr - wr_hi.astype(F32)).astype(BF16)
        br = jnp.concatenate([b_router[l], jnp.full((ROUTER_PAD - ne,), NEG_BIG, F32)]).reshape(1, ROUTER_PAD)
        lw = (norm1_g[l].reshape(1, d), norm2_g[l].reshape(1, d), w_in[l].astype(BF16), w_out[l].astype(BF16),
              w_glu[l].astype(BF16), b_glu[l].reshape(1, -1), ssm_d[l].reshape(1, -1), wr_hi, wr_lo, br)
        bias = sb_bias[l]

        def attend_p(q, k, v, kb, vb):
            return attn_prompt(q, kb, vb, bias, nsb=min(8, tp // KEY_BLOCK))

        ck = cache_k[l].transpose(0, 2, 3, 1).reshape(n_pool, aw, page)
        cv = cache_v[l].transpose(0, 2, 3, 1).reshape(n_pool, aw, page)

        def attend_s(q, k, v, kb, vb):
            return attn_sample(q, k, v, ck, cv, page_table, bias, ts, group=math.gcd(8, page_table.shape[1]))

        h0 = _state_to_lanes(state_ssm_re[l], state_ssm_im[l], nb)
        routed_p, kp, vp, hp = _group_forward(xp, mod_p, attend_p, None, tp, lw, mats, rows)
        routed_s, ks, vs, hs = _group_forward(xs, mod_s, attend_s, h0, ts, lw, mats, rows)
        xp, xs = moe_block([routed_p, routed_s], norm_f_g, l == depth - 1, w_gu[l], b_gu[l], w_down[l], b_down[l],
                           bm=EXPERT_ROWS, tile=256)
        rp, ip = _lanes_to_state(hp, g, p)
        rs, is_ = _lanes_to_state(hs, g, p)
        outs["kp"].append(kp.reshape(bp, tp, n_heads, HEAD_DIM))
        outs["vp"].append(vp.reshape(bp, tp, n_heads, HEAD_DIM))
        outs["rp"].append(rp)
        outs["ip"].append(ip)
        outs["ks"].append(ks.reshape(bs, ts, n_heads, HEAD_DIM))
        outs["vs"].append(vs.reshape(bs, ts, n_heads, HEAD_DIM))
        outs["rs"].append(rs)
        outs["is"].append(is_)
    st = lambda name: jnp.stack(outs[name])
    return (xp.reshape(bp, tp, d), xs.reshape(bs, ts, d), st("kp"), st("vp"), st("rp"), st("ip"),
            st("ks"), st("vs"), st("rs"), st("is"))
```

```python
import functools
import math

import jax
import jax.numpy as jnp
from jax import lax
from jax.experimental import pallas as pl
from jax.experimental.pallas import tpu as pltpu

F32 = jnp.float32
BF16 = jnp.bfloat16

HEAD_DIM = 64
SSM_GROUP = 16
TOP_K = 4
SWIGLU_LIMIT = 7.0
SWIGLU_ALPHA = 1.702
RMS_EPS = 1e-6

LANES = 128
SUBLANES = 8
VMEM_LIMIT_BYTES = 56 * 1024 * 1024

KEY_BLOCK = LANES
SSM_CHUNK = 8
GROUP_BLOCK = LANES // SSM_GROUP
ROUTER_PAD = LANES
EXPERT_ROWS = 512
NEG_BIG = -1e30
EXP_CAP = 1e30
LOG2E = math.log2(math.e)
PAGE_SLOTS = 3

def _cparams(semantics):
    return pltpu.CompilerParams(dimension_semantics=semantics, vmem_limit_bytes=VMEM_LIMIT_BYTES)


def _dot(a, b):
    return jnp.dot(a, b, preferred_element_type=F32)


def _dot_nt(a, b):
    return lax.dot_general(a, b, (((1,), (1,)), ((), ())), preferred_element_type=F32)


def _split_bf16(x):
    hi = x.astype(BF16)
    lo = (x - hi.astype(F32)).astype(BF16)
    return hi, lo


def _ada_kernel(c_ref, w_ref, b_ref, o_ref):
    o_ref[...] = _dot(c_ref[...].astype(BF16), w_ref[...].astype(BF16)) + b_ref[...]


def ada_mod(c, w_ada, b_ada):
    n, d = c.shape
    n_out = w_ada.shape[1]
    return pl.pallas_call(
        _ada_kernel,
        grid=(n_out // d,),
        in_specs=[pl.BlockSpec((n, d), lambda j: (0, 0)),
                  pl.BlockSpec((d, d), lambda j: (0, j)),
                  pl.BlockSpec((1, d), lambda j: (0, j))],
        out_specs=pl.BlockSpec((n, d), lambda j: (0, j)),
        out_shape=jax.ShapeDtypeStruct((n, n_out), F32),
        compiler_params=_cparams(("arbitrary",)),
        name="ada_mod",
    )(c, w_ada, b_ada.reshape(1, n_out))


def _inproj_kernel(x_ref, shift_ref, scale_ref, g_ref, w_ref, q_ref, k_ref, v_ref, kb_ref, vb_ref, u_ref, *, aw):
    x = x_ref[...]
    ms = jnp.mean(x * x, axis=-1, keepdims=True)
    hn = x * lax.rsqrt(ms + RMS_EPS) * g_ref[...]
    hn = hn * (1.0 + scale_ref[...]) + shift_ref[...]
    proj = _dot(hn.astype(BF16), w_ref[...])
    q_ref[...] = (proj[:, :aw] * (HEAD_DIM ** -0.5 * LOG2E)).astype(q_ref.dtype)
    k = proj[:, aw:2 * aw]
    v = proj[:, 2 * aw:3 * aw]
    k_ref[...] = k
    v_ref[...] = v
    kb_ref[...] = k.astype(BF16)
    vb_ref[...] = v.astype(BF16)
    u_ref[...] = proj[:, 3 * aw:]


def _row_spec(n_mod_rows, rows, d):
    if n_mod_rows == 1:
        return pl.BlockSpec((1, d), lambda i: (0, 0))
    return pl.BlockSpec((rows, d), lambda i: (i, 0))


def inproj(x, shift, scale, g, w_in_b, aw, q_dtype, rows):
    n, d = x.shape
    pw = w_in_b.shape[1]
    sw = pw - 3 * aw
    tok = lambda width: pl.BlockSpec((rows, width), lambda i: (i, 0))
    return pl.pallas_call(
        functools.partial(_inproj_kernel, aw=aw),
        grid=(n // rows,),
        in_specs=[tok(d), _row_spec(shift.shape[0], rows, d), _row_spec(scale.shape[0], rows, d),
                  pl.BlockSpec((1, d), lambda i: (0, 0)),
                  pl.BlockSpec((d, pw), lambda i: (0, 0))],
        out_specs=[tok(aw), tok(aw), tok(aw), tok(aw), tok(aw), tok(sw)],
        out_shape=[jax.ShapeDtypeStruct((n, aw), q_dtype),
                   jax.ShapeDtypeStruct((n, aw), F32), jax.ShapeDtypeStruct((n, aw), F32),
                   jax.ShapeDtypeStruct((n, aw), BF16), jax.ShapeDtypeStruct((n, aw), BF16),
                   jax.ShapeDtypeStruct((n, sw), F32)],
        compiler_params=_cparams(("arbitrary",)),
        name="inproj",
    )(x, shift, scale, g, w_in_b)


def _tri_ones():
    j = lax.broadcasted_iota(jnp.int32, (KEY_BLOCK, 2 * KEY_BLOCK), 0)
    s = lax.broadcasted_iota(jnp.int32, (KEY_BLOCK, 2 * KEY_BLOCK), 1)
    return jnp.where((j > s) | (s >= KEY_BLOCK), 1.0, 0.0).astype(BF16)


def _sb_logits(z, mask):
    drop = jnp.maximum(jnp.log(1.0 + jnp.minimum(jnp.exp2(z), EXP_CAP)) * LOG2E, z)
    if mask is not None:
        drop = jnp.where(mask, drop, 0.0)
    return z - drop, drop.astype(BF16)


def _weights_from_log2(x):
    return jnp.exp2(x).astype(BF16)


def _sb_finish(log_beta, drop_b, carry, tri, mask):
    groups = drop_b.shape[1] // KEY_BLOCK
    later, total = [], []
    for g in range(groups):
        cs = _dot(drop_b[:, g * KEY_BLOCK:(g + 1) * KEY_BLOCK], tri)
        later.append(cs[:, :KEY_BLOCK])
        total.append(cs[:, KEY_BLOCK:])
    later = jnp.concatenate(later, axis=1) if groups > 1 else later[0]
    total = jnp.concatenate(total, axis=1) if groups > 1 else total[0]
    a = _weights_from_log2(log_beta - (later + carry))
    if mask is not None:
        a = jnp.where(mask, a, jnp.zeros_like(a))
    return a, total


def _sb_weights(z, carry, tri, mask):
    if mask is not None:
        mask = jnp.concatenate([mask] * (z.shape[1] // KEY_BLOCK), axis=1)
    log_beta, drop_b = _sb_logits(z, mask)
    return _sb_finish(log_beta, drop_b, carry, tri, mask)


def _tri_later(width):
    j = lax.broadcasted_iota(jnp.int32, (width, width), 0)
    s = lax.broadcasted_iota(jnp.int32, (width, width), 1)
    return jnp.where(j > s, 1.0, 0.0).astype(BF16)


def _attn_pairs_kernel(kbias_ref, q_ref, k_ref, v_ref, tri_ref, o_ref, acc_ref, carry_ref, lb_scr, drop_scr,
                       qext_scr, *, nsb):
    it = pl.program_id(1)
    unit = 2 * KEY_BLOCK
    tq = nsb * KEY_BLOCK
    n_units = nsb // 2
    tri = tri_ref[...]
    ones_lanes = lax.broadcasted_iota(jnp.int32, (tq, LANES), 1) < 2
    qext_scr[:, :LANES] = q_ref[...]
    qext_scr[:, LANES:] = jnp.where(ones_lanes, 1.0, 0.0).astype(BF16)
    kbias = kbias_ref[0]
    acc_ref[...] = jnp.zeros_like(acc_ref)
    carry_ref[...] = jnp.zeros_like(carry_ref)
    lane = lax.broadcasted_iota(jnp.int32, (unit, LANES), 1)
    first_head = lane < HEAD_DIM

    def stacked(ref, u):
        st = pl.multiple_of(u * unit, unit)
        slab = ref[pl.ds(st, unit), :]
        zero = jnp.zeros_like(slab)
        return jnp.concatenate([jnp.where(first_head, slab, zero), jnp.where(first_head, zero, slab)], axis=0)

    def logits(r0, u, mask):
        z = _dot_nt(qext_scr[r0:, :], jnp.concatenate([stacked(k_ref, u), kbias], axis=1))
        return _sb_logits(z, mask)

    def finish(r0, u, log_beta, drop_b, mask):
        carry = carry_ref[r0:, :]
        m = log_beta.shape[0]
        parts, new_carry = [], []
        for h in range(2):
            d_h = drop_b[:, h * unit:(h + 1) * unit]
            later = _dot(d_h, tri)
            c_h = carry[:, h * KEY_BLOCK:(h + 1) * KEY_BLOCK]
            parts.append(later + jnp.concatenate([c_h, c_h], axis=1))
            total = later[:, 0:1] + d_h[:, 0:1].astype(F32)
            new_carry.append(c_h + jnp.broadcast_to(total, (m, KEY_BLOCK)))
        a = _weights_from_log2(log_beta - jnp.concatenate(parts, axis=1))
        if mask is not None:
            a = jnp.where(mask, a, jnp.zeros_like(a))
        acc_ref[r0:, :] += _dot(a, stacked(v_ref, u))
        carry_ref[r0:, :] = jnp.concatenate(new_carry, axis=1)

    for c in reversed(range(n_units)):
        r0 = c * unit
        rr = lax.broadcasted_iota(jnp.int32, (tq - r0, unit), 0)
        ll = lax.broadcasted_iota(jnp.int32, (tq - r0, unit), 1)
        mask = jnp.concatenate([ll < rr] * 2, axis=1)
        log_beta, drop_b = logits(r0, it * n_units + c, mask)
        finish(r0, it * n_units + c, log_beta, drop_b, mask)

    n_full = it * n_units

    def first_half(u, slot):
        log_beta, drop_b = logits(0, u, None)
        lb_scr[slot] = log_beta
        drop_scr[slot] = drop_b

    def second_half(u, slot):
        finish(0, u, lb_scr[slot], drop_scr[slot], None)

    @pl.when(n_full > 0)
    def _():
        per_trip = math.gcd(n_units, 4)
        first_half(n_full - 1, 0)

        def body(jj, c):
            u0 = n_full - 1 - per_trip * jj
            for i in range(per_trip):
                first_half(u0 - i - 1, (i + 1) % 2)
                second_half(u0 - i, i % 2)
            return c

        lax.fori_loop(0, n_full // per_trip - 1, body, 0)
        for i in range(per_trip - 1):
            first_half(per_trip - 2 - i, (i + 1) % 2)
            second_half(per_trip - 1 - i, i % 2)
        second_half(0, (per_trip - 1) % 2)

    o_ref[...] = acc_ref[...].astype(o_ref.dtype)


def attn_prompt_pairs(q, kb, vb, bias, nsb):
    n, aw = q.shape
    tq = nsb * KEY_BLOCK
    unit = 2 * KEY_BLOCK
    n_pairs = aw // LANES
    assert nsb % 4 == 0 and n % tq == 0
    b_hi, b_lo = _split_bf16(bias.astype(F32) * LOG2E)
    lane = jnp.arange(LANES)
    kbias = jnp.where(lane == 0, b_hi[:, None, None], jnp.where(lane == 1, b_lo[:, None, None], 0)).astype(BF16)
    kbias = jnp.broadcast_to(kbias, (bias.shape[0], unit, LANES)).reshape(n_pairs, 2 * unit, LANES)
    tri = _tri_later(unit)
    return pl.pallas_call(
        functools.partial(_attn_pairs_kernel, nsb=nsb),
        grid=(n_pairs, n // tq),
        in_specs=[pl.BlockSpec((1, 2 * unit, LANES), lambda hp, i: (hp, 0, 0)),
                  pl.BlockSpec((tq, LANES), lambda hp, i: (i, hp)),
                  pl.BlockSpec((n, LANES), lambda hp, i: (0, hp)),
                  pl.BlockSpec((n, LANES), lambda hp, i: (0, hp)),
                  pl.BlockSpec(tri.shape, lambda hp, i: (0, 0))],
        out_specs=pl.BlockSpec((tq, LANES), lambda hp, i: (i, hp)),
        out_shape=jax.ShapeDtypeStruct((n, aw), BF16),
        scratch_shapes=[pltpu.VMEM((tq, LANES), F32), pltpu.VMEM((tq, 2 * KEY_BLOCK), F32),
                        pltpu.VMEM((2, tq, 2 * unit), F32), pltpu.VMEM((2, tq, 2 * unit), BF16),
                        pltpu.VMEM((tq, 2 * LANES), BF16)],
        compiler_params=_cparams(("arbitrary", "arbitrary")),
        name="attn_prompt",
    )(kbias, q, kb, vb, tri)


def _attn_sample_kernel(pt_ref, q_ref, kn_ref, vn_ref, ck_hbm, cv_hbm, tri_ref, bias_ref, o_ref,
                        kbuf, vbuf, sems, qbd_ref, acc_ref, carry_ref, *, group, n_groups, n_heads, t_new):
    b = pl.program_id(0)
    total = pl.num_programs(0) * n_groups
    m = n_heads * t_new
    aw = n_heads * HEAD_DIM
    tri = tri_ref[...]
    bias = bias_ref[...]

    def request(s):
        slot = s % PAGE_SLOTS
        for i in range(group):
            page = pt_ref[s * group + i]
            pltpu.make_async_copy(ck_hbm.at[page], kbuf.at[slot, i], sems.at[slot, 0]).start()
            pltpu.make_async_copy(cv_hbm.at[page], vbuf.at[slot, i], sems.at[slot, 1]).start()

    def wait_group(slot):
        pltpu.make_async_copy(kbuf.at[slot], kbuf.at[slot], sems.at[slot, 0]).wait()
        pltpu.make_async_copy(vbuf.at[slot], vbuf.at[slot], sems.at[slot, 1]).wait()

    @pl.when(b == 0)
    def _():
        for s in range(PAGE_SLOTS - 1):
            request(s)

    q = q_ref[...].astype(BF16)
    qt = jnp.concatenate([q] * n_heads, axis=0)
    rw = lax.broadcasted_iota(jnp.int32, (m, aw), 0)
    ln = lax.broadcasted_iota(jnp.int32, (m, aw), 1)
    qbd_ref[...] = jnp.where(ln // HEAD_DIM == rw // t_new, qt, jnp.zeros_like(qt))
    pad = jnp.zeros((KEY_BLOCK - t_new, aw), F32)
    kn = jnp.concatenate([kn_ref[...], pad], axis=0).astype(BF16)
    vn = jnp.concatenate([vn_ref[...], pad], axis=0).astype(BF16)
    r2 = lax.broadcasted_iota(jnp.int32, (m, KEY_BLOCK), 0)
    l2 = lax.broadcasted_iota(jnp.int32, (m, KEY_BLOCK), 1)
    z = _dot_nt(qbd_ref[...], kn) + bias
    a, tot = _sb_weights(z, 0.0, tri, l2 < (r2 % t_new))
    acc_ref[...] = _dot(a, vn)
    carry_ref[...] = tot

    def body(g, c):
        s = b * n_groups + g

        @pl.when(s + PAGE_SLOTS - 1 < total)
        def _():
            request(s + PAGE_SLOTS - 1)

        slot = s % PAGE_SLOTS
        wait_group(slot)
        kt = jnp.concatenate([kbuf[slot, i].astype(BF16) for i in range(group)], axis=1)
        vt = jnp.concatenate([vbuf[slot, i].astype(BF16) for i in range(group)], axis=1)
        z = _dot(qbd_ref[...], kt) + jnp.concatenate([bias] * group, axis=1)
        log_beta, drop_b = _sb_logits(z, None)
        carry = carry_ref[...]
        shift = []
        for i in range(group):
            cs = _dot(drop_b[:, i * KEY_BLOCK:(i + 1) * KEY_BLOCK], tri)
            shift.append(cs[:, :KEY_BLOCK] + carry)
            carry = carry + cs[:, KEY_BLOCK:]
        a = _weights_from_log2(log_beta - jnp.concatenate(shift, axis=1))
        acc_ref[...] += _dot_nt(a, vt)
        carry_ref[...] = carry
        return c

    lax.fori_loop(0, n_groups, body, 0)

    acc = acc_ref[...]
    lo = lax.broadcasted_iota(jnp.int32, (t_new, aw), 1)
    o = jnp.zeros((t_new, aw), F32)
    for h in range(n_heads):
        o = o + jnp.where(lo // HEAD_DIM == h, acc[h * t_new:(h + 1) * t_new, :], 0.0)
    o_ref[...] = o.astype(o_ref.dtype)


def attn_sample(q, k_new, v_new, cache_k, cache_v, page_table, bias, t_new, group):
    n, aw = q.shape
    bsz, n_pages = page_table.shape
    n_heads = aw // HEAD_DIM
    page = cache_k.shape[2]
    n_groups = n_pages // group
    assert page == KEY_BLOCK and t_new == SUBLANES and n_pages % group == 0 and bsz * n_groups >= PAGE_SLOTS
    m = n_heads * t_new
    bias_rows = jnp.broadcast_to(jnp.repeat(bias * LOG2E, t_new)[:, None], (m, KEY_BLOCK)).astype(F32)
    pages_recent_first = page_table[:, ::-1].reshape(-1)
    tok = pl.BlockSpec((t_new, aw), lambda b, pt: (b, 0))
    ring = pltpu.VMEM((PAGE_SLOTS, group, aw, page), F32)
    grid_spec = pltpu.PrefetchScalarGridSpec(
        num_scalar_prefetch=1,
        grid=(bsz,),
        in_specs=[tok, tok, tok, pl.BlockSpec(memory_space=pl.ANY), pl.BlockSpec(memory_space=pl.ANY),
                  pl.BlockSpec((KEY_BLOCK, 2 * KEY_BLOCK), lambda b, pt: (0, 0)),
                  pl.BlockSpec((m, KEY_BLOCK), lambda b, pt: (0, 0))],
        out_specs=tok,
        scratch_shapes=[ring, ring, pltpu.SemaphoreType.DMA((PAGE_SLOTS, 2)),
                        pltpu.VMEM((m, aw), BF16), pltpu.VMEM((m, aw), F32), pltpu.VMEM((m, KEY_BLOCK), F32)],
    )
    return pl.pallas_call(
        functools.partial(_attn_sample_kernel, group=group, n_groups=n_groups, n_heads=n_heads, t_new=t_new),
        grid_spec=grid_spec,
        out_shape=jax.ShapeDtypeStruct((n, aw), BF16),
        compiler_params=_cparams(("arbitrary",)),
        name="attn_sample",
    )(pages_recent_first, q, k_new, v_new, cache_k, cache_v, _tri_ones(), bias_rows)


def _ssm_mats(lam_re, lam_im, log_dt, b_re, b_im, c_re, c_im):
    g, p = lam_re.shape
    hh = b_re.shape[-1]
    el = SSM_CHUNK
    nb = g // GROUP_BLOCK
    dt = jnp.exp(log_dt)[:, None]
    ar, ai = lam_re * dt, lam_im * dt
    lbr, lbi = jnp.exp(ar) * jnp.cos(ai), jnp.exp(ar) * jnp.sin(ai)
    den = lam_re * lam_re + lam_im * lam_im
    fr = ((lbr - 1.0) * lam_re + lbi * lam_im) / den
    fi = (lbi * lam_re - (lbr - 1.0) * lam_im) / den
    bbr = fr[..., None] * b_re - fi[..., None] * b_im
    bbi = fr[..., None] * b_im + fi[..., None] * b_re
    n = jnp.arange(el + 1, dtype=F32)[:, None, None]
    pr = jnp.exp(ar[None] * n) * jnp.cos(ai[None] * n)
    pi = jnp.exp(ar[None] * n) * jnp.sin(ai[None] * n)
    eye = jnp.eye(GROUP_BLOCK, dtype=F32)

    cpr = c_re[None] * pr[:el, :, None, :] - c_im[None] * pi[:el, :, None, :]
    cpi = c_re[None] * pi[:el, :, None, :] + c_im[None] * pr[:el, :, None, :]
    kd = jnp.einsum('dgip,gpj->dgij', cpr, bbr) - jnp.einsum('dgip,gpj->dgij', cpi, bbi)
    kmat = jnp.einsum('dkgij,gh->dkgjhi', kd.reshape(el, nb, GROUP_BLOCK, hh, hh), eye)
    kmat = kmat.reshape(el, nb, LANES, LANES)

    rev = pr[el - 1 - jnp.arange(el)], pi[el - 1 - jnp.arange(el)]
    scr = rev[0][..., None] * bbr[None] - rev[1][..., None] * bbi[None]
    sci = rev[0][..., None] * bbi[None] + rev[1][..., None] * bbr[None]

    def blk_s(a):
        a = jnp.einsum('skgpj,gh->skgjhp', a.reshape(el, nb, GROUP_BLOCK, p, hh), eye)
        return a.reshape(el, nb, LANES, GROUP_BLOCK * p)

    smat = jnp.concatenate([blk_s(scr), blk_s(sci)], axis=-1)

    c1r = c_re[None] * pr[1:, :, None, :] - c_im[None] * pi[1:, :, None, :]
    c1i = c_re[None] * pi[1:, :, None, :] + c_im[None] * pr[1:, :, None, :]

    def blk_r(a):
        a = jnp.einsum('tkgip,gh->tkgphi', a.reshape(el, nb, GROUP_BLOCK, hh, p), eye)
        return a.reshape(el, nb, GROUP_BLOCK * p, LANES)

    rmat = jnp.concatenate([blk_r(c1r), -blk_r(c1i)], axis=-2)

    decay_re = pr[el].reshape(1, g * p)
    decay_im = pi[el].reshape(1, g * p)
    return kmat.astype(BF16), smat.astype(BF16), rmat.astype(BF16), decay_re, decay_im


def _state_to_lanes(h_re, h_im, nb):
    b = h_re.shape[0]
    return jnp.stack([h_re.reshape(b, nb, -1), h_im.reshape(b, nb, -1)], axis=2).reshape(b, -1)


def _lanes_to_state(h, g, p):
    b = h.shape[0]
    nb = g // GROUP_BLOCK
    h = h.reshape(b, nb, 2, GROUP_BLOCK, p)
    return h[:, :, 0].reshape(b, g, p), h[:, :, 1].reshape(b, g, p)


def _ssm_state_kernel(*refs, has_h0, nb, sw, half):
    if has_h0:
        u_ref, smat_ref, h0_ref, dre_ref, dim_ref, s_ref = refs
    else:
        u_ref, smat_ref, s_ref = refs
    for k in range(nb):
        acc = None
        for s in range(SSM_CHUNK):
            ub = u_ref[:, s * sw + k * LANES: s * sw + (k + 1) * LANES].astype(BF16)
            d = _dot(ub, smat_ref[s, k])
            acc = d if acc is None else acc + d
        base = 2 * half * k
        if has_h0:
            hr = h0_ref[:, base:base + half]
            hi = h0_ref[:, base + half:base + 2 * half]
            dr = dre_ref[:, k * half:(k + 1) * half]
            di = dim_ref[:, k * half:(k + 1) * half]
            s_ref[:, base:base + half] = acc[:, :half] + dr * hr - di * hi
            s_ref[:, base + half:base + 2 * half] = acc[:, half:] + dr * hi + di * hr
        else:
            s_ref[:, base:base + 2 * half] = acc


def ssm_state(u2, smat, rows, h0=None, decay_re=None, decay_im=None):
    nc, width = u2.shape
    sw = width // SSM_CHUNK
    nb = sw // LANES
    half = smat.shape[-1] // 2
    sl = nb * 2 * half
    has_h0 = h0 is not None
    row = lambda w: pl.BlockSpec((rows, w), lambda i: (i, 0))
    in_specs = [row(width), pl.BlockSpec(smat.shape, lambda i: (0, 0, 0, 0))]
    args = [u2, smat]
    if has_h0:
        in_specs += [row(sl), pl.BlockSpec((1, nb * half), lambda i: (0, 0)),
                     pl.BlockSpec((1, nb * half), lambda i: (0, 0))]
        args += [h0, decay_re, decay_im]
    return pl.pallas_call(
        functools.partial(_ssm_state_kernel, has_h0=has_h0, nb=nb, sw=sw, half=half),
        grid=(nc // rows,),
        in_specs=in_specs,
        out_specs=row(sl),
        out_shape=jax.ShapeDtypeStruct((nc, sl), F32),
        compiler_params=_cparams(("arbitrary",)),
        name="ssm_state",
    )(*args)


def _ssm_scan_kernel(s_ref, dre_ref, dim_ref, hs_ref, hend_ref, h_scr, *, nb, half):
    @pl.when(pl.program_id(0) == 0)
    def _():
        h_scr[...] = jnp.zeros_like(h_scr)

    dr = dre_ref[...]
    di = dim_ref[...]

    def body(r, h):
        hs_ref[pl.ds(r, 1), :] = h
        s = s_ref[pl.ds(r, 1), :]
        parts = []
        for k in range(nb):
            base = 2 * half * k
            hr, hi = h[:, base:base + half], h[:, base + half:base + 2 * half]
            ar, ai = dr[:, k * half:(k + 1) * half], di[:, k * half:(k + 1) * half]
            parts.append(ar * hr - ai * hi + s[:, base:base + half])
            parts.append(ar * hi + ai * hr + s[:, base + half:base + 2 * half])
        return jnp.concatenate(parts, axis=1)

    h = lax.fori_loop(0, s_ref.shape[0], body, h_scr[...])
    h_scr[...] = h
    hend_ref[...] = h


def ssm_scan(s, decay_re, decay_im, rows, nb):
    nc, sl = s.shape
    n_half_total = decay_re.shape[1]
    half = n_half_total // nb
    row = pl.BlockSpec((rows, sl), lambda i: (i, 0))
    vec = pl.BlockSpec((1, n_half_total), lambda i: (0, 0))
    return pl.pallas_call(
        functools.partial(_ssm_scan_kernel, nb=nb, half=half),
        grid=(nc // rows,),
        in_specs=[row, vec, vec],
        out_specs=[row, pl.BlockSpec((1, sl), lambda i: (0, 0))],
        out_shape=[jax.ShapeDtypeStruct((nc, sl), F32), jax.ShapeDtypeStruct((1, sl), F32)],
        scratch_shapes=[pltpu.VMEM((1, sl), F32)],
        compiler_params=_cparams(("arbitrary",)),
        name="ssm_scan",
    )(s, decay_re, decay_im)


def _gelu_exact(y):
    return 0.5 * y * (1.0 + lax.erf(y * (0.5 ** 0.5)))


def _ssm_out_kernel(u_ref, hs_ref, kmat_ref, rmat_ref, d_ref, wglu_ref, bglu_ref, y_ref, *, nb, sw, half):
    hb = [hs_ref[:, 2 * half * k:2 * half * (k + 1)].astype(BF16) for k in range(nb)]
    ub = [[u_ref[:, s * sw + k * LANES: s * sw + (k + 1) * LANES].astype(BF16) for k in range(nb)]
          for s in range(SSM_CHUNK)]
    wglu = wglu_ref[...]
    for t in range(SSM_CHUNK):
        cols = []
        for k in range(nb):
            acc = _dot(hb[k], rmat_ref[t, k])
            for s in range(t + 1):
                acc = acc + _dot(ub[s][k], kmat_ref[t - s, k])
            cols.append(acc)
        y = jnp.concatenate(cols, axis=1) + d_ref[...] * u_ref[:, t * sw:(t + 1) * sw]
        g = _gelu_exact(y)
        gate = jax.nn.sigmoid(_dot(g.astype(BF16), wglu) + bglu_ref[...])
        y_ref[:, t * sw:(t + 1) * sw] = (g * gate).astype(y_ref.dtype)


def ssm_out(u2, hs, kmat, rmat, d_vec, w_glu_b, b_glu, rows):
    nc, width = u2.shape
    sw = width // SSM_CHUNK
    nb = sw // LANES
    sl = hs.shape[1]
    half = sl // (2 * nb)
    row = lambda w: pl.BlockSpec((rows, w), lambda i: (i, 0))
    const = lambda a: pl.BlockSpec(a.shape, lambda i: (0,) * a.ndim)
    return pl.pallas_call(
        functools.partial(_ssm_out_kernel, nb=nb, sw=sw, half=half),
        grid=(nc // rows,),
        in_specs=[row(width), row(sl), const(kmat), const(rmat), const(d_vec), const(w_glu_b), const(b_glu)],
        out_specs=row(width),
        out_shape=jax.ShapeDtypeStruct((nc, width), BF16),
        compiler_params=_cparams(("arbitrary",)),
        name="ssm_out",
    )(u2, hs, kmat, rmat, d_vec, w_glu_b, b_glu)


def _postmix_kernel(x_ref, o_ref, y_ref, wo_ref, gate1_ref, g2_ref, scale2_ref, shift2_ref, wrh_ref, wrl_ref,
                    br_ref, tril_ref, x1_ref, hn2_ref, topi_ref, topg_ref, topr_ref, cnt_ref, carry_ref, *, aw):
    @pl.when(pl.program_id(0) == 0)
    def _():
        carry_ref[...] = jnp.zeros_like(carry_ref)

    mix = _dot(o_ref[...], wo_ref[:aw, :]) + _dot(y_ref[...], wo_ref[aw:, :])
    x1 = x_ref[...] + gate1_ref[...] * mix
    x1_ref[...] = x1
    ms = jnp.mean(x1 * x1, axis=-1, keepdims=True)
    hn2 = x1 * lax.rsqrt(ms + RMS_EPS) * g2_ref[...]
    hn2 = hn2 * (1.0 + scale2_ref[...]) + shift2_ref[...]
    hn2_ref[...] = hn2

    hh, hl = _split_bf16(hn2)
    wrh = wrh_ref[...]
    logits = _dot(hh, wrh) + _dot(hl, wrh) + _dot(hh, wrl_ref[...]) + br_ref[...]

    rows = logits.shape[0]
    lane = lax.broadcasted_iota(jnp.int32, (rows, ROUTER_PAD), 1)
    work = logits
    sel = jnp.zeros((rows, ROUTER_PAD), F32)
    picks, vals, idxs = [], [], []
    for _ in range(TOP_K):
        m = jnp.max(work, axis=-1, keepdims=True)
        idx = jnp.min(jnp.where(work == m, lane, ROUTER_PAD), axis=-1, keepdims=True)
        pick = lane == idx
        picks.append(pick)
        vals.append(m)
        idxs.append(idx)
        sel = jnp.where(pick, 1.0, sel)
        work = jnp.where(pick, -jnp.inf, work)

    rank = _dot(tril_ref[...], sel.astype(BF16)) + carry_ref[...]
    carry_ref[...] = rank[rows - 1:rows, :] + sel[rows - 1:rows, :]
    cnt_ref[...] = carry_ref[...]

    es = [jnp.exp(v - vals[0]) for v in vals]
    den = es[0]
    for e in es[1:]:
        den = den + e
    topi = jnp.zeros((rows, ROUTER_PAD), F32)
    topg = jnp.zeros((rows, ROUTER_PAD), F32)
    topr = jnp.zeros((rows, ROUTER_PAD), F32)
    for r in range(TOP_K):
        rk = jnp.sum(jnp.where(picks[r], rank, 0.0), axis=-1, keepdims=True)
        topi = jnp.where(lane == r, idxs[r].astype(F32), topi)
        topg = jnp.where(lane == r, es[r] / den, topg)
        topr = jnp.where(lane == r, rk, topr)
    topi_ref[...] = topi
    topg_ref[...] = topg
    topr_ref[...] = topr


def postmix(x, o_attn, y_ssm, w_out_b, gate1, g2, scale2, shift2, wr_hi, wr_lo, br, rows):
    n, d = x.shape
    aw = o_attn.shape[1]
    tok = lambda width: pl.BlockSpec((rows, width), lambda i: (i, 0))
    const = lambda a: pl.BlockSpec(a.shape, lambda i: (0,) * a.ndim)
    r = lax.broadcasted_iota(jnp.int32, (rows, rows), 0)
    c = lax.broadcasted_iota(jnp.int32, (rows, rows), 1)
    tril = jnp.where(c < r, 1.0, 0.0).astype(BF16)
    lanes_out = jax.ShapeDtypeStruct((n, ROUTER_PAD), F32)
    return pl.pallas_call(
        functools.partial(_postmix_kernel, aw=aw),
        grid=(n // rows,),
        in_specs=[tok(d), tok(aw), tok(y_ssm.shape[1]), const(w_out_b),
                  _row_spec(gate1.shape[0], rows, d), const(g2),
                  _row_spec(scale2.shape[0], rows, d), _row_spec(shift2.shape[0], rows, d),
                  const(wr_hi), const(wr_lo), const(br), const(tril)],
        out_specs=[tok(d), tok(d), tok(ROUTER_PAD), tok(ROUTER_PAD), tok(ROUTER_PAD),
                   pl.BlockSpec((1, ROUTER_PAD), lambda i: (0, 0))],
        out_shape=[jax.ShapeDtypeStruct((n, d), F32), jax.ShapeDtypeStruct((n, d), F32),
                   lanes_out, lanes_out, lanes_out, jax.ShapeDtypeStruct((1, ROUTER_PAD), F32)],
        scratch_shapes=[pltpu.VMEM((1, ROUTER_PAD), F32)],
        compiler_params=_cparams(("arbitrary",)),
        name="postmix",
    )(x, o_attn, y_ssm, w_out_b, gate1, g2, scale2, shift2, wr_hi, wr_lo, br, tril)


def _wait_rows(ref, n_copy, sem):
    span = ref.at[pl.ds(0, n_copy)]
    pltpu.make_async_copy(span, span, sem).wait()


def _dispatch_kernel(dest_ref, hn_ref, xr_in, xr_hbm, stage, sems, *, tile):
    del xr_in
    i = pl.program_id(0)
    slot = i % 2
    n_copy = tile * TOP_K
    stage[slot] = hn_ref[...]

    def start(t, c):
        for k in range(TOP_K):
            pltpu.make_async_copy(stage.at[slot, pl.ds(t, 1)], xr_hbm.at[pl.ds(dest_ref[t * TOP_K + k], 1)],
                                  sems.at[slot]).start()
        return c

    lax.fori_loop(0, tile, start, 0)

    @pl.when(i > 0)
    def _():
        _wait_rows(xr_hbm, n_copy, sems.at[1 - slot])

    @pl.when(i == pl.num_programs(0) - 1)
    def _():
        _wait_rows(xr_hbm, n_copy, sems.at[slot])


def dispatch(hn, dest_flat, xr0, tile):
    n, d = hn.shape
    return pl.pallas_call(
        functools.partial(_dispatch_kernel, tile=tile),
        grid=(n // tile,),
        in_specs=[pl.BlockSpec((tile * TOP_K,), lambda i: (i,), memory_space=pltpu.SMEM),
                  pl.BlockSpec((tile, d), lambda i: (i, 0)),
                  pl.BlockSpec(memory_space=pl.ANY)],
        out_specs=pl.BlockSpec(memory_space=pl.ANY),
        out_shape=jax.ShapeDtypeStruct(xr0.shape, xr0.dtype),
        scratch_shapes=[pltpu.VMEM((2, tile, d), hn.dtype), pltpu.SemaphoreType.DMA((2,))],
        input_output_aliases={2: 0},
        compiler_params=_cparams(("arbitrary",)),
        name="moe_dispatch",
    )(dest_flat, hn, xr0)


def _expert_kernel(be_ref, nb_ref, x_ref, wgu_ref, bgu_ref, wd_ref, bd_ref, y_ref, wgu_b, wd_b, *, ff):
    i = pl.program_id(0)

    @pl.when(i < nb_ref[0])
    def _():
        prev = be_ref[jnp.maximum(i - 1, 0)]

        @pl.when((i == 0) | (be_ref[i] != prev))
        def _():
            wgu_b[...] = wgu_ref[0].astype(BF16)
            wd_b[...] = wd_ref[0].astype(BF16)

        gu = _dot(x_ref[...].astype(BF16), wgu_b[...]) + bgu_ref[0]
        gate = jnp.minimum(gu[:, :ff], SWIGLU_LIMIT)
        up = jnp.clip(gu[:, ff:], -SWIGLU_LIMIT, SWIGLU_LIMIT)
        act = (up + 1.0) * (gate * jax.nn.sigmoid(SWIGLU_ALPHA * gate))
        y_ref[...] = _dot(act.astype(BF16), wd_b[...]) + bd_ref[0]

    @pl.when(i >= nb_ref[0])
    def _():
        y_ref[...] = jnp.zeros_like(y_ref)


def experts(xr, block_e, n_used, w_gu, b_gu, w_down, b_down, bm):
    n_rows, d = xr.shape
    ne, _, ff2 = w_gu.shape
    ff = ff2 // 2
    grid_spec = pltpu.PrefetchScalarGridSpec(
        num_scalar_prefetch=2,
        grid=(n_rows // bm,),
        in_specs=[pl.BlockSpec((bm, d), lambda i, be, nb: (i, 0)),
                  pl.BlockSpec((1, d, ff2), lambda i, be, nb: (be[i], 0, 0)),
                  pl.BlockSpec((1, 1, ff2), lambda i, be, nb: (be[i], 0, 0)),
                  pl.BlockSpec((1, ff, d), lambda i, be, nb: (be[i], 0, 0)),
                  pl.BlockSpec((1, 1, d), lambda i, be, nb: (be[i], 0, 0))],
        out_specs=pl.BlockSpec((bm, d), lambda i, be, nb: (i, 0)),
        scratch_shapes=[pltpu.VMEM((d, ff2), BF16), pltpu.VMEM((ff, d), BF16)],
    )
    return pl.pallas_call(
        functools.partial(_expert_kernel, ff=ff),
        grid_spec=grid_spec,
        out_shape=jax.ShapeDtypeStruct((n_rows, d), F32),
        compiler_params=_cparams(("arbitrary",)),
        name="moe_experts",
    )(block_e, n_used, xr, w_gu, b_gu.reshape(ne, 1, ff2), w_down, b_down.reshape(ne, 1, d))


def _combine_kernel(dest_ref, dest_next_ref, yr_hbm, g_ref, x1_ref, gate2_ref, gf_ref, out_ref, buf, sems,
                    *, tile, final_norm):
    i = pl.program_id(0)
    slot = i % 2
    n_copy = tile * TOP_K

    def gather(idx_ref, s):
        def start(t, c):
            for k in range(TOP_K):
                pltpu.make_async_copy(yr_hbm.at[pl.ds(idx_ref[t * TOP_K + k], 1)], buf.at[s, k, pl.ds(t, 1)],
                                      sems.at[s]).start()
            return c

        lax.fori_loop(0, tile, start, 0)

    @pl.when(i == 0)
    def _():
        gather(dest_ref, slot)

    @pl.when(i + 1 < pl.num_programs(0))
    def _():
        gather(dest_next_ref, 1 - slot)

    _wait_rows(yr_hbm, n_copy, sems.at[slot])

    y = None
    for k in range(TOP_K):
        term = g_ref[:, k:k + 1] * buf[slot, k]
        y = term if y is None else y + term
    x2 = x1_ref[...] + gate2_ref[...] * y
    if final_norm:
        ms = jnp.mean(x2 * x2, axis=-1, keepdims=True)
        x2 = x2 * lax.rsqrt(ms + RMS_EPS) * gf_ref[...]
    out_ref[...] = x2


def combine(yr, dest_flat, topg, x1, gate2, gf, tile, final_norm):
    n, d = x1.shape
    tok = pl.BlockSpec((tile, d), lambda i: (i, 0))
    steps = n // tile
    return pl.pallas_call(
        functools.partial(_combine_kernel, tile=tile, final_norm=final_norm),
        grid=(steps,),
        in_specs=[pl.BlockSpec((tile * TOP_K,), lambda i: (i,), memory_space=pltpu.SMEM),
                  pl.BlockSpec((tile * TOP_K,), lambda i: (jnp.minimum(i + 1, steps - 1),), memory_space=pltpu.SMEM),
                  pl.BlockSpec(memory_space=pl.ANY),
                  pl.BlockSpec((tile, topg.shape[1]), lambda i: (i, 0)),
                  tok, _row_spec(gate2.shape[0], tile, d), pl.BlockSpec((1, d), lambda i: (0, 0))],
        out_specs=tok,
        out_shape=jax.ShapeDtypeStruct(x1.shape, F32),
        scratch_shapes=[pltpu.VMEM((2, TOP_K, tile, d), F32), pltpu.SemaphoreType.DMA((2,))],
        compiler_params=_cparams(("arbitrary",)),
        name="moe_combine",
    )(dest_flat, dest_flat, yr, topg, x1, gate2, gf)


def moe_block(groups, norm_f_g, final_norm, w_gu, b_gu, w_down, b_down, bm, tile):
    d = groups[0][0].shape[1]
    ne = w_gu.shape[0]
    experts_iota = jnp.arange(ne, dtype=jnp.int32)
    counts = [grp[5][0, :ne].astype(jnp.int32) for grp in groups]
    total = sum(counts)
    padded = ((total + bm - 1) // bm) * bm
    pend = jnp.cumsum(padded)
    base = pend - padded
    dests = []
    for grp, cnt in zip(groups, counts):
        idx = grp[2][:, :TOP_K].astype(jnp.int32)
        onehot = (idx[..., None] == experts_iota).astype(jnp.int32)
        dests.append((jnp.sum(onehot * base, axis=-1) + grp[4][:, :TOP_K].astype(jnp.int32)).reshape(-1))
        base = base + cnt
    n_assign = sum(grp[0].shape[0] for grp in groups) * TOP_K
    n_rows = -(-n_assign // bm) * bm + ne * bm
    n_blocks = n_rows // bm
    starts = jnp.arange(n_blocks, dtype=jnp.int32) * bm
    block_e = jnp.minimum(jnp.sum((starts[:, None] >= pend[None, :]).astype(jnp.int32), axis=1), ne - 1)
    n_used = (pend[-1:] // bm).astype(jnp.int32)
    last_e = block_e[jnp.maximum(n_used[0] - 1, 0)]
    block_e = jnp.where(jnp.arange(n_blocks) < n_used[0], block_e, last_e).astype(jnp.int32)

    xr = jnp.zeros((n_rows, d), F32)
    for grp, dest in zip(groups, dests):
        xr = dispatch(grp[1], dest, xr, tile)
    yr = experts(xr, block_e, n_used, w_gu, b_gu, w_down, b_down, bm)
    return [combine(yr, dest, grp[3], grp[0], grp[6], norm_f_g.reshape(1, d), tile, final_norm)
            for grp, dest in zip(groups, dests)]


def _group_forward(x, mod, attend, h0_lanes, t_per_seq, lw, mats, rows):
    n, d = x.shape
    rows = min(rows, n)
    (norm1_g, norm2_g, w_in_b, w_out_b, w_glu_b, b_glu, d_vec, wr_hi, wr_lo, br) = lw
    kmat, smat, rmat, decay_re, decay_im = mats
    shift1, scale1, gate1, shift2, scale2, gate2 = [mod[:, j * d:(j + 1) * d] for j in range(6)]
    aw = (w_in_b.shape[1] - w_glu_b.shape[0]) // 3
    q_dtype = BF16 if h0_lanes is None else F32
    q, k, v, kb, vb, u = inproj(x, shift1, scale1, norm1_g, w_in_b, aw, q_dtype, rows)
    o_attn = attend(q, k, v, kb, vb)

    sw = u.shape[1]
    nb = sw // LANES
    u2 = u.reshape(n // SSM_CHUNK, SSM_CHUNK * sw)
    nc = u2.shape[0]
    crow = min(256, nc)
    if h0_lanes is None:
        s = ssm_state(u2, smat, crow)
        hs, hend = ssm_scan(s, decay_re, decay_im, crow, nb)
    else:
        assert t_per_seq == SSM_CHUNK
        hs = h0_lanes
        hend = ssm_state(u2, smat, crow, h0_lanes, decay_re, decay_im)
    y_ssm = ssm_out(u2, hs, kmat, rmat, d_vec, w_glu_b, b_glu, crow).reshape(n, sw)

    routed = postmix(x, o_attn, y_ssm, w_out_b, gate1, norm2_g, scale2, shift2, wr_hi, wr_lo, br, rows)
    return tuple(routed) + (gate2,), k, v, hend


def kernel(x_prompt, x_sample, c_prompt, c_sample, cache_k, cache_v, state_ssm_re, state_ssm_im, page_table, norm1_g, norm2_g, w_ada, b_ada, w_in, w_out, sb_bias, lam_re, lam_im, log_dt, ssm_b_re, ssm_b_im, ssm_c_re, ssm_c_im, ssm_d, w_glu, b_glu, w_router, b_router, w_gu, b_gu, w_down, b_down, norm_f_g):
    depth = w_in.shape[0]
    bp, tp, d = x_prompt.shape
    bs, ts, _ = x_sample.shape
    assert bp == 1, "the prompt group is handled as one long sequence"
    n_heads = cache_k.shape[3]
    aw = n_heads * HEAD_DIM
    g, p = lam_re.shape[1:]
    nb = g // GROUP_BLOCK
    ne = w_router.shape[-1]
    n_pool, page = cache_k.shape[1:3]
    rows = 512

    xp = x_prompt.reshape(bp * tp, d)
    xs = x_sample.reshape(bs * ts, d)
    n_c = bp + bs
    c_all = jnp.concatenate([c_prompt, c_sample, jnp.zeros((-n_c % SUBLANES, d), F32)], axis=0)

    outs = {name: [] for name in ("kp", "vp", "rp", "ip", "ks", "vs", "rs", "is")}
    for l in range(depth):
        mod = ada_mod(c_all, w_ada[l], b_ada[l])
        mod_p = mod[:bp]
        mod_s = jnp.repeat(mod[bp:n_c], ts, axis=0)
        mats = _ssm_mats(lam_re[l], lam_im[l], log_dt[l], ssm_b_re[l], ssm_b_im[l], ssm_c_re[l], ssm_c_im[l])
        wr = jnp.pad(w_router[l], ((0, 0), (0, ROUTER_PAD - ne)))
        wr_hi = wr.astype(BF16)
        wr_lo = (wr - wr_hi.astype(F32)).astype(BF16)
        br = jnp.concatenate([b_router[l], jnp.full((ROUTER_PAD - ne,), NEG_BIG, F32)]).reshape(1, ROUTER_PAD)
        lw = (norm1_g[l].reshape(1, d), norm2_g[l].reshape(1, d), w_in[l].astype(BF16), w_out[l].astype(BF16),
              w_glu[l].astype(BF16), b_glu[l].reshape(1, -1), ssm_d[l].reshape(1, -1), wr_hi, wr_lo, br)
        bias = sb_bias[l]

        def attend_p(q, k, v, kb, vb):
            return attn_prompt_pairs(q, kb, vb, bias, nsb=min(8, tp // KEY_BLOCK))

        ck = cache_k[l].transpose(0, 2, 3, 1).reshape(n_pool, aw, page)
        cv = cache_v[l].transpose(0, 2, 3, 1).reshape(n_pool, aw, page)

        def attend_s(q, k, v, kb, vb):
            return attn_sample(q, k, v, ck, cv, page_table, bias, ts, group=math.gcd(8, page_table.shape[1]))

        h0 = _state_to_lanes(state_ssm_re[l], state_ssm_im[l], nb)
        routed_p, kp, vp, hp = _group_forward(xp, mod_p, attend_p, None, tp, lw, mats, rows)
        routed_s, ks, vs, hs = _group_forward(xs, mod_s, attend_s, h0, ts, lw, mats, rows)
        xp, xs = moe_block([routed_p, routed_s], norm_f_g, l == depth - 1, w_gu[l], b_gu[l], w_down[l], b_down[l],
                           bm=EXPERT_ROWS, tile=256)
        rp, ip = _lanes_to_state(hp, g, p)
        rs, is_ = _lanes_to_state(hs, g, p)
        outs["kp"].append(kp.reshape(bp, tp, n_heads, HEAD_DIM))
        outs["vp"].append(vp.reshape(bp, tp, n_heads, HEAD_DIM))
        outs["rp"].append(rp)
        outs["ip"].append(ip)
        outs["ks"].append(ks.reshape(bs, ts, n_heads, HEAD_DIM))
        outs["vs"].append(vs.reshape(bs, ts, n_heads, HEAD_DIM))
        outs["rs"].append(rs)
        outs["is"].append(is_)
    st = lambda name: jnp.stack(outs[name])
    return (xp.reshape(bp, tp, d), xs.reshape(bs, ts, d), st("kp"), st("vp"), st("rp"), st("ip"),
            st("ks"), st("vs"), st("rs"), st("is"))
```

```python
import functools
import math

import jax
import jax.numpy as jnp
from jax import lax
from jax.experimental import pallas as pl
from jax.experimental.pallas import tpu as pltpu

F32 = jnp.float32
BF16 = jnp.bfloat16

HEAD_DIM = 64
SSM_GROUP = 16
TOP_K = 4
SWIGLU_LIMIT = 7.0
SWIGLU_ALPHA = 1.702
RMS_EPS = 1e-6

LANES = 128
SUBLANES = 8
VMEM_LIMIT_BYTES = 56 * 1024 * 1024

KEY_BLOCK = LANES
SSM_CHUNK = 8
GROUP_BLOCK = LANES // SSM_GROUP
ROUTER_PAD = LANES
EXPERT_ROWS = 512
NEG_BIG = -1e30
EXP_CAP = 1e30
LOG2E = math.log2(math.e)
PAGE_SLOTS = 3

def _cparams(semantics):
    return pltpu.CompilerParams(dimension_semantics=semantics, vmem_limit_bytes=VMEM_LIMIT_BYTES)


def _dot(a, b):
    return jnp.dot(a, b, preferred_element_type=F32)


def _dot_nt(a, b):
    return lax.dot_general(a, b, (((1,), (1,)), ((), ())), preferred_element_type=F32)


def _split_bf16(x):
    hi = x.astype(BF16)
    lo = (x - hi.astype(F32)).astype(BF16)
    return hi, lo


def _ada_kernel(c_ref, w_ref, b_ref, o_ref):
    o_ref[...] = _dot(c_ref[...].astype(BF16), w_ref[...].astype(BF16)) + b_ref[...]


def ada_mod(c, w_ada, b_ada):
    n, d = c.shape
    n_out = w_ada.shape[1]
    return pl.pallas_call(
        _ada_kernel,
        grid=(n_out // d,),
        in_specs=[pl.BlockSpec((n, d), lambda j: (0, 0)),
                  pl.BlockSpec((d, d), lambda j: (0, j)),
                  pl.BlockSpec((1, d), lambda j: (0, j))],
        out_specs=pl.BlockSpec((n, d), lambda j: (0, j)),
        out_shape=jax.ShapeDtypeStruct((n, n_out), F32),
        compiler_params=_cparams(("arbitrary",)),
        name="ada_mod",
    )(c, w_ada, b_ada.reshape(1, n_out))


def _inproj_kernel(x_ref, shift_ref, scale_ref, g_ref, w_ref, q_ref, k_ref, v_ref, kb_ref, vb_ref, u_ref, *, aw):
    x = x_ref[...]
    ms = jnp.mean(x * x, axis=-1, keepdims=True)
    hn = x * lax.rsqrt(ms + RMS_EPS) * g_ref[...]
    hn = hn * (1.0 + scale_ref[...]) + shift_ref[...]
    proj = _dot(hn.astype(BF16), w_ref[...])
    q_ref[...] = (proj[:, :aw] * (HEAD_DIM ** -0.5 * LOG2E)).astype(q_ref.dtype)
    k = proj[:, aw:2 * aw]
    v = proj[:, 2 * aw:3 * aw]
    k_ref[...] = k
    v_ref[...] = v
    kb_ref[...] = k.astype(BF16)
    vb_ref[...] = v.astype(BF16)
    u_ref[...] = proj[:, 3 * aw:]


def _row_spec(n_mod_rows, rows, d):
    if n_mod_rows == 1:
        return pl.BlockSpec((1, d), lambda i: (0, 0))
    return pl.BlockSpec((rows, d), lambda i: (i, 0))


def inproj(x, shift, scale, g, w_in_b, aw, q_dtype, rows):
    n, d = x.shape
    pw = w_in_b.shape[1]
    sw = pw - 3 * aw
    tok = lambda width: pl.BlockSpec((rows, width), lambda i: (i, 0))
    return pl.pallas_call(
        functools.partial(_inproj_kernel, aw=aw),
        grid=(n // rows,),
        in_specs=[tok(d), _row_spec(shift.shape[0], rows, d), _row_spec(scale.shape[0], rows, d),
                  pl.BlockSpec((1, d), lambda i: (0, 0)),
                  pl.BlockSpec((d, pw), lambda i: (0, 0))],
        out_specs=[tok(aw), tok(aw), tok(aw), tok(aw), tok(aw), tok(sw)],
        out_shape=[jax.ShapeDtypeStruct((n, aw), q_dtype),
                   jax.ShapeDtypeStruct((n, aw), F32), jax.ShapeDtypeStruct((n, aw), F32),
                   jax.ShapeDtypeStruct((n, aw), BF16), jax.ShapeDtypeStruct((n, aw), BF16),
                   jax.ShapeDtypeStruct((n, sw), F32)],
        compiler_params=_cparams(("arbitrary",)),
        name="inproj",
    )(x, shift, scale, g, w_in_b)


def _tri_ones():
    j = lax.broadcasted_iota(jnp.int32, (KEY_BLOCK, 2 * KEY_BLOCK), 0)
    s = lax.broadcasted_iota(jnp.int32, (KEY_BLOCK, 2 * KEY_BLOCK), 1)
    return jnp.where((j > s) | (s >= KEY_BLOCK), 1.0, 0.0).astype(BF16)


def _sb_drop(z, mask):
    drop = jnp.maximum(jnp.log(1.0 + jnp.minimum(jnp.exp2(z), EXP_CAP)) * LOG2E, z)
    if mask is not None:
        drop = jnp.where(mask, drop, 0.0)
    return drop


def _sb_logits(z, mask):
    drop = _sb_drop(z, mask)
    return z - drop, drop.astype(BF16)


def _weights_from_log2(x):
    return jnp.exp2(x).astype(BF16)


def _sb_finish(log_beta, drop_b, carry, tri, mask):
    groups = drop_b.shape[1] // KEY_BLOCK
    later, total = [], []
    for g in range(groups):
        cs = _dot(drop_b[:, g * KEY_BLOCK:(g + 1) * KEY_BLOCK], tri)
        later.append(cs[:, :KEY_BLOCK])
        total.append(cs[:, KEY_BLOCK:])
    later = jnp.concatenate(later, axis=1) if groups > 1 else later[0]
    total = jnp.concatenate(total, axis=1) if groups > 1 else total[0]
    a = _weights_from_log2(log_beta - (later + carry))
    if mask is not None:
        a = jnp.where(mask, a, jnp.zeros_like(a))
    return a, total


def _sb_weights(z, carry, tri, mask):
    if mask is not None:
        mask = jnp.concatenate([mask] * (z.shape[1] // KEY_BLOCK), axis=1)
    log_beta, drop_b = _sb_logits(z, mask)
    return _sb_finish(log_beta, drop_b, carry, tri, mask)


def _tri_suffix(width):
    j = lax.broadcasted_iota(jnp.int32, (width, width), 0)
    s = lax.broadcasted_iota(jnp.int32, (width, width), 1)
    return jnp.where(j >= s, 1.0, 0.0).astype(BF16)


def _attn_pairs_kernel(kbias_ref, q_ref, k_ref, v_ref, tri_ref, o_ref, acc_ref, carry_ref, lb_scr, drop_scr,
                       qext_scr, *, nsb):
    it = pl.program_id(1)
    unit = 2 * KEY_BLOCK
    tq = nsb * KEY_BLOCK
    n_units = nsb // 2
    tri = tri_ref[...]
    ones_lanes = lax.broadcasted_iota(jnp.int32, (tq, LANES), 1) < 2
    qext_scr[:, :LANES] = q_ref[...]
    qext_scr[:, LANES:] = jnp.where(ones_lanes, 1.0, 0.0).astype(BF16)
    kbias = kbias_ref[0]
    acc_ref[...] = jnp.zeros_like(acc_ref)
    carry_ref[...] = jnp.zeros_like(carry_ref)
    lane = lax.broadcasted_iota(jnp.int32, (unit, LANES), 1)
    first_head = lane < HEAD_DIM

    def stacked(ref, u):
        st = pl.multiple_of(u * unit, unit)
        slab = ref[pl.ds(st, unit), :]
        zero = jnp.zeros_like(slab)
        return jnp.concatenate([jnp.where(first_head, slab, zero), jnp.where(first_head, zero, slab)], axis=0)

    def logits(r0, u, mask):
        z = _dot_nt(qext_scr[r0:, :], jnp.concatenate([stacked(k_ref, u), kbias], axis=1))
        return z, _sb_drop(z, mask).astype(BF16)

    def finish(r0, u, z, drop_b, mask):
        carry = carry_ref[r0:, :]
        m = z.shape[0]
        parts, new_carry = [], []
        for h in range(2):
            incl = _dot(drop_b[:, h * unit:(h + 1) * unit], tri)
            c_h = carry[:, h * KEY_BLOCK:(h + 1) * KEY_BLOCK]
            parts.append(incl + jnp.concatenate([c_h, c_h], axis=1))
            new_carry.append(c_h + jnp.broadcast_to(incl[:, 0:1], (m, KEY_BLOCK)))
        a = jnp.exp2((z - jnp.concatenate(parts, axis=1)).astype(BF16))
        if mask is not None:
            a = jnp.where(mask, a, jnp.zeros_like(a))
        acc_ref[r0:, :] += _dot(a, stacked(v_ref, u))
        carry_ref[r0:, :] = jnp.concatenate(new_carry, axis=1)

    for c in reversed(range(n_units)):
        r0 = c * unit
        rr = lax.broadcasted_iota(jnp.int32, (tq - r0, unit), 0)
        ll = lax.broadcasted_iota(jnp.int32, (tq - r0, unit), 1)
        mask = jnp.concatenate([ll < rr] * 2, axis=1)
        z, drop_b = logits(r0, it * n_units + c, mask)
        finish(r0, it * n_units + c, z, drop_b, mask)

    n_full = it * n_units

    def first_half(u, slot):
        z, drop_b = logits(0, u, None)
        lb_scr[slot] = z
        drop_scr[slot] = drop_b

    def second_half(u, slot):
        finish(0, u, lb_scr[slot], drop_scr[slot], None)

    @pl.when(n_full > 0)
    def _():
        per_trip = math.gcd(n_units, 4)
        first_half(n_full - 1, 0)

        def body(jj, c):
            u0 = n_full - 1 - per_trip * jj
            for i in range(per_trip):
                first_half(u0 - i - 1, (i + 1) % 2)
                second_half(u0 - i, i % 2)
            return c

        lax.fori_loop(0, n_full // per_trip - 1, body, 0)
        for i in range(per_trip - 1):
            first_half(per_trip - 2 - i, (i + 1) % 2)
            second_half(per_trip - 1 - i, i % 2)
        second_half(0, (per_trip - 1) % 2)

    o_ref[...] = acc_ref[...].astype(o_ref.dtype)


def attn_prompt_pairs(q, kb, vb, bias, nsb):
    n, aw = q.shape
    tq = nsb * KEY_BLOCK
    unit = 2 * KEY_BLOCK
    n_pairs = aw // LANES
    assert nsb % 4 == 0 and n % tq == 0
    b_hi, b_lo = _split_bf16(bias.astype(F32) * LOG2E)
    lane = jnp.arange(LANES)
    kbias = jnp.where(lane == 0, b_hi[:, None, None], jnp.where(lane == 1, b_lo[:, None, None], 0)).astype(BF16)
    kbias = jnp.broadcast_to(kbias, (bias.shape[0], unit, LANES)).reshape(n_pairs, 2 * unit, LANES)
    tri = _tri_suffix(unit)
    return pl.pallas_call(
        functools.partial(_attn_pairs_kernel, nsb=nsb),
        grid=(n_pairs, n // tq),
        in_specs=[pl.BlockSpec((1, 2 * unit, LANES), lambda hp, i: (hp, 0, 0)),
                  pl.BlockSpec((tq, LANES), lambda hp, i: (i, hp)),
                  pl.BlockSpec((n, LANES), lambda hp, i: (0, hp)),
                  pl.BlockSpec((n, LANES), lambda hp, i: (0, hp)),
                  pl.BlockSpec(tri.shape, lambda hp, i: (0, 0))],
        out_specs=pl.BlockSpec((tq, LANES), lambda hp, i: (i, hp)),
        out_shape=jax.ShapeDtypeStruct((n, aw), BF16),
        scratch_shapes=[pltpu.VMEM((tq, LANES), F32), pltpu.VMEM((tq, 2 * KEY_BLOCK), F32),
                        pltpu.VMEM((2, tq, 2 * unit), F32), pltpu.VMEM((2, tq, 2 * unit), BF16),
                        pltpu.VMEM((tq, 2 * LANES), BF16)],
        compiler_params=_cparams(("arbitrary", "arbitrary")),
        name="attn_prompt",
    )(kbias, q, kb, vb, tri)


def _attn_sample_kernel(pt_ref, q_ref, kn_ref, vn_ref, ck_hbm, cv_hbm, tri_ref, bias_ref, o_ref,
                        kbuf, vbuf, sems, qbd_ref, acc_ref, carry_ref, *, group, n_groups, n_heads, t_new):
    b = pl.program_id(0)
    total = pl.num_programs(0) * n_groups
    m = n_heads * t_new
    aw = n_heads * HEAD_DIM
    tri = tri_ref[...]
    bias = bias_ref[...]

    def request(s):
        slot = s % PAGE_SLOTS
        for i in range(group):
            page = pt_ref[s * group + i]
            pltpu.make_async_copy(ck_hbm.at[page], kbuf.at[slot, i], sems.at[slot, 0]).start()
            pltpu.make_async_copy(cv_hbm.at[page], vbuf.at[slot, i], sems.at[slot, 1]).start()

    def wait_group(slot):
        pltpu.make_async_copy(kbuf.at[slot], kbuf.at[slot], sems.at[slot, 0]).wait()
        pltpu.make_async_copy(vbuf.at[slot], vbuf.at[slot], sems.at[slot, 1]).wait()

    @pl.when(b == 0)
    def _():
        for s in range(PAGE_SLOTS - 1):
            request(s)

    q = q_ref[...].astype(BF16)
    qt = jnp.concatenate([q] * n_heads, axis=0)
    rw = lax.broadcasted_iota(jnp.int32, (m, aw), 0)
    ln = lax.broadcasted_iota(jnp.int32, (m, aw), 1)
    qbd_ref[...] = jnp.where(ln // HEAD_DIM == rw // t_new, qt, jnp.zeros_like(qt))
    pad = jnp.zeros((KEY_BLOCK - t_new, aw), F32)
    kn = jnp.concatenate([kn_ref[...], pad], axis=0).astype(BF16)
    vn = jnp.concatenate([vn_ref[...], pad], axis=0).astype(BF16)
    r2 = lax.broadcasted_iota(jnp.int32, (m, KEY_BLOCK), 0)
    l2 = lax.broadcasted_iota(jnp.int32, (m, KEY_BLOCK), 1)
    z = _dot_nt(qbd_ref[...], kn) + bias
    a, tot = _sb_weights(z, 0.0, tri, l2 < (r2 % t_new))
    acc_ref[...] = _dot(a, vn)
    carry_ref[...] = tot

    def body(g, c):
        s = b * n_groups + g

        @pl.when(s + PAGE_SLOTS - 1 < total)
        def _():
            request(s + PAGE_SLOTS - 1)

        slot = s % PAGE_SLOTS
        wait_group(slot)
        kt = jnp.concatenate([kbuf[slot, i].astype(BF16) for i in range(group)], axis=1)
        vt = jnp.concatenate([vbuf[slot, i].astype(BF16) for i in range(group)], axis=1)
        z = _dot(qbd_ref[...], kt) + jnp.concatenate([bias] * group, axis=1)
        log_beta, drop_b = _sb_logits(z, None)
        carry = carry_ref[...]
        shift = []
        for i in range(group):
            cs = _dot(drop_b[:, i * KEY_BLOCK:(i + 1) * KEY_BLOCK], tri)
            shift.append(cs[:, :KEY_BLOCK] + carry)
            carry = carry + cs[:, KEY_BLOCK:]
        a = _weights_from_log2(log_beta - jnp.concatenate(shift, axis=1))
        acc_ref[...] += _dot_nt(a, vt)
        carry_ref[...] = carry
        return c

    lax.fori_loop(0, n_groups, body, 0)

    acc = acc_ref[...]
    lo = lax.broadcasted_iota(jnp.int32, (t_new, aw), 1)
    o = jnp.zeros((t_new, aw), F32)
    for h in range(n_heads):
        o = o + jnp.where(lo // HEAD_DIM == h, acc[h * t_new:(h + 1) * t_new, :], 0.0)
    o_ref[...] = o.astype(o_ref.dtype)


def attn_sample(q, k_new, v_new, cache_k, cache_v, page_table, bias, t_new, group):
    n, aw = q.shape
    bsz, n_pages = page_table.shape
    n_heads = aw // HEAD_DIM
    page = cache_k.shape[2]
    n_groups = n_pages // group
    assert page == KEY_BLOCK and t_new == SUBLANES and n_pages % group == 0 and bsz * n_groups >= PAGE_SLOTS
    m = n_heads * t_new
    bias_rows = jnp.broadcast_to(jnp.repeat(bias * LOG2E, t_new)[:, None], (m, KEY_BLOCK)).astype(F32)
    pages_recent_first = page_table[:, ::-1].reshape(-1)
    tok = pl.BlockSpec((t_new, aw), lambda b, pt: (b, 0))
    ring = pltpu.VMEM((PAGE_SLOTS, group, aw, page), F32)
    grid_spec = pltpu.PrefetchScalarGridSpec(
        num_scalar_prefetch=1,
        grid=(bsz,),
        in_specs=[tok, tok, tok, pl.BlockSpec(memory_space=pl.ANY), pl.BlockSpec(memory_space=pl.ANY),
                  pl.BlockSpec((KEY_BLOCK, 2 * KEY_BLOCK), lambda b, pt: (0, 0)),
                  pl.BlockSpec((m, KEY_BLOCK), lambda b, pt: (0, 0))],
        out_specs=tok,
        scratch_shapes=[ring, ring, pltpu.SemaphoreType.DMA((PAGE_SLOTS, 2)),
                        pltpu.VMEM((m, aw), BF16), pltpu.VMEM((m, aw), F32), pltpu.VMEM((m, KEY_BLOCK), F32)],
    )
    return pl.pallas_call(
        functools.partial(_attn_sample_kernel, group=group, n_groups=n_groups, n_heads=n_heads, t_new=t_new),
        grid_spec=grid_spec,
        out_shape=jax.ShapeDtypeStruct((n, aw), BF16),
        compiler_params=_cparams(("arbitrary",)),
        name="attn_sample",
    )(pages_recent_first, q, k_new, v_new, cache_k, cache_v, _tri_ones(), bias_rows)


def _ssm_mats(lam_re, lam_im, log_dt, b_re, b_im, c_re, c_im):
    g, p = lam_re.shape
    hh = b_re.shape[-1]
    el = SSM_CHUNK
    nb = g // GROUP_BLOCK
    dt = jnp.exp(log_dt)[:, None]
    ar, ai = lam_re * dt, lam_im * dt
    lbr, lbi = jnp.exp(ar) * jnp.cos(ai), jnp.exp(ar) * jnp.sin(ai)
    den = lam_re * lam_re + lam_im * lam_im
    fr = ((lbr - 1.0) * lam_re + lbi * lam_im) / den
    fi = (lbi * lam_re - (lbr - 1.0) * lam_im) / den
    bbr = fr[..., None] * b_re - fi[..., None] * b_im
    bbi = fr[..., None] * b_im + fi[..., None] * b_re
    n = jnp.arange(el + 1, dtype=F32)[:, None, None]
    pr = jnp.exp(ar[None] * n) * jnp.cos(ai[None] * n)
    pi = jnp.exp(ar[None] * n) * jnp.sin(ai[None] * n)
    eye = jnp.eye(GROUP_BLOCK, dtype=F32)

    cpr = c_re[None] * pr[:el, :, None, :] - c_im[None] * pi[:el, :, None, :]
    cpi = c_re[None] * pi[:el, :, None, :] + c_im[None] * pr[:el, :, None, :]
    kd = jnp.einsum('dgip,gpj->dgij', cpr, bbr) - jnp.einsum('dgip,gpj->dgij', cpi, bbi)
    kmat = jnp.einsum('dkgij,gh->dkgjhi', kd.reshape(el, nb, GROUP_BLOCK, hh, hh), eye)
    kmat = kmat.reshape(el, nb, LANES, LANES)

    rev = pr[el - 1 - jnp.arange(el)], pi[el - 1 - jnp.arange(el)]
    scr = rev[0][..., None] * bbr[None] - rev[1][..., None] * bbi[None]
    sci = rev[0][..., None] * bbi[None] + rev[1][..., None] * bbr[None]

    def blk_s(a):
        a = jnp.einsum('skgpj,gh->skgjhp', a.reshape(el, nb, GROUP_BLOCK, p, hh), eye)
        return a.reshape(el, nb, LANES, GROUP_BLOCK * p)

    smat = jnp.concatenate([blk_s(scr), blk_s(sci)], axis=-1)

    c1r = c_re[None] * pr[1:, :, None, :] - c_im[None] * pi[1:, :, None, :]
    c1i = c_re[None] * pi[1:, :, None, :] + c_im[None] * pr[1:, :, None, :]

    def blk_r(a):
        a = jnp.einsum('tkgip,gh->tkgphi', a.reshape(el, nb, GROUP_BLOCK, hh, p), eye)
        return a.reshape(el, nb, GROUP_BLOCK * p, LANES)

    rmat = jnp.concatenate([blk_r(c1r), -blk_r(c1i)], axis=-2)

    decay_re = pr[el].reshape(1, g * p)
    decay_im = pi[el].reshape(1, g * p)
    return kmat.astype(BF16), smat.astype(BF16), rmat.astype(BF16), decay_re, decay_im


def _state_to_lanes(h_re, h_im, nb):
    b = h_re.shape[0]
    return jnp.stack([h_re.reshape(b, nb, -1), h_im.reshape(b, nb, -1)], axis=2).reshape(b, -1)


def _lanes_to_state(h, g, p):
    b = h.shape[0]
    nb = g // GROUP_BLOCK
    h = h.reshape(b, nb, 2, GROUP_BLOCK, p)
    return h[:, :, 0].reshape(b, g, p), h[:, :, 1].reshape(b, g, p)


def _ssm_state_kernel(*refs, has_h0, nb, sw, half):
    if has_h0:
        u_ref, smat_ref, h0_ref, dre_ref, dim_ref, s_ref = refs
    else:
        u_ref, smat_ref, s_ref = refs
    for k in range(nb):
        acc = None
        for s in range(SSM_CHUNK):
            ub = u_ref[:, s * sw + k * LANES: s * sw + (k + 1) * LANES].astype(BF16)
            d = _dot(ub, smat_ref[s, k])
            acc = d if acc is None else acc + d
        base = 2 * half * k
        if has_h0:
            hr = h0_ref[:, base:base + half]
            hi = h0_ref[:, base + half:base + 2 * half]
            dr = dre_ref[:, k * half:(k + 1) * half]
            di = dim_ref[:, k * half:(k + 1) * half]
            s_ref[:, base:base + half] = acc[:, :half] + dr * hr - di * hi
            s_ref[:, base + half:base + 2 * half] = acc[:, half:] + dr * hi + di * hr
        else:
            s_ref[:, base:base + 2 * half] = acc


def ssm_state(u2, smat, rows, h0=None, decay_re=None, decay_im=None):
    nc, width = u2.shape
    sw = width // SSM_CHUNK
    nb = sw // LANES
    half = smat.shape[-1] // 2
    sl = nb * 2 * half
    has_h0 = h0 is not None
    row = lambda w: pl.BlockSpec((rows, w), lambda i: (i, 0))
    in_specs = [row(width), pl.BlockSpec(smat.shape, lambda i: (0, 0, 0, 0))]
    args = [u2, smat]
    if has_h0:
        in_specs += [row(sl), pl.BlockSpec((1, nb * half), lambda i: (0, 0)),
                     pl.BlockSpec((1, nb * half), lambda i: (0, 0))]
        args += [h0, decay_re, decay_im]
    return pl.pallas_call(
        functools.partial(_ssm_state_kernel, has_h0=has_h0, nb=nb, sw=sw, half=half),
        grid=(nc // rows,),
        in_specs=in_specs,
        out_specs=row(sl),
        out_shape=jax.ShapeDtypeStruct((nc, sl), F32),
        compiler_params=_cparams(("arbitrary",)),
        name="ssm_state",
    )(*args)


def _ssm_scan_kernel(s_ref, dre_ref, dim_ref, hs_ref, hend_ref, h_scr, *, nb, half):
    @pl.when(pl.program_id(0) == 0)
    def _():
        h_scr[...] = jnp.zeros_like(h_scr)

    dr = dre_ref[...]
    di = dim_ref[...]

    def body(r, h):
        hs_ref[pl.ds(r, 1), :] = h
        s = s_ref[pl.ds(r, 1), :]
        parts = []
        for k in range(nb):
            base = 2 * half * k
            hr, hi = h[:, base:base + half], h[:, base + half:base + 2 * half]
            ar, ai = dr[:, k * half:(k + 1) * half], di[:, k * half:(k + 1) * half]
            parts.append(ar * hr - ai * hi + s[:, base:base + half])
            parts.append(ar * hi + ai * hr + s[:, base + half:base + 2 * half])
        return jnp.concatenate(parts, axis=1)

    h = lax.fori_loop(0, s_ref.shape[0], body, h_scr[...])
    h_scr[...] = h
    hend_ref[...] = h


def ssm_scan(s, decay_re, decay_im, rows, nb):
    nc, sl = s.shape
    n_half_total = decay_re.shape[1]
    half = n_half_total // nb
    row = pl.BlockSpec((rows, sl), lambda i: (i, 0))
    vec = pl.BlockSpec((1, n_half_total), lambda i: (0, 0))
    return pl.pallas_call(
        functools.partial(_ssm_scan_kernel, nb=nb, half=half),
        grid=(nc // rows,),
        in_specs=[row, vec, vec],
        out_specs=[row, pl.BlockSpec((1, sl), lambda i: (0, 0))],
        out_shape=[jax.ShapeDtypeStruct((nc, sl), F32), jax.ShapeDtypeStruct((1, sl), F32)],
        scratch_shapes=[pltpu.VMEM((1, sl), F32)],
        compiler_params=_cparams(("arbitrary",)),
        name="ssm_scan",
    )(s, decay_re, decay_im)


def _gelu_exact(y):
    return 0.5 * y * (1.0 + lax.erf(y * (0.5 ** 0.5)))


def _ssm_out_kernel(u_ref, hs_ref, kmat_ref, rmat_ref, d_ref, wglu_ref, bglu_ref, y_ref, *, nb, sw, half):
    hb = [hs_ref[:, 2 * half * k:2 * half * (k + 1)].astype(BF16) for k in range(nb)]
    ub = [[u_ref[:, s * sw + k * LANES: s * sw + (k + 1) * LANES].astype(BF16) for k in range(nb)]
          for s in range(SSM_CHUNK)]
    wglu = wglu_ref[...]
    for t in range(SSM_CHUNK):
        cols = []
        for k in range(nb):
            acc = _dot(hb[k], rmat_ref[t, k])
            for s in range(t + 1):
                acc = acc + _dot(ub[s][k], kmat_ref[t - s, k])
            cols.append(acc)
        y = jnp.concatenate(cols, axis=1) + d_ref[...] * u_ref[:, t * sw:(t + 1) * sw]
        g = _gelu_exact(y)
        gate = jax.nn.sigmoid(_dot(g.astype(BF16), wglu) + bglu_ref[...])
        y_ref[:, t * sw:(t + 1) * sw] = (g * gate).astype(y_ref.dtype)


def ssm_out(u2, hs, kmat, rmat, d_vec, w_glu_b, b_glu, rows):
    nc, width = u2.shape
    sw = width // SSM_CHUNK
    nb = sw // LANES
    sl = hs.shape[1]
    half = sl // (2 * nb)
    row = lambda w: pl.BlockSpec((rows, w), lambda i: (i, 0))
    const = lambda a: pl.BlockSpec(a.shape, lambda i: (0,) * a.ndim)
    return pl.pallas_call(
        functools.partial(_ssm_out_kernel, nb=nb, sw=sw, half=half),
        grid=(nc // rows,),
        in_specs=[row(width), row(sl), const(kmat), const(rmat), const(d_vec), const(w_glu_b), const(b_glu)],
        out_specs=row(width),
        out_shape=jax.ShapeDtypeStruct((nc, width), BF16),
        compiler_params=_cparams(("arbitrary",)),
        name="ssm_out",
    )(u2, hs, kmat, rmat, d_vec, w_glu_b, b_glu)


def _postmix_kernel(x_ref, o_ref, y_ref, wo_ref, gate1_ref, g2_ref, scale2_ref, shift2_ref, wrh_ref, wrl_ref,
                    br_ref, tril_ref, x1_ref, hn2_ref, topi_ref, topg_ref, topr_ref, cnt_ref, carry_ref, *, aw):
    @pl.when(pl.program_id(0) == 0)
    def _():
        carry_ref[...] = jnp.zeros_like(carry_ref)

    mix = _dot(o_ref[...], wo_ref[:aw, :]) + _dot(y_ref[...], wo_ref[aw:, :])
    x1 = x_ref[...] + gate1_ref[...] * mix
    x1_ref[...] = x1
    ms = jnp.mean(x1 * x1, axis=-1, keepdims=True)
    hn2 = x1 * lax.rsqrt(ms + RMS_EPS) * g2_ref[...]
    hn2 = hn2 * (1.0 + scale2_ref[...]) + shift2_ref[...]
    hn2_ref[...] = hn2

    hh, hl = _split_bf16(hn2)
    wrh = wrh_ref[...]
    logits = _dot(hh, wrh) + _dot(hl, wrh) + _dot(hh, wrl_ref[...]) + br_ref[...]

    rows = logits.shape[0]
    lane = lax.broadcasted_iota(jnp.int32, (rows, ROUTER_PAD), 1)
    work = logits
    sel = jnp.zeros((rows, ROUTER_PAD), F32)
    picks, vals, idxs = [], [], []
    for _ in range(TOP_K):
        m = jnp.max(work, axis=-1, keepdims=True)
        idx = jnp.min(jnp.where(work == m, lane, ROUTER_PAD), axis=-1, keepdims=True)
        pick = lane == idx
        picks.append(pick)
        vals.append(m)
        idxs.append(idx)
        sel = jnp.where(pick, 1.0, sel)
        work = jnp.where(pick, -jnp.inf, work)

    rank = _dot(tril_ref[...], sel.astype(BF16)) + carry_ref[...]
    carry_ref[...] = rank[rows - 1:rows, :] + sel[rows - 1:rows, :]
    cnt_ref[...] = carry_ref[...]

    es = [jnp.exp(v - vals[0]) for v in vals]
    den = es[0]
    for e in es[1:]:
        den = den + e
    topi = jnp.zeros((rows, ROUTER_PAD), F32)
    topg = jnp.zeros((rows, ROUTER_PAD), F32)
    topr = jnp.zeros((rows, ROUTER_PAD), F32)
    for r in range(TOP_K):
        rk = jnp.sum(jnp.where(picks[r], rank, 0.0), axis=-1, keepdims=True)
        topi = jnp.where(lane == r, idxs[r].astype(F32), topi)
        topg = jnp.where(lane == r, es[r] / den, topg)
        topr = jnp.where(lane == r, rk, topr)
    topi_ref[...] = topi
    topg_ref[...] = topg
    topr_ref[...] = topr


def postmix(x, o_attn, y_ssm, w_out_b, gate1, g2, scale2, shift2, wr_hi, wr_lo, br, rows):
    n, d = x.shape
    aw = o_attn.shape[1]
    tok = lambda width: pl.BlockSpec((rows, width), lambda i: (i, 0))
    const = lambda a: pl.BlockSpec(a.shape, lambda i: (0,) * a.ndim)
    r = lax.broadcasted_iota(jnp.int32, (rows, rows), 0)
    c = lax.broadcasted_iota(jnp.int32, (rows, rows), 1)
    tril = jnp.where(c < r, 1.0, 0.0).astype(BF16)
    lanes_out = jax.ShapeDtypeStruct((n, ROUTER_PAD), F32)
    return pl.pallas_call(
        functools.partial(_postmix_kernel, aw=aw),
        grid=(n // rows,),
        in_specs=[tok(d), tok(aw), tok(y_ssm.shape[1]), const(w_out_b),
                  _row_spec(gate1.shape[0], rows, d), const(g2),
                  _row_spec(scale2.shape[0], rows, d), _row_spec(shift2.shape[0], rows, d),
                  const(wr_hi), const(wr_lo), const(br), const(tril)],
        out_specs=[tok(d), tok(d), tok(ROUTER_PAD), tok(ROUTER_PAD), tok(ROUTER_PAD),
                   pl.BlockSpec((1, ROUTER_PAD), lambda i: (0, 0))],
        out_shape=[jax.ShapeDtypeStruct((n, d), F32), jax.ShapeDtypeStruct((n, d), F32),
                   lanes_out, lanes_out, lanes_out, jax.ShapeDtypeStruct((1, ROUTER_PAD), F32)],
        scratch_shapes=[pltpu.VMEM((1, ROUTER_PAD), F32)],
        compiler_params=_cparams(("arbitrary",)),
        name="postmix",
    )(x, o_attn, y_ssm, w_out_b, gate1, g2, scale2, shift2, wr_hi, wr_lo, br, tril)


def _wait_rows(ref, n_copy, sem):
    span = ref.at[pl.ds(0, n_copy)]
    pltpu.make_async_copy(span, span, sem).wait()


def _dispatch_kernel(dest_ref, hn_ref, xr_in, xr_hbm, stage, sems, *, tile):
    del xr_in
    i = pl.program_id(0)
    slot = i % 2
    n_copy = tile * TOP_K
    stage[slot] = hn_ref[...]

    def start(t, c):
        for k in range(TOP_K):
            pltpu.make_async_copy(stage.at[slot, pl.ds(t, 1)], xr_hbm.at[pl.ds(dest_ref[t * TOP_K + k], 1)],
                                  sems.at[slot]).start()
        return c

    lax.fori_loop(0, tile, start, 0)

    @pl.when(i > 0)
    def _():
        _wait_rows(xr_hbm, n_copy, sems.at[1 - slot])

    @pl.when(i == pl.num_programs(0) - 1)
    def _():
        _wait_rows(xr_hbm, n_copy, sems.at[slot])


def dispatch(hn, dest_flat, xr0, tile):
    n, d = hn.shape
    return pl.pallas_call(
        functools.partial(_dispatch_kernel, tile=tile),
        grid=(n // tile,),
        in_specs=[pl.BlockSpec((tile * TOP_K,), lambda i: (i,), memory_space=pltpu.SMEM),
                  pl.BlockSpec((tile, d), lambda i: (i, 0)),
                  pl.BlockSpec(memory_space=pl.ANY)],
        out_specs=pl.BlockSpec(memory_space=pl.ANY),
        out_shape=jax.ShapeDtypeStruct(xr0.shape, xr0.dtype),
        scratch_shapes=[pltpu.VMEM((2, tile, d), hn.dtype), pltpu.SemaphoreType.DMA((2,))],
        input_output_aliases={2: 0},
        compiler_params=_cparams(("arbitrary",)),
        name="moe_dispatch",
    )(dest_flat, hn, xr0)


def _expert_kernel(be_ref, nb_ref, x_ref, wgu_ref, bgu_ref, wd_ref, bd_ref, y_ref, wgu_b, wd_b, *, ff):
    i = pl.program_id(0)

    @pl.when(i < nb_ref[0])
    def _():
        prev = be_ref[jnp.maximum(i - 1, 0)]

        @pl.when((i == 0) | (be_ref[i] != prev))
        def _():
            wgu_b[...] = wgu_ref[0].astype(BF16)
            wd_b[...] = wd_ref[0].astype(BF16)

        gu = _dot(x_ref[...].astype(BF16), wgu_b[...]) + bgu_ref[0]
        gate = jnp.minimum(gu[:, :ff], SWIGLU_LIMIT)
        up = jnp.clip(gu[:, ff:], -SWIGLU_LIMIT, SWIGLU_LIMIT)
        act = (up + 1.0) * (gate * jax.nn.sigmoid(SWIGLU_ALPHA * gate))
        y_ref[...] = _dot(act.astype(BF16), wd_b[...]) + bd_ref[0]

    @pl.when(i >= nb_ref[0])
    def _():
        y_ref[...] = jnp.zeros_like(y_ref)


def experts(xr, block_e, n_used, w_gu, b_gu, w_down, b_down, bm):
    n_rows, d = xr.shape
    ne, _, ff2 = w_gu.shape
    ff = ff2 // 2
    grid_spec = pltpu.PrefetchScalarGridSpec(
        num_scalar_prefetch=2,
        grid=(n_rows // bm,),
        in_specs=[pl.BlockSpec((bm, d), lambda i, be, nb: (i, 0)),
                  pl.BlockSpec((1, d, ff2), lambda i, be, nb: (be[i], 0, 0)),
                  pl.BlockSpec((1, 1, ff2), lambda i, be, nb: (be[i], 0, 0)),
                  pl.BlockSpec((1, ff, d), lambda i, be, nb: (be[i], 0, 0)),
                  pl.BlockSpec((1, 1, d), lambda i, be, nb: (be[i], 0, 0))],
        out_specs=pl.BlockSpec((bm, d), lambda i, be, nb: (i, 0)),
        scratch_shapes=[pltpu.VMEM((d, ff2), BF16), pltpu.VMEM((ff, d), BF16)],
    )
    return pl.pallas_call(
        functools.partial(_expert_kernel, ff=ff),
        grid_spec=grid_spec,
        out_shape=jax.ShapeDtypeStruct((n_rows, d), F32),
        compiler_params=_cparams(("arbitrary",)),
        name="moe_experts",
    )(block_e, n_used, xr, w_gu, b_gu.reshape(ne, 1, ff2), w_down, b_down.reshape(ne, 1, d))


def _combine_kernel(dest_ref, dest_next_ref, yr_hbm, g_ref, x1_ref, gate2_ref, gf_ref, out_ref, buf, sems,
                    *, tile, final_norm):
    i = pl.program_id(0)
    slot = i % 2
    n_copy = tile * TOP_K

    def gather(idx_ref, s):
        def start(t, c):
            for k in range(TOP_K):
                pltpu.make_async_copy(yr_hbm.at[pl.ds(idx_ref[t * TOP_K + k], 1)], buf.at[s, k, pl.ds(t, 1)],
                                      sems.at[s]).start()
            return c

        lax.fori_loop(0, tile, start, 0)

    @pl.when(i == 0)
    def _():
        gather(dest_ref, slot)

    @pl.when(i + 1 < pl.num_programs(0))
    def _():
        gather(dest_next_ref, 1 - slot)

    _wait_rows(yr_hbm, n_copy, sems.at[slot])

    y = None
    for k in range(TOP_K):
        term = g_ref[:, k:k + 1] * buf[slot, k]
        y = term if y is None else y + term
    x2 = x1_ref[...] + gate2_ref[...] * y
    if final_norm:
        ms = jnp.mean(x2 * x2, axis=-1, keepdims=True)
        x2 = x2 * lax.rsqrt(ms + RMS_EPS) * gf_ref[...]
    out_ref[...] = x2


def combine(yr, dest_flat, topg, x1, gate2, gf, tile, final_norm):
    n, d = x1.shape
    tok = pl.BlockSpec((tile, d), lambda i: (i, 0))
    steps = n // tile
    return pl.pallas_call(
        functools.partial(_combine_kernel, tile=tile, final_norm=final_norm),
        grid=(steps,),
        in_specs=[pl.BlockSpec((tile * TOP_K,), lambda i: (i,), memory_space=pltpu.SMEM),
                  pl.BlockSpec((tile * TOP_K,), lambda i: (jnp.minimum(i + 1, steps - 1),), memory_space=pltpu.SMEM),
                  pl.BlockSpec(memory_space=pl.ANY),
                  pl.BlockSpec((tile, topg.shape[1]), lambda i: (i, 0)),
                  tok, _row_spec(gate2.shape[0], tile, d), pl.BlockSpec((1, d), lambda i: (0, 0))],
        out_specs=tok,
        out_shape=jax.ShapeDtypeStruct(x1.shape, F32),
        scratch_shapes=[pltpu.VMEM((2, TOP_K, tile, d), F32), pltpu.SemaphoreType.DMA((2,))],
        compiler_params=_cparams(("arbitrary",)),
        name="moe_combine",
    )(dest_flat, dest_flat, yr, topg, x1, gate2, gf)


def moe_block(groups, norm_f_g, final_norm, w_gu, b_gu, w_down, b_down, bm, tile):
    d = groups[0][0].shape[1]
    ne = w_gu.shape[0]
    experts_iota = jnp.arange(ne, dtype=jnp.int32)
    counts = [grp[5][0, :ne].astype(jnp.int32) for grp in groups]
    total = sum(counts)
    padded = ((total + bm - 1) // bm) * bm
    pend = jnp.cumsum(padded)
    base = pend - padded
    dests = []
    for grp, cnt in zip(groups, counts):
        idx = grp[2][:, :TOP_K].astype(jnp.int32)
        onehot = (idx[..., None] == experts_iota).astype(jnp.int32)
        dests.append((jnp.sum(onehot * base, axis=-1) + grp[4][:, :TOP_K].astype(jnp.int32)).reshape(-1))
        base = base + cnt
    n_assign = sum(grp[0].shape[0] for grp in groups) * TOP_K
    n_rows = -(-n_assign // bm) * bm + ne * bm
    n_blocks = n_rows // bm
    starts = jnp.arange(n_blocks, dtype=jnp.int32) * bm
    block_e = jnp.minimum(jnp.sum((starts[:, None] >= pend[None, :]).astype(jnp.int32), axis=1), ne - 1)
    n_used = (pend[-1:] // bm).astype(jnp.int32)
    last_e = block_e[jnp.maximum(n_used[0] - 1, 0)]
    block_e = jnp.where(jnp.arange(n_blocks) < n_used[0], block_e, last_e).astype(jnp.int32)

    xr = jnp.zeros((n_rows, d), F32)
    for grp, dest in zip(groups, dests):
        xr = dispatch(grp[1], dest, xr, tile)
    yr = experts(xr, block_e, n_used, w_gu, b_gu, w_down, b_down, bm)
    return [combine(yr, dest, grp[3], grp[0], grp[6], norm_f_g.reshape(1, d), tile, final_norm)
            for grp, dest in zip(groups, dests)]


def _group_forward(x, mod, attend, h0_lanes, t_per_seq, lw, mats, rows):
    n, d = x.shape
    rows = min(rows, n)
    (norm1_g, norm2_g, w_in_b, w_out_b, w_glu_b, b_glu, d_vec, wr_hi, wr_lo, br) = lw
    kmat, smat, rmat, decay_re, decay_im = mats
    shift1, scale1, gate1, shift2, scale2, gate2 = [mod[:, j * d:(j + 1) * d] for j in range(6)]
    aw = (w_in_b.shape[1] - w_glu_b.shape[0]) // 3
    q_dtype = BF16 if h0_lanes is None else F32
    q, k, v, kb, vb, u = inproj(x, shift1, scale1, norm1_g, w_in_b, aw, q_dtype, rows)
    o_attn = attend(q, k, v, kb, vb)

    sw = u.shape[1]
    nb = sw // LANES
    u2 = u.reshape(n // SSM_CHUNK, SSM_CHUNK * sw)
    nc = u2.shape[0]
    crow = min(256, nc)
    if h0_lanes is None:
        s = ssm_state(u2, smat, crow)
        hs, hend = ssm_scan(s, decay_re, decay_im, crow, nb)
    else:
        assert t_per_seq == SSM_CHUNK
        hs = h0_lanes
        hend = ssm_state(u2, smat, crow, h0_lanes, decay_re, decay_im)
    y_ssm = ssm_out(u2, hs, kmat, rmat, d_vec, w_glu_b, b_glu, crow).reshape(n, sw)

    routed = postmix(x, o_attn, y_ssm, w_out_b, gate1, norm2_g, scale2, shift2, wr_hi, wr_lo, br, rows)
    return tuple(routed) + (gate2,), k, v, hend


def kernel(x_prompt, x_sample, c_prompt, c_sample, cache_k, cache_v, state_ssm_re, state_ssm_im, page_table, norm1_g, norm2_g, w_ada, b_ada, w_in, w_out, sb_bias, lam_re, lam_im, log_dt, ssm_b_re, ssm_b_im, ssm_c_re, ssm_c_im, ssm_d, w_glu, b_glu, w_router, b_router, w_gu, b_gu, w_down, b_down, norm_f_g):
    depth = w_in.shape[0]
    bp, tp, d = x_prompt.shape
    bs, ts, _ = x_sample.shape
    assert bp == 1, "the prompt group is handled as one long sequence"
    n_heads = cache_k.shape[3]
    aw = n_heads * HEAD_DIM
    g, p = lam_re.shape[1:]
    nb = g // GROUP_BLOCK
    ne = w_router.shape[-1]
    n_pool, page = cache_k.shape[1:3]
    rows = 512

    xp = x_prompt.reshape(bp * tp, d)
    xs = x_sample.reshape(bs * ts, d)
    n_c = bp + bs
    c_all = jnp.concatenate([c_prompt, c_sample, jnp.zeros((-n_c % SUBLANES, d), F32)], axis=0)

    outs = {name: [] for name in ("kp", "vp", "rp", "ip", "ks", "vs", "rs", "is")}
    for l in range(depth):
        mod = ada_mod(c_all, w_ada[l], b_ada[l])
        mod_p = mod[:bp]
        mod_s = jnp.repeat(mod[bp:n_c], ts, axis=0)
        mats = _ssm_mats(lam_re[l], lam_im[l], log_dt[l], ssm_b_re[l], ssm_b_im[l], ssm_c_re[l], ssm_c_im[l])
        wr = jnp.pad(w_router[l], ((0, 0), (0, ROUTER_PAD - ne)))
        wr_hi = wr.astype(BF16)
        wr_lo = (wr - wr_hi.astype(F32)).astype(BF16)
        br = jnp.concatenate([b_router[l], jnp.full((ROUTER_PAD - ne,), NEG_BIG, F32)]).reshape(1, ROUTER_PAD)
        lw = (norm1_g[l].reshape(1, d), norm2_g[l].reshape(1, d), w_in[l].astype(BF16), w_out[l].astype(BF16),
              w_glu[l].astype(BF16), b_glu[l].reshape(1, -1), ssm_d[l].reshape(1, -1), wr_hi, wr_lo, br)
        bias = sb_bias[l]

        def attend_p(q, k, v, kb, vb):
            return attn_prompt_pairs(q, kb, vb, bias, nsb=min(8, tp // KEY_BLOCK))

        ck = cache_k[l].transpose(0, 2, 3, 1).reshape(n_pool, aw, page)
        cv = cache_v[l].transpose(0, 2, 3, 1).reshape(n_pool, aw, page)

        def attend_s(q, k, v, kb, vb):
            return attn_sample(q, k, v, ck, cv, page_table, bias, ts, group=math.gcd(8, page_table.shape[1]))

        h0 = _state_to_lanes(state_ssm_re[l], state_ssm_im[l], nb)
        routed_p, kp, vp, hp = _group_forward(xp, mod_p, attend_p, None, tp, lw, mats, rows)
        routed_s, ks, vs, hs = _group_forward(xs, mod_s, attend_s, h0, ts, lw, mats, rows)
        xp, xs = moe_block([routed_p, routed_s], norm_f_g, l == depth - 1, w_gu[l], b_gu[l], w_down[l], b_down[l],
                           bm=EXPERT_ROWS, tile=256)
        rp, ip = _lanes_to_state(hp, g, p)
        rs, is_ = _lanes_to_state(hs, g, p)
        outs["kp"].append(kp.reshape(bp, tp, n_heads, HEAD_DIM))
        outs["vp"].append(vp.reshape(bp, tp, n_heads, HEAD_DIM))
        outs["rp"].append(rp)
        outs["ip"].append(ip)
        outs["ks"].append(ks.reshape(bs, ts, n_heads, HEAD_DIM))
        outs["vs"].append(vs.reshape(bs, ts, n_heads, HEAD_DIM))
        outs["rs"].append(rs)
        outs["is"].append(is_)
    st = lambda name: jnp.stack(outs[name])
    return (xp.reshape(bp, tp, d), xs.reshape(bs, ts, d), st("kp"), st("vp"), st("rp"), st("ip"),
            st("ks"), st("vs"), st("rs"), st("is"))
```

```python
import functools
import math

import jax
import jax.numpy as jnp
from jax import lax
from jax.experimental import pallas as pl
from jax.experimental.pallas import tpu as pltpu

F32 = jnp.float32
BF16 = jnp.bfloat16

HEAD_DIM = 64
SSM_GROUP = 16
TOP_K = 4
SWIGLU_LIMIT = 7.0
SWIGLU_ALPHA = 1.702
RMS_EPS = 1e-6

LANES = 128
SUBLANES = 8
VMEM_LIMIT_BYTES = 56 * 1024 * 1024

KEY_BLOCK = LANES
SSM_CHUNK = 8
GROUP_BLOCK = LANES // SSM_GROUP
ROUTER_PAD = LANES
EXPERT_ROWS = 512
NEG_BIG = -1e30
EXP_CAP = 1e30
LOG2E = math.log2(math.e)
PAGE_SLOTS = 3

def _cparams(semantics):
    return pltpu.CompilerParams(dimension_semantics=semantics, vmem_limit_bytes=VMEM_LIMIT_BYTES)


def _dot(a, b):
    return jnp.dot(a, b, preferred_element_type=F32)


def _dot_nt(a, b):
    return lax.dot_general(a, b, (((1,), (1,)), ((), ())), preferred_element_type=F32)


def _split_bf16(x):
    hi = x.astype(BF16)
    lo = (x - hi.astype(F32)).astype(BF16)
    return hi, lo


def _ada_kernel(c_ref, w_ref, b_ref, o_ref):
    o_ref[...] = _dot(c_ref[...].astype(BF16), w_ref[...].astype(BF16)) + b_ref[...]


def ada_mod(c, w_ada, b_ada):
    n, d = c.shape
    n_out = w_ada.shape[1]
    return pl.pallas_call(
        _ada_kernel,
        grid=(n_out // d,),
        in_specs=[pl.BlockSpec((n, d), lambda j: (0, 0)),
                  pl.BlockSpec((d, d), lambda j: (0, j)),
                  pl.BlockSpec((1, d), lambda j: (0, j))],
        out_specs=pl.BlockSpec((n, d), lambda j: (0, j)),
        out_shape=jax.ShapeDtypeStruct((n, n_out), F32),
        compiler_params=_cparams(("arbitrary",)),
        name="ada_mod",
    )(c, w_ada, b_ada.reshape(1, n_out))


def _inproj_kernel(x_ref, shift_ref, scale_ref, g_ref, w_ref, q_ref, k_ref, v_ref, kb_ref, vb_ref, u_ref, *, aw):
    x = x_ref[...]
    ms = jnp.mean(x * x, axis=-1, keepdims=True)
    hn = x * lax.rsqrt(ms + RMS_EPS) * g_ref[...]
    hn = hn * (1.0 + scale_ref[...]) + shift_ref[...]
    proj = _dot(hn.astype(BF16), w_ref[...])
    q_ref[...] = (proj[:, :aw] * (HEAD_DIM ** -0.5 * LOG2E)).astype(q_ref.dtype)
    k = proj[:, aw:2 * aw]
    v = proj[:, 2 * aw:3 * aw]
    k_ref[...] = k
    v_ref[...] = v
    kb_ref[...] = k.astype(BF16)
    vb_ref[...] = v.astype(BF16)
    u_ref[...] = proj[:, 3 * aw:]


def _row_spec(n_mod_rows, rows, d):
    if n_mod_rows == 1:
        return pl.BlockSpec((1, d), lambda i: (0, 0))
    return pl.BlockSpec((rows, d), lambda i: (i, 0))


def inproj(x, shift, scale, g, w_in_b, aw, q_dtype, rows):
    n, d = x.shape
    pw = w_in_b.shape[1]
    sw = pw - 3 * aw
    tok = lambda width: pl.BlockSpec((rows, width), lambda i: (i, 0))
    return pl.pallas_call(
        functools.partial(_inproj_kernel, aw=aw),
        grid=(n // rows,),
        in_specs=[tok(d), _row_spec(shift.shape[0], rows, d), _row_spec(scale.shape[0], rows, d),
                  pl.BlockSpec((1, d), lambda i: (0, 0)),
                  pl.BlockSpec((d, pw), lambda i: (0, 0))],
        out_specs=[tok(aw), tok(aw), tok(aw), tok(aw), tok(aw), tok(sw)],
        out_shape=[jax.ShapeDtypeStruct((n, aw), q_dtype),
                   jax.ShapeDtypeStruct((n, aw), F32), jax.ShapeDtypeStruct((n, aw), F32),
                   jax.ShapeDtypeStruct((n, aw), BF16), jax.ShapeDtypeStruct((n, aw), BF16),
                   jax.ShapeDtypeStruct((n, sw), F32)],
        compiler_params=_cparams(("arbitrary",)),
        name="inproj",
    )(x, shift, scale, g, w_in_b)


def _tri_ones():
    j = lax.broadcasted_iota(jnp.int32, (KEY_BLOCK, 2 * KEY_BLOCK), 0)
    s = lax.broadcasted_iota(jnp.int32, (KEY_BLOCK, 2 * KEY_BLOCK), 1)
    return jnp.where((j > s) | (s >= KEY_BLOCK), 1.0, 0.0).astype(BF16)


def _sb_drop(z, mask):
    drop = jnp.maximum(jnp.log(1.0 + jnp.minimum(jnp.exp2(z), EXP_CAP)) * LOG2E, z)
    if mask is not None:
        drop = jnp.where(mask, drop, 0.0)
    return drop


def _sb_logits(z, mask):
    drop = _sb_drop(z, mask)
    return z - drop, drop.astype(BF16)


def _weights_from_log2(x):
    return jnp.exp2(x).astype(BF16)


def _sb_finish(log_beta, drop_b, carry, tri, mask):
    groups = drop_b.shape[1] // KEY_BLOCK
    later, total = [], []
    for g in range(groups):
        cs = _dot(drop_b[:, g * KEY_BLOCK:(g + 1) * KEY_BLOCK], tri)
        later.append(cs[:, :KEY_BLOCK])
        total.append(cs[:, KEY_BLOCK:])
    later = jnp.concatenate(later, axis=1) if groups > 1 else later[0]
    total = jnp.concatenate(total, axis=1) if groups > 1 else total[0]
    a = _weights_from_log2(log_beta - (later + carry))
    if mask is not None:
        a = jnp.where(mask, a, jnp.zeros_like(a))
    return a, total


def _sb_weights(z, carry, tri, mask):
    if mask is not None:
        mask = jnp.concatenate([mask] * (z.shape[1] // KEY_BLOCK), axis=1)
    log_beta, drop_b = _sb_logits(z, mask)
    return _sb_finish(log_beta, drop_b, carry, tri, mask)


def _tri_suffix(width):
    j = lax.broadcasted_iota(jnp.int32, (width, width), 0)
    s = lax.broadcasted_iota(jnp.int32, (width, width), 1)
    return jnp.where(j >= s, 1.0, 0.0).astype(BF16)


def _attn_pairs_kernel(kbias_ref, q_ref, k_ref, v_ref, tri_ref, o_ref, acc_ref, carry_ref, lb_scr, drop_scr,
                       qext_scr, *, nsb):
    it = pl.program_id(1)
    unit = 2 * KEY_BLOCK
    tq = nsb * KEY_BLOCK
    n_units = nsb // 2
    tri = tri_ref[...]
    ones_lanes = lax.broadcasted_iota(jnp.int32, (tq, LANES), 1) < 2
    qext_scr[:, :LANES] = q_ref[...]
    qext_scr[:, LANES:] = jnp.where(ones_lanes, 1.0, 0.0).astype(BF16)
    kbias = kbias_ref[0]
    acc_ref[...] = jnp.zeros_like(acc_ref)
    carry_ref[...] = jnp.zeros_like(carry_ref)
    lane = lax.broadcasted_iota(jnp.int32, (unit, LANES), 1)
    first_head = lane < HEAD_DIM

    def stacked(ref, u):
        st = pl.multiple_of(u * unit, unit)
        slab = ref[pl.ds(st, unit), :]
        zero = jnp.zeros_like(slab)
        return jnp.concatenate([jnp.where(first_head, slab, zero), jnp.where(first_head, zero, slab)], axis=0)

    def logits(r0, u, mask):
        z = _dot_nt(qext_scr[r0:, :], jnp.concatenate([stacked(k_ref, u), kbias], axis=1))
        return z, _sb_drop(z, mask).astype(BF16)

    def finish(r0, u, z, drop_b, mask):
        carry = carry_ref[r0:, :]
        m = z.shape[0]
        parts, new_carry = [], []
        for h in range(2):
            incl = _dot(drop_b[:, h * unit:(h + 1) * unit], tri)
            c_h = carry[:, h * KEY_BLOCK:(h + 1) * KEY_BLOCK]
            parts.append(incl + jnp.concatenate([c_h, c_h], axis=1))
            new_carry.append(c_h + jnp.broadcast_to(incl[:, 0:1], (m, KEY_BLOCK)))
        a = jnp.exp2((z - jnp.concatenate(parts, axis=1)).astype(BF16))
        if mask is not None:
            a = jnp.where(mask, a, jnp.zeros_like(a))
        acc_ref[r0:, :] += _dot(a, stacked(v_ref, u))
        carry_ref[r0:, :] = jnp.concatenate(new_carry, axis=1)

    for c in reversed(range(n_units)):
        r0 = c * unit
        rr = lax.broadcasted_iota(jnp.int32, (tq - r0, unit), 0)
        ll = lax.broadcasted_iota(jnp.int32, (tq - r0, unit), 1)
        mask = jnp.concatenate([ll < rr] * 2, axis=1)
        z, drop_b = logits(r0, it * n_units + c, mask)
        finish(r0, it * n_units + c, z, drop_b, mask)

    n_full = it * n_units

    def first_half(u, slot):
        z, drop_b = logits(0, u, None)
        lb_scr[slot] = z
        drop_scr[slot] = drop_b

    def second_half(u, slot):
        finish(0, u, lb_scr[slot], drop_scr[slot], None)

    @pl.when(n_full > 0)
    def _():
        per_trip = math.gcd(n_units, 4)
        first_half(n_full - 1, 0)

        def body(jj, c):
            u0 = n_full - 1 - per_trip * jj
            for i in range(per_trip):
                first_half(u0 - i - 1, (i + 1) % 2)
                second_half(u0 - i, i % 2)
            return c

        lax.fori_loop(0, n_full // per_trip - 1, body, 0)
        for i in range(per_trip - 1):
            first_half(per_trip - 2 - i, (i + 1) % 2)
            second_half(per_trip - 1 - i, i % 2)
        second_half(0, (per_trip - 1) % 2)

    o_ref[...] = acc_ref[...].astype(o_ref.dtype)


def attn_prompt_pairs(q, kb, vb, bias, nsb):
    n, aw = q.shape
    tq = nsb * KEY_BLOCK
    unit = 2 * KEY_BLOCK
    n_pairs = aw // LANES
    assert nsb % 4 == 0 and n % tq == 0
    b_hi, b_lo = _split_bf16(bias.astype(F32) * LOG2E)
    lane = jnp.arange(LANES)
    kbias = jnp.where(lane == 0, b_hi[:, None, None], jnp.where(lane == 1, b_lo[:, None, None], 0)).astype(BF16)
    kbias = jnp.broadcast_to(kbias, (bias.shape[0], unit, LANES)).reshape(n_pairs, 2 * unit, LANES)
    tri = _tri_suffix(unit)
    return pl.pallas_call(
        functools.partial(_attn_pairs_kernel, nsb=nsb),
        grid=(n_pairs, n // tq),
        in_specs=[pl.BlockSpec((1, 2 * unit, LANES), lambda hp, i: (hp, 0, 0)),
                  pl.BlockSpec((tq, LANES), lambda hp, i: (i, hp)),
                  pl.BlockSpec((n, LANES), lambda hp, i: (0, hp)),
                  pl.BlockSpec((n, LANES), lambda hp, i: (0, hp)),
                  pl.BlockSpec(tri.shape, lambda hp, i: (0, 0))],
        out_specs=pl.BlockSpec((tq, LANES), lambda hp, i: (i, hp)),
        out_shape=jax.ShapeDtypeStruct((n, aw), BF16),
        scratch_shapes=[pltpu.VMEM((tq, LANES), F32), pltpu.VMEM((tq, 2 * KEY_BLOCK), F32),
                        pltpu.VMEM((2, tq, 2 * unit), F32), pltpu.VMEM((2, tq, 2 * unit), BF16),
                        pltpu.VMEM((tq, 2 * LANES), BF16)],
        compiler_params=_cparams(("arbitrary", "arbitrary")),
        name="attn_prompt",
    )(kbias, q, kb, vb, tri)


def _attn_sample_kernel(pt_ref, q_ref, kn_ref, vn_ref, ck_hbm, cv_hbm, tri_ref, bias_ref, o_ref,
                        kbuf, vbuf, sems, qbd_ref, acc_ref, carry_ref, *, group, n_groups, n_heads, t_new):
    b = pl.program_id(0)
    total = pl.num_programs(0) * n_groups
    m = n_heads * t_new
    aw = n_heads * HEAD_DIM
    tri = tri_ref[...]
    bias = bias_ref[...]

    def request(s):
        slot = s % PAGE_SLOTS
        for i in range(group):
            page = pt_ref[s * group + i]
            pltpu.make_async_copy(ck_hbm.at[page], kbuf.at[slot, i], sems.at[slot, 0]).start()
            pltpu.make_async_copy(cv_hbm.at[page], vbuf.at[slot, i], sems.at[slot, 1]).start()

    def wait_group(slot):
        pltpu.make_async_copy(kbuf.at[slot], kbuf.at[slot], sems.at[slot, 0]).wait()
        pltpu.make_async_copy(vbuf.at[slot], vbuf.at[slot], sems.at[slot, 1]).wait()

    @pl.when(b == 0)
    def _():
        for s in range(PAGE_SLOTS - 1):
            request(s)

    q = q_ref[...].astype(BF16)
    qt = jnp.concatenate([q] * n_heads, axis=0)
    rw = lax.broadcasted_iota(jnp.int32, (m, aw), 0)
    ln = lax.broadcasted_iota(jnp.int32, (m, aw), 1)
    qbd_ref[...] = jnp.where(ln // HEAD_DIM == rw // t_new, qt, jnp.zeros_like(qt))
    pad = jnp.zeros((KEY_BLOCK - t_new, aw), F32)
    kn = jnp.concatenate([kn_ref[...], pad], axis=0).astype(BF16)
    vn = jnp.concatenate([vn_ref[...], pad], axis=0).astype(BF16)
    r2 = lax.broadcasted_iota(jnp.int32, (m, KEY_BLOCK), 0)
    l2 = lax.broadcasted_iota(jnp.int32, (m, KEY_BLOCK), 1)
    z = _dot_nt(qbd_ref[...], kn) + bias
    a, tot = _sb_weights(z, 0.0, tri, l2 < (r2 % t_new))
    acc_ref[...] = _dot(a, vn)
    carry_ref[...] = tot

    def body(g, c):
        s = b * n_groups + g

        @pl.when(s + PAGE_SLOTS - 1 < total)
        def _():
            request(s + PAGE_SLOTS - 1)

        slot = s % PAGE_SLOTS
        wait_group(slot)
        kt = jnp.concatenate([kbuf[slot, i].astype(BF16) for i in range(group)], axis=1)
        vt = jnp.concatenate([vbuf[slot, i].astype(BF16) for i in range(group)], axis=1)
        z = _dot(qbd_ref[...], kt) + jnp.concatenate([bias] * group, axis=1)
        log_beta, drop_b = _sb_logits(z, None)
        carry = carry_ref[...]
        shift = []
        for i in range(group):
            cs = _dot(drop_b[:, i * KEY_BLOCK:(i + 1) * KEY_BLOCK], tri)
            shift.append(cs[:, :KEY_BLOCK] + carry)
            carry = carry + cs[:, KEY_BLOCK:]
        a = _weights_from_log2(log_beta - jnp.concatenate(shift, axis=1))
        acc_ref[...] += _dot_nt(a, vt)
        carry_ref[...] = carry
        return c

    lax.fori_loop(0, n_groups, body, 0)

    acc = acc_ref[...]
    lo = lax.broadcasted_iota(jnp.int32, (t_new, aw), 1)
    o = jnp.zeros((t_new, aw), F32)
    for h in range(n_heads):
        o = o + jnp.where(lo // HEAD_DIM == h, acc[h * t_new:(h + 1) * t_new, :], 0.0)
    o_ref[...] = o.astype(o_ref.dtype)


def attn_sample(q, k_new, v_new, cache_k, cache_v, page_table, bias, t_new, group):
    n, aw = q.shape
    bsz, n_pages = page_table.shape
    n_heads = aw // HEAD_DIM
    page = cache_k.shape[2]
    n_groups = n_pages // group
    assert page == KEY_BLOCK and t_new == SUBLANES and n_pages % group == 0 and bsz * n_groups >= PAGE_SLOTS
    m = n_heads * t_new
    bias_rows = jnp.broadcast_to(jnp.repeat(bias * LOG2E, t_new)[:, None], (m, KEY_BLOCK)).astype(F32)
    pages_recent_first = page_table[:, ::-1].reshape(-1)
    tok = pl.BlockSpec((t_new, aw), lambda b, pt: (b, 0))
    ring = pltpu.VMEM((PAGE_SLOTS, group, aw, page), F32)
    grid_spec = pltpu.PrefetchScalarGridSpec(
        num_scalar_prefetch=1,
        grid=(bsz,),
        in_specs=[tok, tok, tok, pl.BlockSpec(memory_space=pl.ANY), pl.BlockSpec(memory_space=pl.ANY),
                  pl.BlockSpec((KEY_BLOCK, 2 * KEY_BLOCK), lambda b, pt: (0, 0)),
                  pl.BlockSpec((m, KEY_BLOCK), lambda b, pt: (0, 0))],
        out_specs=tok,
        scratch_shapes=[ring, ring, pltpu.SemaphoreType.DMA((PAGE_SLOTS, 2)),
                        pltpu.VMEM((m, aw), BF16), pltpu.VMEM((m, aw), F32), pltpu.VMEM((m, KEY_BLOCK), F32)],
    )
    return pl.pallas_call(
        functools.partial(_attn_sample_kernel, group=group, n_groups=n_groups, n_heads=n_heads, t_new=t_new),
        grid_spec=grid_spec,
        out_shape=jax.ShapeDtypeStruct((n, aw), BF16),
        compiler_params=_cparams(("arbitrary",)),
        name="attn_sample",
    )(pages_recent_first, q, k_new, v_new, cache_k, cache_v, _tri_ones(), bias_rows)


def _ssm_mats(lam_re, lam_im, log_dt, b_re, b_im, c_re, c_im):
    g, p = lam_re.shape
    hh = b_re.shape[-1]
    el = SSM_CHUNK
    nb = g // GROUP_BLOCK
    dt = jnp.exp(log_dt)[:, None]
    ar, ai = lam_re * dt, lam_im * dt
    lbr, lbi = jnp.exp(ar) * jnp.cos(ai), jnp.exp(ar) * jnp.sin(ai)
    den = lam_re * lam_re + lam_im * lam_im
    fr = ((lbr - 1.0) * lam_re + lbi * lam_im) / den
    fi = (lbi * lam_re - (lbr - 1.0) * lam_im) / den
    bbr = fr[..., None] * b_re - fi[..., None] * b_im
    bbi = fr[..., None] * b_im + fi[..., None] * b_re
    n = jnp.arange(el + 1, dtype=F32)[:, None, None]
    pr = jnp.exp(ar[None] * n) * jnp.cos(ai[None] * n)
    pi = jnp.exp(ar[None] * n) * jnp.sin(ai[None] * n)
    eye = jnp.eye(GROUP_BLOCK, dtype=F32)

    cpr = c_re[None] * pr[:el, :, None, :] - c_im[None] * pi[:el, :, None, :]
    cpi = c_re[None] * pi[:el, :, None, :] + c_im[None] * pr[:el, :, None, :]
    kd = jnp.einsum('dgip,gpj->dgij', cpr, bbr) - jnp.einsum('dgip,gpj->dgij', cpi, bbi)
    kmat = jnp.einsum('dkgij,gh->dkgjhi', kd.reshape(el, nb, GROUP_BLOCK, hh, hh), eye)
    kmat = kmat.reshape(el, nb, LANES, LANES)

    rev = pr[el - 1 - jnp.arange(el)], pi[el - 1 - jnp.arange(el)]
    scr = rev[0][..., None] * bbr[None] - rev[1][..., None] * bbi[None]
    sci = rev[0][..., None] * bbi[None] + rev[1][..., None] * bbr[None]

    def blk_s(a):
        a = jnp.einsum('skgpj,gh->skgjhp', a.reshape(el, nb, GROUP_BLOCK, p, hh), eye)
        return a.reshape(el, nb, LANES, GROUP_BLOCK * p)

    smat = jnp.concatenate([blk_s(scr), blk_s(sci)], axis=-1)

    c1r = c_re[None] * pr[1:, :, None, :] - c_im[None] * pi[1:, :, None, :]
    c1i = c_re[None] * pi[1:, :, None, :] + c_im[None] * pr[1:, :, None, :]

    def blk_r(a):
        a = jnp.einsum('tkgip,gh->tkgphi', a.reshape(el, nb, GROUP_BLOCK, hh, p), eye)
        return a.reshape(el, nb, GROUP_BLOCK * p, LANES)

    rmat = jnp.concatenate([blk_r(c1r), -blk_r(c1i)], axis=-2)

    decay_re = pr[el].reshape(1, g * p)
    decay_im = pi[el].reshape(1, g * p)
    return kmat.astype(BF16), smat.astype(BF16), rmat.astype(BF16), decay_re, decay_im


def _state_to_lanes(h_re, h_im, nb):
    b = h_re.shape[0]
    return jnp.stack([h_re.reshape(b, nb, -1), h_im.reshape(b, nb, -1)], axis=2).reshape(b, -1)


def _lanes_to_state(h, g, p):
    b = h.shape[0]
    nb = g // GROUP_BLOCK
    h = h.reshape(b, nb, 2, GROUP_BLOCK, p)
    return h[:, :, 0].reshape(b, g, p), h[:, :, 1].reshape(b, g, p)


def _ssm_state_kernel(*refs, has_h0, nb, sw, half):
    if has_h0:
        u_ref, smat_ref, h0_ref, dre_ref, dim_ref, s_ref = refs
    else:
        u_ref, smat_ref, s_ref = refs
    for k in range(nb):
        acc = None
        for s in range(SSM_CHUNK):
            ub = u_ref[:, s * sw + k * LANES: s * sw + (k + 1) * LANES].astype(BF16)
            d = _dot(ub, smat_ref[s, k])
            acc = d if acc is None else acc + d
        base = 2 * half * k
        if has_h0:
            hr = h0_ref[:, base:base + half]
            hi = h0_ref[:, base + half:base + 2 * half]
            dr = dre_ref[:, k * half:(k + 1) * half]
            di = dim_ref[:, k * half:(k + 1) * half]
            s_ref[:, base:base + half] = acc[:, :half] + dr * hr - di * hi
            s_ref[:, base + half:base + 2 * half] = acc[:, half:] + dr * hi + di * hr
        else:
            s_ref[:, base:base + 2 * half] = acc


def ssm_state(u2, smat, rows, h0=None, decay_re=None, decay_im=None):
    nc, width = u2.shape
    sw = width // SSM_CHUNK
    nb = sw // LANES
    half = smat.shape[-1] // 2
    sl = nb * 2 * half
    has_h0 = h0 is not None
    row = lambda w: pl.BlockSpec((rows, w), lambda i: (i, 0))
    in_specs = [row(width), pl.BlockSpec(smat.shape, lambda i: (0, 0, 0, 0))]
    args = [u2, smat]
    if has_h0:
        in_specs += [row(sl), pl.BlockSpec((1, nb * half), lambda i: (0, 0)),
                     pl.BlockSpec((1, nb * half), lambda i: (0, 0))]
        args += [h0, decay_re, decay_im]
    return pl.pallas_call(
        functools.partial(_ssm_state_kernel, has_h0=has_h0, nb=nb, sw=sw, half=half),
        grid=(nc // rows,),
        in_specs=in_specs,
        out_specs=row(sl),
        out_shape=jax.ShapeDtypeStruct((nc, sl), F32),
        compiler_params=_cparams(("arbitrary",)),
        name="ssm_state",
    )(*args)


def _ssm_scan_kernel(s_ref, dre_ref, dim_ref, hs_ref, hend_ref, h_scr, *, nb, half):
    @pl.when(pl.program_id(0) == 0)
    def _():
        h_scr[...] = jnp.zeros_like(h_scr)

    dr = dre_ref[...]
    di = dim_ref[...]

    def body(r, h):
        hs_ref[pl.ds(r, 1), :] = h
        s = s_ref[pl.ds(r, 1), :]
        parts = []
        for k in range(nb):
            base = 2 * half * k
            hr, hi = h[:, base:base + half], h[:, base + half:base + 2 * half]
            ar, ai = dr[:, k * half:(k + 1) * half], di[:, k * half:(k + 1) * half]
            parts.append(ar * hr - ai * hi + s[:, base:base + half])
            parts.append(ar * hi + ai * hr + s[:, base + half:base + 2 * half])
        return jnp.concatenate(parts, axis=1)

    h = lax.fori_loop(0, s_ref.shape[0], body, h_scr[...])
    h_scr[...] = h
    hend_ref[...] = h


def ssm_scan(s, decay_re, decay_im, rows, nb):
    nc, sl = s.shape
    n_half_total = decay_re.shape[1]
    half = n_half_total // nb
    row = pl.BlockSpec((rows, sl), lambda i: (i, 0))
    vec = pl.BlockSpec((1, n_half_total), lambda i: (0, 0))
    return pl.pallas_call(
        functools.partial(_ssm_scan_kernel, nb=nb, half=half),
        grid=(nc // rows,),
        in_specs=[row, vec, vec],
        out_specs=[row, pl.BlockSpec((1, sl), lambda i: (0, 0))],
        out_shape=[jax.ShapeDtypeStruct((nc, sl), F32), jax.ShapeDtypeStruct((1, sl), F32)],
        scratch_shapes=[pltpu.VMEM((1, sl), F32)],
        compiler_params=_cparams(("arbitrary",)),
        name="ssm_scan",
    )(s, decay_re, decay_im)


def _gelu_exact(y):
    return 0.5 * y * (1.0 + lax.erf(y * (0.5 ** 0.5)))


def _ssm_out_kernel(u_ref, hs_ref, kmat_ref, rmat_ref, d_ref, wglu_ref, bglu_ref, y_ref, *, nb, sw, half):
    hb = [hs_ref[:, 2 * half * k:2 * half * (k + 1)].astype(BF16) for k in range(nb)]
    ub = [[u_ref[:, s * sw + k * LANES: s * sw + (k + 1) * LANES].astype(BF16) for k in range(nb)]
          for s in range(SSM_CHUNK)]
    wglu = wglu_ref[...]
    for t in range(SSM_CHUNK):
        cols = []
        for k in range(nb):
            acc = _dot(hb[k], rmat_ref[t, k])
            for s in range(t + 1):
                acc = acc + _dot(ub[s][k], kmat_ref[t - s, k])
            cols.append(acc)
        y = jnp.concatenate(cols, axis=1) + d_ref[...] * u_ref[:, t * sw:(t + 1) * sw]
        g = _gelu_exact(y)
        gate = jax.nn.sigmoid(_dot(g.astype(BF16), wglu) + bglu_ref[...])
        y_ref[:, t * sw:(t + 1) * sw] = (g * gate).astype(y_ref.dtype)


def ssm_out(u2, hs, kmat, rmat, d_vec, w_glu_b, b_glu, rows):
    nc, width = u2.shape
    sw = width // SSM_CHUNK
    nb = sw // LANES
    sl = hs.shape[1]
    half = sl // (2 * nb)
    row = lambda w: pl.BlockSpec((rows, w), lambda i: (i, 0))
    const = lambda a: pl.BlockSpec(a.shape, lambda i: (0,) * a.ndim)
    return pl.pallas_call(
        functools.partial(_ssm_out_kernel, nb=nb, sw=sw, half=half),
        grid=(nc // rows,),
        in_specs=[row(width), row(sl), const(kmat), const(rmat), const(d_vec), const(w_glu_b), const(b_glu)],
        out_specs=row(width),
        out_shape=jax.ShapeDtypeStruct((nc, width), BF16),
        compiler_params=_cparams(("arbitrary",)),
        name="ssm_out",
    )(u2, hs, kmat, rmat, d_vec, w_glu_b, b_glu)


def _postmix_kernel(x_ref, o_ref, y_ref, wo_ref, gate1_ref, g2_ref, scale2_ref, shift2_ref, wrh_ref, wrl_ref,
                    br_ref, tril_ref, x1_ref, hn2_ref, topi_ref, topg_ref, topr_ref, cnt_ref, carry_ref, *, aw):
    @pl.when(pl.program_id(0) == 0)
    def _():
        carry_ref[...] = jnp.zeros_like(carry_ref)

    mix = _dot(o_ref[...], wo_ref[:aw, :]) + _dot(y_ref[...], wo_ref[aw:, :])
    x1 = x_ref[...] + gate1_ref[...] * mix
    x1_ref[...] = x1
    ms = jnp.mean(x1 * x1, axis=-1, keepdims=True)
    hn2 = x1 * lax.rsqrt(ms + RMS_EPS) * g2_ref[...]
    hn2 = hn2 * (1.0 + scale2_ref[...]) + shift2_ref[...]
    hn2_ref[...] = hn2

    hh, hl = _split_bf16(hn2)
    wrh = wrh_ref[...]
    logits = _dot(hh, wrh) + _dot(hl, wrh) + _dot(hh, wrl_ref[...]) + br_ref[...]

    rows = logits.shape[0]
    lane = lax.broadcasted_iota(jnp.int32, (rows, ROUTER_PAD), 1)
    work = logits
    sel = jnp.zeros((rows, ROUTER_PAD), F32)
    picks, vals, idxs = [], [], []
    for _ in range(TOP_K):
        m = jnp.max(work, axis=-1, keepdims=True)
        idx = jnp.min(jnp.where(work == m, lane, ROUTER_PAD), axis=-1, keepdims=True)
        pick = lane == idx
        picks.append(pick)
        vals.append(m)
        idxs.append(idx)
        sel = jnp.where(pick, 1.0, sel)
        work = jnp.where(pick, -jnp.inf, work)

    rank = _dot(tril_ref[...], sel.astype(BF16)) + carry_ref[...]
    carry_ref[...] = rank[rows - 1:rows, :] + sel[rows - 1:rows, :]
    cnt_ref[...] = carry_ref[...]

    es = [jnp.exp(v - vals[0]) for v in vals]
    den = es[0]
    for e in es[1:]:
        den = den + e
    topi = jnp.zeros((rows, ROUTER_PAD), F32)
    topg = jnp.zeros((rows, ROUTER_PAD), F32)
    topr = jnp.zeros((rows, ROUTER_PAD), F32)
    for r in range(TOP_K):
        rk = jnp.sum(jnp.where(picks[r], rank, 0.0), axis=-1, keepdims=True)
        topi = jnp.where(lane == r, idxs[r].astype(F32), topi)
        topg = jnp.where(lane == r, es[r] / den, topg)
        topr = jnp.where(lane == r, rk, topr)
    topi_ref[...] = topi
    topg_ref[...] = topg
    topr_ref[...] = topr


def postmix(x, o_attn, y_ssm, w_out_b, gate1, g2, scale2, shift2, wr_hi, wr_lo, br, rows):
    n, d = x.shape
    aw = o_attn.shape[1]
    tok = lambda width: pl.BlockSpec((rows, width), lambda i: (i, 0))
    const = lambda a: pl.BlockSpec(a.shape, lambda i: (0,) * a.ndim)
    r = lax.broadcasted_iota(jnp.int32, (rows, rows), 0)
    c = lax.broadcasted_iota(jnp.int32, (rows, rows), 1)
    tril = jnp.where(c < r, 1.0, 0.0).astype(BF16)
    lanes_out = jax.ShapeDtypeStruct((n, ROUTER_PAD), F32)
    return pl.pallas_call(
        functools.partial(_postmix_kernel, aw=aw),
        grid=(n // rows,),
        in_specs=[tok(d), tok(aw), tok(y_ssm.shape[1]), const(w_out_b),
                  _row_spec(gate1.shape[0], rows, d), const(g2),
                  _row_spec(scale2.shape[0], rows, d), _row_spec(shift2.shape[0], rows, d),
                  const(wr_hi), const(wr_lo), const(br), const(tril)],
        out_specs=[tok(d), tok(d), tok(ROUTER_PAD), tok(ROUTER_PAD), tok(ROUTER_PAD),
                   pl.BlockSpec((1, ROUTER_PAD), lambda i: (0, 0))],
        out_shape=[jax.ShapeDtypeStruct((n, d), F32), jax.ShapeDtypeStruct((n, d), F32),
                   lanes_out, lanes_out, lanes_out, jax.ShapeDtypeStruct((1, ROUTER_PAD), F32)],
        scratch_shapes=[pltpu.VMEM((1, ROUTER_PAD), F32)],
        compiler_params=_cparams(("arbitrary",)),
        name="postmix",
    )(x, o_attn, y_ssm, w_out_b, gate1, g2, scale2, shift2, wr_hi, wr_lo, br, tril)


def _wait_rows(ref, n_copy, sem):
    span = ref.at[pl.ds(0, n_copy)]
    pltpu.make_async_copy(span, span, sem).wait()


def _dispatch_kernel(dest_ref, hn_ref, xr_in, xr_hbm, stage, sems, *, tile):
    del xr_in
    i = pl.program_id(0)
    slot = i % 2
    n_copy = tile * TOP_K
    stage[slot] = hn_ref[...]

    def start(t, c):
        for k in range(TOP_K):
            pltpu.make_async_copy(stage.at[slot, pl.ds(t, 1)], xr_hbm.at[pl.ds(dest_ref[t * TOP_K + k], 1)],
                                  sems.at[slot]).start()
        return c

    lax.fori_loop(0, tile, start, 0)

    @pl.when(i > 0)
    def _():
        _wait_rows(xr_hbm, n_copy, sems.at[1 - slot])

    @pl.when(i == pl.num_programs(0) - 1)
    def _():
        _wait_rows(xr_hbm, n_copy, sems.at[slot])


def dispatch(hn, dest_flat, xr0, tile):
    n, d = hn.shape
    return pl.pallas_call(
        functools.partial(_dispatch_kernel, tile=tile),
        grid=(n // tile,),
        in_specs=[pl.BlockSpec((tile * TOP_K,), lambda i: (i,), memory_space=pltpu.SMEM),
                  pl.BlockSpec((tile, d), lambda i: (i, 0)),
                  pl.BlockSpec(memory_space=pl.ANY)],
        out_specs=pl.BlockSpec(memory_space=pl.ANY),
        out_shape=jax.ShapeDtypeStruct(xr0.shape, xr0.dtype),
        scratch_shapes=[pltpu.VMEM((2, tile, d), hn.dtype), pltpu.SemaphoreType.DMA((2,))],
        input_output_aliases={2: 0},
        compiler_params=_cparams(("arbitrary",)),
        name="moe_dispatch",
    )(dest_flat, hn, xr0)


def _expert_kernel(be_ref, nb_ref, x_ref, wgu_ref, bgu_ref, wd_ref, bd_ref, y_ref, wgu_b, wd_b, *, ff):
    i = pl.program_id(0)

    @pl.when(i < nb_ref[0])
    def _():
        prev = be_ref[jnp.maximum(i - 1, 0)]

        @pl.when((i == 0) | (be_ref[i] != prev))
        def _():
            wgu_b[...] = wgu_ref[0].astype(BF16)
            wd_b[...] = wd_ref[0].astype(BF16)

        gu = _dot(x_ref[...].astype(BF16), wgu_b[...]) + bgu_ref[0]
        gate = jnp.minimum(gu[:, :ff], SWIGLU_LIMIT)
        up = jnp.clip(gu[:, ff:], -SWIGLU_LIMIT, SWIGLU_LIMIT)
        act = (up + 1.0) * (gate * jax.nn.sigmoid(SWIGLU_ALPHA * gate))
        y_ref[...] = _dot(act.astype(BF16), wd_b[...]) + bd_ref[0]

    @pl.when(i >= nb_ref[0])
    def _():
        y_ref[...] = jnp.zeros_like(y_ref)


def experts(xr, block_e, n_used, w_gu, b_gu, w_down, b_down, bm):
    n_rows, d = xr.shape
    ne, _, ff2 = w_gu.shape
    ff = ff2 // 2
    grid_spec = pltpu.PrefetchScalarGridSpec(
        num_scalar_prefetch=2,
        grid=(n_rows // bm,),
        in_specs=[pl.BlockSpec((bm, d), lambda i, be, nb: (i, 0)),
                  pl.BlockSpec((1, d, ff2), lambda i, be, nb: (be[i], 0, 0)),
                  pl.BlockSpec((1, 1, ff2), lambda i, be, nb: (be[i], 0, 0)),
                  pl.BlockSpec((1, ff, d), lambda i, be, nb: (be[i], 0, 0)),
                  pl.BlockSpec((1, 1, d), lambda i, be, nb: (be[i], 0, 0))],
        out_specs=pl.BlockSpec((bm, d), lambda i, be, nb: (i, 0)),
        scratch_shapes=[pltpu.VMEM((d, ff2), BF16), pltpu.VMEM((ff, d), BF16)],
    )
    return pl.pallas_call(
        functools.partial(_expert_kernel, ff=ff),
        grid_spec=grid_spec,
        out_shape=jax.ShapeDtypeStruct((n_rows, d), F32),
        compiler_params=_cparams(("arbitrary",)),
        name="moe_experts",
    )(block_e, n_used, xr, w_gu, b_gu.reshape(ne, 1, ff2), w_down, b_down.reshape(ne, 1, d))


def _combine_kernel(dest_ref, dest_next_ref, yr_hbm, g_ref, x1_ref, gate2_ref, gf_ref, out_ref, buf, sems,
                    *, tile, final_norm):
    i = pl.program_id(0)
    slot = i % 2
    n_copy = tile * TOP_K

    def gather(idx_ref, s):
        def start(t, c):
            for k in range(TOP_K):
                pltpu.make_async_copy(yr_hbm.at[pl.ds(idx_ref[t * TOP_K + k], 1)], buf.at[s, k, pl.ds(t, 1)],
                                      sems.at[s]).start()
            return c

        lax.fori_loop(0, tile, start, 0)

    @pl.when(i == 0)
    def _():
        gather(dest_ref, slot)

    @pl.when(i + 1 < pl.num_programs(0))
    def _():
        gather(dest_next_ref, 1 - slot)

    _wait_rows(yr_hbm, n_copy, sems.at[slot])

    y = None
    for k in range(TOP_K):
        term = g_ref[:, k:k + 1] * buf[slot, k]
        y = term if y is None else y + term
    x2 = x1_ref[...] + gate2_ref[...] * y
    if final_norm:
        ms = jnp.mean(x2 * x2, axis=-1, keepdims=True)
        x2 = x2 * lax.rsqrt(ms + RMS_EPS) * gf_ref[...]
    out_ref[...] = x2


def combine(yr, dest_flat, topg, x1, gate2, gf, tile, final_norm):
    n, d = x1.shape
    tok = pl.BlockSpec((tile, d), lambda i: (i, 0))
    steps = n // tile
    return pl.pallas_call(
        functools.partial(_combine_kernel, tile=tile, final_norm=final_norm),
        grid=(steps,),
        in_specs=[pl.BlockSpec((tile * TOP_K,), lambda i: (i,), memory_space=pltpu.SMEM),
                  pl.BlockSpec((tile * TOP_K,), lambda i: (jnp.minimum(i + 1, steps - 1),), memory_space=pltpu.SMEM),
                  pl.BlockSpec(memory_space=pl.ANY),
                  pl.BlockSpec((tile, topg.shape[1]), lambda i: (i, 0)),
                  tok, _row_spec(gate2.shape[0], tile, d), pl.BlockSpec((1, d), lambda i: (0, 0))],
        out_specs=tok,
        out_shape=jax.ShapeDtypeStruct(x1.shape, F32),
        scratch_shapes=[pltpu.VMEM((2, TOP_K, tile, d), F32), pltpu.SemaphoreType.DMA((2,))],
        compiler_params=_cparams(("arbitrary",)),
        name="moe_combine",
    )(dest_flat, dest_flat, yr, topg, x1, gate2, gf)


def moe_block(groups, norm_f_g, final_norm, w_gu, b_gu, w_down, b_down, bm, tile):
    d = groups[0][0].shape[1]
    ne = w_gu.shape[0]
    experts_iota = jnp.arange(ne, dtype=jnp.int32)
    counts = [grp[5][0, :ne].astype(jnp.int32) for grp in groups]
    total = sum(counts)
    padded = ((total + bm - 1) // bm) * bm
    pend = jnp.cumsum(padded)
    base = pend - padded
    dests = []
    for grp, cnt in zip(groups, counts):
        idx = grp[2][:, :TOP_K].astype(jnp.int32)
        onehot = (idx[..., None] == experts_iota).astype(jnp.int32)
        dests.append((jnp.sum(onehot * base, axis=-1) + grp[4][:, :TOP_K].astype(jnp.int32)).reshape(-1))
        base = base + cnt
    n_assign = sum(grp[0].shape[0] for grp in groups) * TOP_K
    n_rows = -(-n_assign // bm) * bm + ne * bm
    n_blocks = n_rows // bm
    starts = jnp.arange(n_blocks, dtype=jnp.int32) * bm
    block_e = jnp.minimum(jnp.sum((starts[:, None] >= pend[None, :]).astype(jnp.int32), axis=1), ne - 1)
    n_used = (pend[-1:] // bm).astype(jnp.int32)
    last_e = block_e[jnp.maximum(n_used[0] - 1, 0)]
    block_e = jnp.where(jnp.arange(n_blocks) < n_used[0], block_e, last_e).astype(jnp.int32)

    xr = jnp.zeros((n_rows, d), F32)
    for grp, dest in zip(groups, dests):
        xr = dispatch(grp[1], dest, xr, tile)
    yr = experts(xr, block_e, n_used, w_gu, b_gu, w_down, b_down, bm)
    return [combine(yr, dest, grp[3], grp[0], grp[6], norm_f_g.reshape(1, d), tile, final_norm)
            for grp, dest in zip(groups, dests)]


def _group_forward(x, mod, attend, h0_lanes, t_per_seq, lw, mats, rows):
    n, d = x.shape
    rows = min(rows, n)
    (norm1_g, norm2_g, w_in_b, w_out_b, w_glu_b, b_glu, d_vec, wr_hi, wr_lo, br) = lw
    kmat, smat, rmat, decay_re, decay_im = mats
    shift1, scale1, gate1, shift2, scale2, gate2 = [mod[:, j * d:(j + 1) * d] for j in range(6)]
    aw = (w_in_b.shape[1] - w_glu_b.shape[0]) // 3
    q_dtype = BF16 if h0_lanes is None else F32
    q, k, v, kb, vb, u = inproj(x, shift1, scale1, norm1_g, w_in_b, aw, q_dtype, rows)
    o_attn = attend(q, k, v, kb, vb)

    sw = u.shape[1]
    nb = sw // LANES
    u2 = u.reshape(n // SSM_CHUNK, SSM_CHUNK * sw)
    nc = u2.shape[0]
    crow = min(256, nc)
    if h0_lanes is None:
        s = ssm_state(u2, smat, crow)
        hs, hend = ssm_scan(s, decay_re, decay_im, crow, nb)
    else:
        assert t_per_seq == SSM_CHUNK
        hs = h0_lanes
        hend = ssm_state(u2, smat, crow, h0_lanes, decay_re, decay_im)
    y_ssm = ssm_out(u2, hs, kmat, rmat, d_vec, w_glu_b, b_glu, crow).reshape(n, sw)

    routed = postmix(x, o_attn, y_ssm, w_out_b, gate1, norm2_g, scale2, shift2, wr_hi, wr_lo, br, rows)
    return tuple(routed) + (gate2,), k, v, hend


def kernel(x_prompt, x_sample, c_prompt, c_sample, cache_k, cache_v, state_ssm_re, state_ssm_im, page_table, norm1_g, norm2_g, w_ada, b_ada, w_in, w_out, sb_bias, lam_re, lam_im, log_dt, ssm_b_re, ssm_b_im, ssm_c_re, ssm_c_im, ssm_d, w_glu, b_glu, w_router, b_router, w_gu, b_gu, w_down, b_down, norm_f_g):
    depth = w_in.shape[0]
    bp, tp, d = x_prompt.shape
    bs, ts, _ = x_sample.shape
    assert bp == 1, "the prompt group is handled as one long sequence"
    n_heads = cache_k.shape[3]
    aw = n_heads * HEAD_DIM
    g, p = lam_re.shape[1:]
    nb = g // GROUP_BLOCK
    ne = w_router.shape[-1]
    n_pool, page = cache_k.shape[1:3]
    rows = 512

    xp = x_prompt.reshape(bp * tp, d)
    xs = x_sample.reshape(bs * ts, d)
    n_c = bp + bs
    c_all = jnp.concatenate([c_prompt, c_sample, jnp.zeros((-n_c % SUBLANES, d), F32)], axis=0)

    outs = {name: [] for name in ("kp", "vp", "rp", "ip", "ks", "vs", "rs", "is")}
    for l in range(depth):
        mod = ada_mod(c_all, w_ada[l], b_ada[l])
        mod_p = mod[:bp]
        mod_s = jnp.repeat(mod[bp:n_c], ts, axis=0)
        mats = _ssm_mats(lam_re[l], lam_im[l], log_dt[l], ssm_b_re[l], ssm_b_im[l], ssm_c_re[l], ssm_c_im[l])
        wr = jnp.pad(w_router[l], ((0, 0), (0, ROUTER_PAD - ne)))
        wr_hi = wr.astype(BF16)
        wr_lo = (wr - wr_hi.astype(F32)).astype(BF16)
        br = jnp.concatenate([b_router[l], jnp.full((ROUTER_PAD - ne,), NEG_BIG, F32)]).reshape(1, ROUTER_PAD)
        lw = (norm1_g[l].reshape(1, d), norm2_g[l].reshape(1, d), w_in[l].astype(BF16), w_out[l].astype(BF16),
              w_glu[l].astype(BF16), b_glu[l].reshape(1, -1), ssm_d[l].reshape(1, -1), wr_hi, wr_lo, br)
        bias = sb_bias[l]

        def attend_p(q, k, v, kb, vb):
            return attn_prompt_pairs(q, kb, vb, bias, nsb=min(8, tp // KEY_BLOCK))

        ck = cache_k[l].transpose(0, 2, 3, 1).reshape(n_pool, aw, page)
        cv = cache_v[l].transpose(0, 2, 3, 1).reshape(n_pool, aw, page)

        def attend_s(q, k, v, kb, vb):
            return attn_sample(q, k, v, ck, cv, page_table, bias, ts, group=math.gcd(16, page_table.shape[1]))

        h0 = _state_to_lanes(state_ssm_re[l], state_ssm_im[l], nb)
        routed_p, kp, vp, hp = _group_forward(xp, mod_p, attend_p, None, tp, lw, mats, rows)
        routed_s, ks, vs, hs = _group_forward(xs, mod_s, attend_s, h0, ts, lw, mats, rows)
        xp, xs = moe_block([routed_p, routed_s], norm_f_g, l == depth - 1, w_gu[l], b_gu[l], w_down[l], b_down[l],
                           bm=EXPERT_ROWS, tile=512)
        rp, ip = _lanes_to_state(hp, g, p)
        rs, is_ = _lanes_to_state(hs, g, p)
        outs["kp"].append(kp.reshape(bp, tp, n_heads, HEAD_DIM))
        outs["vp"].append(vp.reshape(bp, tp, n_heads, HEAD_DIM))
        outs["rp"].append(rp)
        outs["ip"].append(ip)
        outs["ks"].append(ks.reshape(bs, ts, n_heads, HEAD_DIM))
        outs["vs"].append(vs.reshape(bs, ts, n_heads, HEAD_DIM))
        outs["rs"].append(rs)
        outs["is"].append(is_)
    st = lambda name: jnp.stack(outs[name])
    return (xp.reshape(bp, tp, d), xs.reshape(bs, ts, d), st("kp"), st("vp"), st("rp"), st("ip"),
            st("ks"), st("vs"), st("rs"), st("is"))
```

```python
import functools
import math

import jax
import jax.numpy as jnp
from jax import lax
from jax.experimental import pallas as pl
from jax.experimental.pallas import tpu as pltpu

F32 = jnp.float32
BF16 = jnp.bfloat16

HEAD_DIM = 64
SSM_GROUP = 16
TOP_K = 4
SWIGLU_LIMIT = 7.0
SWIGLU_ALPHA = 1.702
RMS_EPS = 1e-6

LANES = 128
SUBLANES = 8
VMEM_LIMIT_BYTES = 56 * 1024 * 1024

KEY_BLOCK = LANES
SSM_CHUNK = 8
GROUP_BLOCK = LANES // SSM_GROUP
ROUTER_PAD = LANES
EXPERT_ROWS = 512
NEG_BIG = -1e30
EXP_CAP = 1e30
LOG2E = math.log2(math.e)
PAGE_SLOTS = 3

def _cparams(semantics):
    return pltpu.CompilerParams(dimension_semantics=semantics, vmem_limit_bytes=VMEM_LIMIT_BYTES)


def _dot(a, b):
    return jnp.dot(a, b, preferred_element_type=F32)


def _dot_nt(a, b):
    return lax.dot_general(a, b, (((1,), (1,)), ((), ())), preferred_element_type=F32)


def _split_bf16(x):
    hi = x.astype(BF16)
    lo = (x - hi.astype(F32)).astype(BF16)
    return hi, lo


def _ada_kernel(c_ref, w_ref, b_ref, o_ref):
    o_ref[...] = _dot(c_ref[...].astype(BF16), w_ref[...].astype(BF16)) + b_ref[...]


def ada_mod(c, w_ada, b_ada):
    n, d = c.shape
    n_out = w_ada.shape[1]
    return pl.pallas_call(
        _ada_kernel,
        grid=(n_out // d,),
        in_specs=[pl.BlockSpec((n, d), lambda j: (0, 0)),
                  pl.BlockSpec((d, d), lambda j: (0, j)),
                  pl.BlockSpec((1, d), lambda j: (0, j))],
        out_specs=pl.BlockSpec((n, d), lambda j: (0, j)),
        out_shape=jax.ShapeDtypeStruct((n, n_out), F32),
        compiler_params=_cparams(("arbitrary",)),
        name="ada_mod",
    )(c, w_ada, b_ada.reshape(1, n_out))


def _inproj_kernel(x_ref, shift_ref, scale_ref, g_ref, w_ref, q_ref, k_ref, v_ref, kb_ref, vb_ref, u_ref, *, aw):
    x = x_ref[...]
    ms = jnp.mean(x * x, axis=-1, keepdims=True)
    hn = x * lax.rsqrt(ms + RMS_EPS) * g_ref[...]
    hn = hn * (1.0 + scale_ref[...]) + shift_ref[...]
    proj = _dot(hn.astype(BF16), w_ref[...])
    q_ref[...] = (proj[:, :aw] * (HEAD_DIM ** -0.5 * LOG2E)).astype(q_ref.dtype)
    k = proj[:, aw:2 * aw]
    v = proj[:, 2 * aw:3 * aw]
    k_ref[...] = k
    v_ref[...] = v
    kb_ref[...] = k.astype(BF16)
    vb_ref[...] = v.astype(BF16)
    u_ref[...] = proj[:, 3 * aw:]


def _row_spec(n_mod_rows, rows, d):
    if n_mod_rows == 1:
        return pl.BlockSpec((1, d), lambda i: (0, 0))
    return pl.BlockSpec((rows, d), lambda i: (i, 0))


def inproj(x, shift, scale, g, w_in_b, aw, q_dtype, rows):
    n, d = x.shape
    pw = w_in_b.shape[1]
    sw = pw - 3 * aw
    tok = lambda width: pl.BlockSpec((rows, width), lambda i: (i, 0))
    return pl.pallas_call(
        functools.partial(_inproj_kernel, aw=aw),
        grid=(n // rows,),
        in_specs=[tok(d), _row_spec(shift.shape[0], rows, d), _row_spec(scale.shape[0], rows, d),
                  pl.BlockSpec((1, d), lambda i: (0, 0)),
                  pl.BlockSpec((d, pw), lambda i: (0, 0))],
        out_specs=[tok(aw), tok(aw), tok(aw), tok(aw), tok(aw), tok(sw)],
        out_shape=[jax.ShapeDtypeStruct((n, aw), q_dtype),
                   jax.ShapeDtypeStruct((n, aw), F32), jax.ShapeDtypeStruct((n, aw), F32),
                   jax.ShapeDtypeStruct((n, aw), BF16), jax.ShapeDtypeStruct((n, aw), BF16),
                   jax.ShapeDtypeStruct((n, sw), F32)],
        compiler_params=_cparams(("arbitrary",)),
        name="inproj",
    )(x, shift, scale, g, w_in_b)


def _tri_ones():
    j = lax.broadcasted_iota(jnp.int32, (KEY_BLOCK, 2 * KEY_BLOCK), 0)
    s = lax.broadcasted_iota(jnp.int32, (KEY_BLOCK, 2 * KEY_BLOCK), 1)
    return jnp.where((j > s) | (s >= KEY_BLOCK), 1.0, 0.0).astype(BF16)


def _sb_drop(z, mask):
    drop = jnp.maximum(jnp.log(1.0 + jnp.minimum(jnp.exp2(z), EXP_CAP)) * LOG2E, z)
    if mask is not None:
        drop = jnp.where(mask, drop, 0.0)
    return drop


def _sb_logits(z, mask):
    drop = _sb_drop(z, mask)
    return z - drop, drop.astype(BF16)


def _weights_from_log2(x):
    return jnp.exp2(x).astype(BF16)


def _sb_finish(log_beta, drop_b, carry, tri, mask):
    groups = drop_b.shape[1] // KEY_BLOCK
    later, total = [], []
    for g in range(groups):
        cs = _dot(drop_b[:, g * KEY_BLOCK:(g + 1) * KEY_BLOCK], tri)
        later.append(cs[:, :KEY_BLOCK])
        total.append(cs[:, KEY_BLOCK:])
    later = jnp.concatenate(later, axis=1) if groups > 1 else later[0]
    total = jnp.concatenate(total, axis=1) if groups > 1 else total[0]
    a = _weights_from_log2(log_beta - (later + carry))
    if mask is not None:
        a = jnp.where(mask, a, jnp.zeros_like(a))
    return a, total


def _sb_weights(z, carry, tri, mask):
    if mask is not None:
        mask = jnp.concatenate([mask] * (z.shape[1] // KEY_BLOCK), axis=1)
    log_beta, drop_b = _sb_logits(z, mask)
    return _sb_finish(log_beta, drop_b, carry, tri, mask)


def _tri_suffix(width):
    j = lax.broadcasted_iota(jnp.int32, (width, width), 0)
    s = lax.broadcasted_iota(jnp.int32, (width, width), 1)
    return jnp.where(j >= s, 1.0, 0.0).astype(BF16)


def _attn_pairs_kernel(kbias_ref, q_ref, k_ref, v_ref, tri_ref, o_ref, acc_ref, carry_ref, lb_scr, drop_scr,
                       qext_scr, *, nsb):
    it = pl.program_id(1)
    unit = 2 * KEY_BLOCK
    tq = nsb * KEY_BLOCK
    n_units = nsb // 2
    tri = tri_ref[...]
    ones_lanes = lax.broadcasted_iota(jnp.int32, (tq, LANES), 1) < 2
    qext_scr[:, :LANES] = q_ref[...]
    qext_scr[:, LANES:] = jnp.where(ones_lanes, 1.0, 0.0).astype(BF16)
    kbias = kbias_ref[0]
    acc_ref[...] = jnp.zeros_like(acc_ref)
    carry_ref[...] = jnp.zeros_like(carry_ref)
    lane = lax.broadcasted_iota(jnp.int32, (unit, LANES), 1)
    first_head = lane < HEAD_DIM

    def stacked(ref, u):
        st = pl.multiple_of(u * unit, unit)
        slab = ref[pl.ds(st, unit), :]
        zero = jnp.zeros_like(slab)
        return jnp.concatenate([jnp.where(first_head, slab, zero), jnp.where(first_head, zero, slab)], axis=0)

    def logits(r0, u, mask):
        z = _dot_nt(qext_scr[r0:, :], jnp.concatenate([stacked(k_ref, u), kbias], axis=1))
        return z, _sb_drop(z, mask).astype(BF16)

    def finish(r0, u, z, drop_b, mask):
        carry = carry_ref[r0:, :]
        m = z.shape[0]
        parts, new_carry = [], []
        for h in range(2):
            incl = _dot(drop_b[:, h * unit:(h + 1) * unit], tri)
            c_h = carry[:, h * KEY_BLOCK:(h + 1) * KEY_BLOCK]
            parts.append(incl + jnp.concatenate([c_h, c_h], axis=1))
            new_carry.append(c_h + jnp.broadcast_to(incl[:, 0:1], (m, KEY_BLOCK)))
        a = jnp.exp2((z - jnp.concatenate(parts, axis=1)).astype(BF16))
        if mask is not None:
            a = jnp.where(mask, a, jnp.zeros_like(a))
        acc_ref[r0:, :] += _dot(a, stacked(v_ref, u))
        carry_ref[r0:, :] = jnp.concatenate(new_carry, axis=1)

    for c in reversed(range(n_units)):
        r0 = c * unit
        rr = lax.broadcasted_iota(jnp.int32, (tq - r0, unit), 0)
        ll = lax.broadcasted_iota(jnp.int32, (tq - r0, unit), 1)
        mask = jnp.concatenate([ll < rr] * 2, axis=1)
        z, drop_b = logits(r0, it * n_units + c, mask)
        finish(r0, it * n_units + c, z, drop_b, mask)

    n_full = it * n_units

    def first_half(u, slot):
        z, drop_b = logits(0, u, None)
        lb_scr[slot] = z
        drop_scr[slot] = drop_b

    def second_half(u, slot):
        finish(0, u, lb_scr[slot], drop_scr[slot], None)

    @pl.when(n_full > 0)
    def _():
        per_trip = math.gcd(n_units, 4)
        first_half(n_full - 1, 0)

        def body(jj, c):
            u0 = n_full - 1 - per_trip * jj
            for i in range(per_trip):
                first_half(u0 - i - 1, (i + 1) % 2)
                second_half(u0 - i, i % 2)
            return c

        lax.fori_loop(0, n_full // per_trip - 1, body, 0)
        for i in range(per_trip - 1):
            first_half(per_trip - 2 - i, (i + 1) % 2)
            second_half(per_trip - 1 - i, i % 2)
        second_half(0, (per_trip - 1) % 2)

    o_ref[...] = acc_ref[...].astype(o_ref.dtype)


def attn_prompt_pairs(q, kb, vb, bias, nsb):
    n, aw = q.shape
    tq = nsb * KEY_BLOCK
    unit = 2 * KEY_BLOCK
    n_pairs = aw // LANES
    assert nsb % 4 == 0 and n % tq == 0
    b_hi, b_lo = _split_bf16(bias.astype(F32) * LOG2E)
    lane = jnp.arange(LANES)
    kbias = jnp.where(lane == 0, b_hi[:, None, None], jnp.where(lane == 1, b_lo[:, None, None], 0)).astype(BF16)
    kbias = jnp.broadcast_to(kbias, (bias.shape[0], unit, LANES)).reshape(n_pairs, 2 * unit, LANES)
    tri = _tri_suffix(unit)
    return pl.pallas_call(
        functools.partial(_attn_pairs_kernel, nsb=nsb),
        grid=(n_pairs, n // tq),
        in_specs=[pl.BlockSpec((1, 2 * unit, LANES), lambda hp, i: (hp, 0, 0)),
                  pl.BlockSpec((tq, LANES), lambda hp, i: (i, hp)),
                  pl.BlockSpec((n, LANES), lambda hp, i: (0, hp)),
                  pl.BlockSpec((n, LANES), lambda hp, i: (0, hp)),
                  pl.BlockSpec(tri.shape, lambda hp, i: (0, 0))],
        out_specs=pl.BlockSpec((tq, LANES), lambda hp, i: (i, hp)),
        out_shape=jax.ShapeDtypeStruct((n, aw), BF16),
        scratch_shapes=[pltpu.VMEM((tq, LANES), F32), pltpu.VMEM((tq, 2 * KEY_BLOCK), F32),
                        pltpu.VMEM((2, tq, 2 * unit), F32), pltpu.VMEM((2, tq, 2 * unit), BF16),
                        pltpu.VMEM((tq, 2 * LANES), BF16)],
        compiler_params=_cparams(("arbitrary", "arbitrary")),
        name="attn_prompt",
    )(kbias, q, kb, vb, tri)


def _attn_sample_kernel(pt_ref, q_ref, kn_ref, vn_ref, ck_hbm, cv_hbm, tri_ref, bias_ref, o_ref,
                        kbuf, vbuf, sems, qbd_ref, acc_ref, carry_ref, *, group, n_groups, n_heads, t_new):
    b = pl.program_id(0)
    total = pl.num_programs(0) * n_groups
    m = n_heads * t_new
    aw = n_heads * HEAD_DIM
    tri = tri_ref[...]
    bias = bias_ref[...]

    def request(s):
        slot = s % PAGE_SLOTS
        for i in range(group):
            page = pt_ref[s * group + i]
            pltpu.make_async_copy(ck_hbm.at[page], kbuf.at[slot, i], sems.at[slot, 0]).start()
            pltpu.make_async_copy(cv_hbm.at[page], vbuf.at[slot, i], sems.at[slot, 1]).start()

    def wait_group(slot):
        pltpu.make_async_copy(kbuf.at[slot], kbuf.at[slot], sems.at[slot, 0]).wait()
        pltpu.make_async_copy(vbuf.at[slot], vbuf.at[slot], sems.at[slot, 1]).wait()

    @pl.when(b == 0)
    def _():
        for s in range(PAGE_SLOTS - 1):
            request(s)

    q = q_ref[...].astype(BF16)
    qt = jnp.concatenate([q] * n_heads, axis=0)
    rw = lax.broadcasted_iota(jnp.int32, (m, aw), 0)
    ln = lax.broadcasted_iota(jnp.int32, (m, aw), 1)
    qbd_ref[...] = jnp.where(ln // HEAD_DIM == rw // t_new, qt, jnp.zeros_like(qt))
    pad = jnp.zeros((KEY_BLOCK - t_new, aw), F32)
    kn = jnp.concatenate([kn_ref[...], pad], axis=0).astype(BF16)
    vn = jnp.concatenate([vn_ref[...], pad], axis=0).astype(BF16)
    r2 = lax.broadcasted_iota(jnp.int32, (m, KEY_BLOCK), 0)
    l2 = lax.broadcasted_iota(jnp.int32, (m, KEY_BLOCK), 1)
    z = _dot_nt(qbd_ref[...], kn) + bias
    a, tot = _sb_weights(z, 0.0, tri, l2 < (r2 % t_new))
    acc_ref[...] = _dot(a, vn)
    carry_ref[...] = tot

    def body(g, c):
        s = b * n_groups + g

        @pl.when(s + PAGE_SLOTS - 1 < total)
        def _():
            request(s + PAGE_SLOTS - 1)

        slot = s % PAGE_SLOTS
        wait_group(slot)
        kt = jnp.concatenate([kbuf[slot, i].astype(BF16) for i in range(group)], axis=1)
        vt = jnp.concatenate([vbuf[slot, i].astype(BF16) for i in range(group)], axis=1)
        z = _dot(qbd_ref[...], kt) + jnp.concatenate([bias] * group, axis=1)
        log_beta, drop_b = _sb_logits(z, None)
        carry = carry_ref[...]
        shift = []
        for i in range(group):
            cs = _dot(drop_b[:, i * KEY_BLOCK:(i + 1) * KEY_BLOCK], tri)
            shift.append(cs[:, :KEY_BLOCK] + carry)
            carry = carry + cs[:, KEY_BLOCK:]
        a = _weights_from_log2(log_beta - jnp.concatenate(shift, axis=1))
        acc_ref[...] += _dot_nt(a, vt)
        carry_ref[...] = carry
        return c

    lax.fori_loop(0, n_groups, body, 0)

    acc = acc_ref[...]
    lo = lax.broadcasted_iota(jnp.int32, (t_new, aw), 1)
    o = jnp.zeros((t_new, aw), F32)
    for h in range(n_heads):
        o = o + jnp.where(lo // HEAD_DIM == h, acc[h * t_new:(h + 1) * t_new, :], 0.0)
    o_ref[...] = o.astype(o_ref.dtype)


def attn_sample(q, k_new, v_new, cache_k, cache_v, page_table, bias, t_new, group):
    n, aw = q.shape
    bsz, n_pages = page_table.shape
    n_heads = aw // HEAD_DIM
    page = cache_k.shape[2]
    n_groups = n_pages // group
    assert page == KEY_BLOCK and t_new == SUBLANES and n_pages % group == 0 and bsz * n_groups >= PAGE_SLOTS
    m = n_heads * t_new
    bias_rows = jnp.broadcast_to(jnp.repeat(bias * LOG2E, t_new)[:, None], (m, KEY_BLOCK)).astype(F32)
    pages_recent_first = page_table[:, ::-1].reshape(-1)
    tok = pl.BlockSpec((t_new, aw), lambda b, pt: (b, 0))
    ring = pltpu.VMEM((PAGE_SLOTS, group, aw, page), F32)
    grid_spec = pltpu.PrefetchScalarGridSpec(
        num_scalar_prefetch=1,
        grid=(bsz,),
        in_specs=[tok, tok, tok, pl.BlockSpec(memory_space=pl.ANY), pl.BlockSpec(memory_space=pl.ANY),
                  pl.BlockSpec((KEY_BLOCK, 2 * KEY_BLOCK), lambda b, pt: (0, 0)),
                  pl.BlockSpec((m, KEY_BLOCK), lambda b, pt: (0, 0))],
        out_specs=tok,
        scratch_shapes=[ring, ring, pltpu.SemaphoreType.DMA((PAGE_SLOTS, 2)),
                        pltpu.VMEM((m, aw), BF16), pltpu.VMEM((m, aw), F32), pltpu.VMEM((m, KEY_BLOCK), F32)],
    )
    return pl.pallas_call(
        functools.partial(_attn_sample_kernel, group=group, n_groups=n_groups, n_heads=n_heads, t_new=t_new),
        grid_spec=grid_spec,
        out_shape=jax.ShapeDtypeStruct((n, aw), BF16),
        compiler_params=_cparams(("arbitrary",)),
        name="attn_sample",
    )(pages_recent_first, q, k_new, v_new, cache_k, cache_v, _tri_ones(), bias_rows)


def _ssm_mats(lam_re, lam_im, log_dt, b_re, b_im, c_re, c_im):
    g, p = lam_re.shape
    hh = b_re.shape[-1]
    el = SSM_CHUNK
    nb = g // GROUP_BLOCK
    dt = jnp.exp(log_dt)[:, None]
    ar, ai = lam_re * dt, lam_im * dt
    lbr, lbi = jnp.exp(ar) * jnp.cos(ai), jnp.exp(ar) * jnp.sin(ai)
    den = lam_re * lam_re + lam_im * lam_im
    fr = ((lbr - 1.0) * lam_re + lbi * lam_im) / den
    fi = (lbi * lam_re - (lbr - 1.0) * lam_im) / den
    bbr = fr[..., None] * b_re - fi[..., None] * b_im
    bbi = fr[..., None] * b_im + fi[..., None] * b_re
    n = jnp.arange(el + 1, dtype=F32)[:, None, None]
    pr = jnp.exp(ar[None] * n) * jnp.cos(ai[None] * n)
    pi = jnp.exp(ar[None] * n) * jnp.sin(ai[None] * n)
    eye = jnp.eye(GROUP_BLOCK, dtype=F32)

    cpr = c_re[None] * pr[:el, :, None, :] - c_im[None] * pi[:el, :, None, :]
    cpi = c_re[None] * pi[:el, :, None, :] + c_im[None] * pr[:el, :, None, :]
    kd = jnp.einsum('dgip,gpj->dgij', cpr, bbr) - jnp.einsum('dgip,gpj->dgij', cpi, bbi)
    kmat = jnp.einsum('dkgij,gh->dkgjhi', kd.reshape(el, nb, GROUP_BLOCK, hh, hh), eye)
    kmat = kmat.reshape(el, nb, LANES, LANES)

    rev = pr[el - 1 - jnp.arange(el)], pi[el - 1 - jnp.arange(el)]
    scr = rev[0][..., None] * bbr[None] - rev[1][..., None] * bbi[None]
    sci = rev[0][..., None] * bbi[None] + rev[1][..., None] * bbr[None]

    def blk_s(a):
        a = jnp.einsum('skgpj,gh->skgjhp', a.reshape(el, nb, GROUP_BLOCK, p, hh), eye)
        return a.reshape(el, nb, LANES, GROUP_BLOCK * p)

    smat = jnp.concatenate([blk_s(scr), blk_s(sci)], axis=-1)

    c1r = c_re[None] * pr[1:, :, None, :] - c_im[None] * pi[1:, :, None, :]
    c1i = c_re[None] * pi[1:, :, None, :] + c_im[None] * pr[1:, :, None, :]

    def blk_r(a):
        a = jnp.einsum('tkgip,gh->tkgphi', a.reshape(el, nb, GROUP_BLOCK, hh, p), eye)
        return a.reshape(el, nb, GROUP_BLOCK * p, LANES)

    rmat = jnp.concatenate([blk_r(c1r), -blk_r(c1i)], axis=-2)

    decay_re = pr[el].reshape(1, g * p)
    decay_im = pi[el].reshape(1, g * p)
    return kmat.astype(BF16), smat.astype(BF16), rmat.astype(BF16), decay_re, decay_im


def _state_to_lanes(h_re, h_im, nb):
    b = h_re.shape[0]
    return jnp.stack([h_re.reshape(b, nb, -1), h_im.reshape(b, nb, -1)], axis=2).reshape(b, -1)


def _lanes_to_state(h, g, p):
    b = h.shape[0]
    nb = g // GROUP_BLOCK
    h = h.reshape(b, nb, 2, GROUP_BLOCK, p)
    return h[:, :, 0].reshape(b, g, p), h[:, :, 1].reshape(b, g, p)


def _ssm_state_kernel(*refs, has_h0, nb, sw, half):
    if has_h0:
        u_ref, smat_ref, h0_ref, dre_ref, dim_ref, s_ref = refs
    else:
        u_ref, smat_ref, s_ref = refs
    for k in range(nb):
        acc = None
        for s in range(SSM_CHUNK):
            ub = u_ref[:, s * sw + k * LANES: s * sw + (k + 1) * LANES].astype(BF16)
            d = _dot(ub, smat_ref[s, k])
            acc = d if acc is None else acc + d
        base = 2 * half * k
        if has_h0:
            hr = h0_ref[:, base:base + half]
            hi = h0_ref[:, base + half:base + 2 * half]
            dr = dre_ref[:, k * half:(k + 1) * half]
            di = dim_ref[:, k * half:(k + 1) * half]
            s_ref[:, base:base + half] = acc[:, :half] + dr * hr - di * hi
            s_ref[:, base + half:base + 2 * half] = acc[:, half:] + dr * hi + di * hr
        else:
            s_ref[:, base:base + 2 * half] = acc


def ssm_state(u2, smat, rows, h0=None, decay_re=None, decay_im=None):
    nc, width = u2.shape
    sw = width // SSM_CHUNK
    nb = sw // LANES
    half = smat.shape[-1] // 2
    sl = nb * 2 * half
    has_h0 = h0 is not None
    row = lambda w: pl.BlockSpec((rows, w), lambda i: (i, 0))
    in_specs = [row(width), pl.BlockSpec(smat.shape, lambda i: (0, 0, 0, 0))]
    args = [u2, smat]
    if has_h0:
        in_specs += [row(sl), pl.BlockSpec((1, nb * half), lambda i: (0, 0)),
                     pl.BlockSpec((1, nb * half), lambda i: (0, 0))]
        args += [h0, decay_re, decay_im]
    return pl.pallas_call(
        functools.partial(_ssm_state_kernel, has_h0=has_h0, nb=nb, sw=sw, half=half),
        grid=(nc // rows,),
        in_specs=in_specs,
        out_specs=row(sl),
        out_shape=jax.ShapeDtypeStruct((nc, sl), F32),
        compiler_params=_cparams(("arbitrary",)),
        name="ssm_state",
    )(*args)


def _ssm_scan_kernel(s_ref, dre_ref, dim_ref, hs_ref, hend_ref, h_scr, *, nb, half):
    @pl.when(pl.program_id(0) == 0)
    def _():
        h_scr[...] = jnp.zeros_like(h_scr)

    dr = dre_ref[...]
    di = dim_ref[...]

    def body(r, h):
        hs_ref[pl.ds(r, 1), :] = h
        s = s_ref[pl.ds(r, 1), :]
        parts = []
        for k in range(nb):
            base = 2 * half * k
            hr, hi = h[:, base:base + half], h[:, base + half:base + 2 * half]
            ar, ai = dr[:, k * half:(k + 1) * half], di[:, k * half:(k + 1) * half]
            parts.append(ar * hr - ai * hi + s[:, base:base + half])
            parts.append(ar * hi + ai * hr + s[:, base + half:base + 2 * half])
        return jnp.concatenate(parts, axis=1)

    h = lax.fori_loop(0, s_ref.shape[0], body, h_scr[...])
    h_scr[...] = h
    hend_ref[...] = h


def ssm_scan(s, decay_re, decay_im, rows, nb):
    nc, sl = s.shape
    n_half_total = decay_re.shape[1]
    half = n_half_total // nb
    row = pl.BlockSpec((rows, sl), lambda i: (i, 0))
    vec = pl.BlockSpec((1, n_half_total), lambda i: (0, 0))
    return pl.pallas_call(
        functools.partial(_ssm_scan_kernel, nb=nb, half=half),
        grid=(nc // rows,),
        in_specs=[row, vec, vec],
        out_specs=[row, pl.BlockSpec((1, sl), lambda i: (0, 0))],
        out_shape=[jax.ShapeDtypeStruct((nc, sl), F32), jax.ShapeDtypeStruct((1, sl), F32)],
        scratch_shapes=[pltpu.VMEM((1, sl), F32)],
        compiler_params=_cparams(("arbitrary",)),
        name="ssm_scan",
    )(s, decay_re, decay_im)


def _gelu_exact(y):
    return 0.5 * y * (1.0 + lax.erf(y * (0.5 ** 0.5)))


def _ssm_out_kernel(u_ref, hs_ref, kmat_ref, rmat_ref, d_ref, wglu_ref, bglu_ref, y_ref, *, nb, sw, half):
    hb = [hs_ref[:, 2 * half * k:2 * half * (k + 1)].astype(BF16) for k in range(nb)]
    ub = [[u_ref[:, s * sw + k * LANES: s * sw + (k + 1) * LANES].astype(BF16) for k in range(nb)]
          for s in range(SSM_CHUNK)]
    wglu = wglu_ref[...]
    for t in range(SSM_CHUNK):
        cols = []
        for k in range(nb):
            acc = _dot(hb[k], rmat_ref[t, k])
            for s in range(t + 1):
                acc = acc + _dot(ub[s][k], kmat_ref[t - s, k])
            cols.append(acc)
        y = jnp.concatenate(cols, axis=1) + d_ref[...] * u_ref[:, t * sw:(t + 1) * sw]
        g = _gelu_exact(y)
        gate = jax.nn.sigmoid(_dot(g.astype(BF16), wglu) + bglu_ref[...])
        y_ref[:, t * sw:(t + 1) * sw] = (g * gate).astype(y_ref.dtype)


def ssm_out(u2, hs, kmat, rmat, d_vec, w_glu_b, b_glu, rows):
    nc, width = u2.shape
    sw = width // SSM_CHUNK
    nb = sw // LANES
    sl = hs.shape[1]
    half = sl // (2 * nb)
    row = lambda w: pl.BlockSpec((rows, w), lambda i: (i, 0))
    const = lambda a: pl.BlockSpec(a.shape, lambda i: (0,) * a.ndim)
    return pl.pallas_call(
        functools.partial(_ssm_out_kernel, nb=nb, sw=sw, half=half),
        grid=(nc // rows,),
        in_specs=[row(width), row(sl), const(kmat), const(rmat), const(d_vec), const(w_glu_b), const(b_glu)],
        out_specs=row(width),
        out_shape=jax.ShapeDtypeStruct((nc, width), BF16),
        compiler_params=_cparams(("arbitrary",)),
        name="ssm_out",
    )(u2, hs, kmat, rmat, d_vec, w_glu_b, b_glu)


def _postmix_kernel(x_ref, o_ref, y_ref, wo_ref, gate1_ref, g2_ref, scale2_ref, shift2_ref, wrh_ref, wrl_ref,
                    br_ref, tril_ref, x1_ref, hn2_ref, topi_ref, topg_ref, topr_ref, cnt_ref, carry_ref, *, aw):
    @pl.when(pl.program_id(0) == 0)
    def _():
        carry_ref[...] = jnp.zeros_like(carry_ref)

    mix = _dot(o_ref[...], wo_ref[:aw, :]) + _dot(y_ref[...], wo_ref[aw:, :])
    x1 = x_ref[...] + gate1_ref[...] * mix
    x1_ref[...] = x1
    ms = jnp.mean(x1 * x1, axis=-1, keepdims=True)
    hn2 = x1 * lax.rsqrt(ms + RMS_EPS) * g2_ref[...]
    hn2 = hn2 * (1.0 + scale2_ref[...]) + shift2_ref[...]
    hn2_ref[...] = hn2

    hh, hl = _split_bf16(hn2)
    wrh = wrh_ref[...]
    logits = _dot(hh, wrh) + _dot(hl, wrh) + _dot(hh, wrl_ref[...]) + br_ref[...]

    rows = logits.shape[0]
    lane = lax.broadcasted_iota(jnp.int32, (rows, ROUTER_PAD), 1)
    work = logits
    sel = jnp.zeros((rows, ROUTER_PAD), F32)
    picks, vals, idxs = [], [], []
    for _ in range(TOP_K):
        m = jnp.max(work, axis=-1, keepdims=True)
        idx = jnp.min(jnp.where(work == m, lane, ROUTER_PAD), axis=-1, keepdims=True)
        pick = lane == idx
        picks.append(pick)
        vals.append(m)
        idxs.append(idx)
        sel = jnp.where(pick, 1.0, sel)
        work = jnp.where(pick, -jnp.inf, work)

    rank = _dot(tril_ref[...], sel.astype(BF16)) + carry_ref[...]
    carry_ref[...] = rank[rows - 1:rows, :] + sel[rows - 1:rows, :]
    cnt_ref[...] = carry_ref[...]

    es = [jnp.exp(v - vals[0]) for v in vals]
    den = es[0]
    for e in es[1:]:
        den = den + e
    topi = jnp.zeros((rows, ROUTER_PAD), F32)
    topg = jnp.zeros((rows, ROUTER_PAD), F32)
    topr = jnp.zeros((rows, ROUTER_PAD), F32)
    for r in range(TOP_K):
        rk = jnp.sum(jnp.where(picks[r], rank, 0.0), axis=-1, keepdims=True)
        topi = jnp.where(lane == r, idxs[r].astype(F32), topi)
        topg = jnp.where(lane == r, es[r] / den, topg)
        topr = jnp.where(lane == r, rk, topr)
    topi_ref[...] = topi
    topg_ref[...] = topg
    topr_ref[...] = topr


def postmix(x, o_attn, y_ssm, w_out_b, gate1, g2, scale2, shift2, wr_hi, wr_lo, br, rows):
    n, d = x.shape
    aw = o_attn.shape[1]
    tok = lambda width: pl.BlockSpec((rows, width), lambda i: (i, 0))
    const = lambda a: pl.BlockSpec(a.shape, lambda i: (0,) * a.ndim)
    r = lax.broadcasted_iota(jnp.int32, (rows, rows), 0)
    c = lax.broadcasted_iota(jnp.int32, (rows, rows), 1)
    tril = jnp.where(c < r, 1.0, 0.0).astype(BF16)
    lanes_out = jax.ShapeDtypeStruct((n, ROUTER_PAD), F32)
    return pl.pallas_call(
        functools.partial(_postmix_kernel, aw=aw),
        grid=(n // rows,),
        in_specs=[tok(d), tok(aw), tok(y_ssm.shape[1]), const(w_out_b),
                  _row_spec(gate1.shape[0], rows, d), const(g2),
                  _row_spec(scale2.shape[0], rows, d), _row_spec(shift2.shape[0], rows, d),
                  const(wr_hi), const(wr_lo), const(br), const(tril)],
        out_specs=[tok(d), tok(d), tok(ROUTER_PAD), tok(ROUTER_PAD), tok(ROUTER_PAD),
                   pl.BlockSpec((1, ROUTER_PAD), lambda i: (0, 0))],
        out_shape=[jax.ShapeDtypeStruct((n, d), F32), jax.ShapeDtypeStruct((n, d), F32),
                   lanes_out, lanes_out, lanes_out, jax.ShapeDtypeStruct((1, ROUTER_PAD), F32)],
        scratch_shapes=[pltpu.VMEM((1, ROUTER_PAD), F32)],
        compiler_params=_cparams(("arbitrary",)),
        name="postmix",
    )(x, o_attn, y_ssm, w_out_b, gate1, g2, scale2, shift2, wr_hi, wr_lo, br, tril)


def _wait_rows(ref, n_copy, sem):
    span = ref.at[pl.ds(0, n_copy)]
    pltpu.make_async_copy(span, span, sem).wait()


def _dispatch_kernel(dest_ref, hn_ref, xr_in, xr_hbm, stage, sems, *, tile):
    del xr_in
    i = pl.program_id(0)
    slot = i % 2
    n_copy = tile * TOP_K
    stage[slot] = hn_ref[...]

    def start(t, c):
        for k in range(TOP_K):
            pltpu.make_async_copy(stage.at[slot, pl.ds(t, 1)], xr_hbm.at[pl.ds(dest_ref[t * TOP_K + k], 1)],
                                  sems.at[slot]).start(priority=k % 2)
        return c

    lax.fori_loop(0, tile, start, 0)

    @pl.when(i > 0)
    def _():
        _wait_rows(xr_hbm, n_copy, sems.at[1 - slot])

    @pl.when(i == pl.num_programs(0) - 1)
    def _():
        _wait_rows(xr_hbm, n_copy, sems.at[slot])


def dispatch(hn, dest_flat, xr0, tile):
    n, d = hn.shape
    return pl.pallas_call(
        functools.partial(_dispatch_kernel, tile=tile),
        grid=(n // tile,),
        in_specs=[pl.BlockSpec((tile * TOP_K,), lambda i: (i,), memory_space=pltpu.SMEM),
                  pl.BlockSpec((tile, d), lambda i: (i, 0)),
                  pl.BlockSpec(memory_space=pl.ANY)],
        out_specs=pl.BlockSpec(memory_space=pl.ANY),
        out_shape=jax.ShapeDtypeStruct(xr0.shape, xr0.dtype),
        scratch_shapes=[pltpu.VMEM((2, tile, d), hn.dtype), pltpu.SemaphoreType.DMA((2,))],
        input_output_aliases={2: 0},
        compiler_params=_cparams(("arbitrary",)),
        name="moe_dispatch",
    )(dest_flat, hn, xr0)


def _expert_kernel(be_ref, nb_ref, x_ref, wgu_ref, bgu_ref, wd_ref, bd_ref, y_ref, wgu_b, wd_b, *, ff):
    i = pl.program_id(0)

    @pl.when(i < nb_ref[0])
    def _():
        prev = be_ref[jnp.maximum(i - 1, 0)]

        @pl.when((i == 0) | (be_ref[i] != prev))
        def _():
            wgu_b[...] = wgu_ref[0].astype(BF16)
            wd_b[...] = wd_ref[0].astype(BF16)

        gu = _dot(x_ref[...].astype(BF16), wgu_b[...]) + bgu_ref[0]
        gate = jnp.minimum(gu[:, :ff], SWIGLU_LIMIT)
        up = jnp.clip(gu[:, ff:], -SWIGLU_LIMIT, SWIGLU_LIMIT)
        act = (up + 1.0) * (gate * jax.nn.sigmoid(SWIGLU_ALPHA * gate))
        y_ref[...] = _dot(act.astype(BF16), wd_b[...]) + bd_ref[0]

    @pl.when(i >= nb_ref[0])
    def _():
        y_ref[...] = jnp.zeros_like(y_ref)


def experts(xr, block_e, n_used, w_gu, b_gu, w_down, b_down, bm):
    n_rows, d = xr.shape
    ne, _, ff2 = w_gu.shape
    ff = ff2 // 2
    grid_spec = pltpu.PrefetchScalarGridSpec(
        num_scalar_prefetch=2,
        grid=(n_rows // bm,),
        in_specs=[pl.BlockSpec((bm, d), lambda i, be, nb: (i, 0)),
                  pl.BlockSpec((1, d, ff2), lambda i, be, nb: (be[i], 0, 0)),
                  pl.BlockSpec((1, 1, ff2), lambda i, be, nb: (be[i], 0, 0)),
                  pl.BlockSpec((1, ff, d), lambda i, be, nb: (be[i], 0, 0)),
                  pl.BlockSpec((1, 1, d), lambda i, be, nb: (be[i], 0, 0))],
        out_specs=pl.BlockSpec((bm, d), lambda i, be, nb: (i, 0)),
        scratch_shapes=[pltpu.VMEM((d, ff2), BF16), pltpu.VMEM((ff, d), BF16)],
    )
    return pl.pallas_call(
        functools.partial(_expert_kernel, ff=ff),
        grid_spec=grid_spec,
        out_shape=jax.ShapeDtypeStruct((n_rows, d), F32),
        compiler_params=_cparams(("arbitrary",)),
        name="moe_experts",
    )(block_e, n_used, xr, w_gu, b_gu.reshape(ne, 1, ff2), w_down, b_down.reshape(ne, 1, d))


def _combine_kernel(dest_ref, dest_next_ref, yr_hbm, g_ref, x1_ref, gate2_ref, gf_ref, out_ref, buf, sems,
                    *, tile, final_norm):
    i = pl.program_id(0)
    slot = i % 2
    n_copy = tile * TOP_K

    def gather(idx_ref, s):
        def start(t, c):
            for k in range(TOP_K):
                pltpu.make_async_copy(yr_hbm.at[pl.ds(idx_ref[t * TOP_K + k], 1)], buf.at[s, k, pl.ds(t, 1)],
                                      sems.at[s]).start(priority=k % 2)
            return c

        lax.fori_loop(0, tile, start, 0)

    @pl.when(i == 0)
    def _():
        gather(dest_ref, slot)

    @pl.when(i + 1 < pl.num_programs(0))
    def _():
        gather(dest_next_ref, 1 - slot)

    _wait_rows(yr_hbm, n_copy, sems.at[slot])

    y = None
    for k in range(TOP_K):
        term = g_ref[:, k:k + 1] * buf[slot, k]
        y = term if y is None else y + term
    x2 = x1_ref[...] + gate2_ref[...] * y
    if final_norm:
        ms = jnp.mean(x2 * x2, axis=-1, keepdims=True)
        x2 = x2 * lax.rsqrt(ms + RMS_EPS) * gf_ref[...]
    out_ref[...] = x2


def combine(yr, dest_flat, topg, x1, gate2, gf, tile, final_norm):
    n, d = x1.shape
    tok = pl.BlockSpec((tile, d), lambda i: (i, 0))
    steps = n // tile
    return pl.pallas_call(
        functools.partial(_combine_kernel, tile=tile, final_norm=final_norm),
        grid=(steps,),
        in_specs=[pl.BlockSpec((tile * TOP_K,), lambda i: (i,), memory_space=pltpu.SMEM),
                  pl.BlockSpec((tile * TOP_K,), lambda i: (jnp.minimum(i + 1, steps - 1),), memory_space=pltpu.SMEM),
                  pl.BlockSpec(memory_space=pl.ANY),
                  pl.BlockSpec((tile, topg.shape[1]), lambda i: (i, 0)),
                  tok, _row_spec(gate2.shape[0], tile, d), pl.BlockSpec((1, d), lambda i: (0, 0))],
        out_specs=tok,
        out_shape=jax.ShapeDtypeStruct(x1.shape, F32),
        scratch_shapes=[pltpu.VMEM((2, TOP_K, tile, d), F32), pltpu.SemaphoreType.DMA((2,))],
        compiler_params=_cparams(("arbitrary",)),
        name="moe_combine",
    )(dest_flat, dest_flat, yr, topg, x1, gate2, gf)


def moe_block(groups, norm_f_g, final_norm, w_gu, b_gu, w_down, b_down, bm, tile):
    d = groups[0][0].shape[1]
    ne = w_gu.shape[0]
    experts_iota = jnp.arange(ne, dtype=jnp.int32)
    counts = [grp[5][0, :ne].astype(jnp.int32) for grp in groups]
    total = sum(counts)
    padded = ((total + bm - 1) // bm) * bm
    pend = jnp.cumsum(padded)
    base = pend - padded
    dests = []
    for grp, cnt in zip(groups, counts):
        idx = grp[2][:, :TOP_K].astype(jnp.int32)
        onehot = (idx[..., None] == experts_iota).astype(jnp.int32)
        dests.append((jnp.sum(onehot * base, axis=-1) + grp[4][:, :TOP_K].astype(jnp.int32)).reshape(-1))
        base = base + cnt
    n_assign = sum(grp[0].shape[0] for grp in groups) * TOP_K
    n_rows = -(-n_assign // bm) * bm + ne * bm
    n_blocks = n_rows // bm
    starts = jnp.arange(n_blocks, dtype=jnp.int32) * bm
    block_e = jnp.minimum(jnp.sum((starts[:, None] >= pend[None, :]).astype(jnp.int32), axis=1), ne - 1)
    n_used = (pend[-1:] // bm).astype(jnp.int32)
    last_e = block_e[jnp.maximum(n_used[0] - 1, 0)]
    block_e = jnp.where(jnp.arange(n_blocks) < n_used[0], block_e, last_e).astype(jnp.int32)

    xr = jnp.zeros((n_rows, d), F32)
    for grp, dest in zip(groups, dests):
        xr = dispatch(grp[1], dest, xr, tile)
    yr = experts(xr, block_e, n_used, w_gu, b_gu, w_down, b_down, bm)
    return [combine(yr, dest, grp[3], grp[0], grp[6], norm_f_g.reshape(1, d), tile, final_norm)
            for grp, dest in zip(groups, dests)]


def _group_forward(x, mod, attend, h0_lanes, t_per_seq, lw, mats, rows):
    n, d = x.shape
    rows = min(rows, n)
    (norm1_g, norm2_g, w_in_b, w_out_b, w_glu_b, b_glu, d_vec, wr_hi, wr_lo, br) = lw
    kmat, smat, rmat, decay_re, decay_im = mats
    shift1, scale1, gate1, shift2, scale2, gate2 = [mod[:, j * d:(j + 1) * d] for j in range(6)]
    aw = (w_in_b.shape[1] - w_glu_b.shape[0]) // 3
    q_dtype = BF16 if h0_lanes is None else F32
    q, k, v, kb, vb, u = inproj(x, shift1, scale1, norm1_g, w_in_b, aw, q_dtype, rows)
    o_attn = attend(q, k, v, kb, vb)

    sw = u.shape[1]
    nb = sw // LANES
    u2 = u.reshape(n // SSM_CHUNK, SSM_CHUNK * sw)
    nc = u2.shape[0]
    crow = min(256, nc)
    if h0_lanes is None:
        s = ssm_state(u2, smat, crow)
        hs, hend = ssm_scan(s, decay_re, decay_im, crow, nb)
    else:
        assert t_per_seq == SSM_CHUNK
        hs = h0_lanes
        hend = ssm_state(u2, smat, crow, h0_lanes, decay_re, decay_im)
    y_ssm = ssm_out(u2, hs, kmat, rmat, d_vec, w_glu_b, b_glu, crow).reshape(n, sw)

    routed = postmix(x, o_attn, y_ssm, w_out_b, gate1, norm2_g, scale2, shift2, wr_hi, wr_lo, br, rows)
    return tuple(routed) + (gate2,), k, v, hend


def kernel(x_prompt, x_sample, c_prompt, c_sample, cache_k, cache_v, state_ssm_re, state_ssm_im, page_table, norm1_g, norm2_g, w_ada, b_ada, w_in, w_out, sb_bias, lam_re, lam_im, log_dt, ssm_b_re, ssm_b_im, ssm_c_re, ssm_c_im, ssm_d, w_glu, b_glu, w_router, b_router, w_gu, b_gu, w_down, b_down, norm_f_g):
    depth = w_in.shape[0]
    bp, tp, d = x_prompt.shape
    bs, ts, _ = x_sample.shape
    assert bp == 1, "the prompt group is handled as one long sequence"
    n_heads = cache_k.shape[3]
    aw = n_heads * HEAD_DIM
    g, p = lam_re.shape[1:]
    nb = g // GROUP_BLOCK
    ne = w_router.shape[-1]
    n_pool, page = cache_k.shape[1:3]
    rows = 512

    xp = x_prompt.reshape(bp * tp, d)
    xs = x_sample.reshape(bs * ts, d)
    n_c = bp + bs
    c_all = jnp.concatenate([c_prompt, c_sample, jnp.zeros((-n_c % SUBLANES, d), F32)], axis=0)

    outs = {name: [] for name in ("kp", "vp", "rp", "ip", "ks", "vs", "rs", "is")}
    for l in range(depth):
        mod = ada_mod(c_all, w_ada[l], b_ada[l])
        mod_p = mod[:bp]
        mod_s = jnp.repeat(mod[bp:n_c], ts, axis=0)
        mats = _ssm_mats(lam_re[l], lam_im[l], log_dt[l], ssm_b_re[l], ssm_b_im[l], ssm_c_re[l], ssm_c_im[l])
        wr = jnp.pad(w_router[l], ((0, 0), (0, ROUTER_PAD - ne)))
        wr_hi = wr.astype(BF16)
        wr_lo = (wr - wr_hi.astype(F32)).astype(BF16)
        br = jnp.concatenate([b_router[l], jnp.full((ROUTER_PAD - ne,), NEG_BIG, F32)]).reshape(1, ROUTER_PAD)
        lw = (norm1_g[l].reshape(1, d), norm2_g[l].reshape(1, d), w_in[l].astype(BF16), w_out[l].astype(BF16),
              w_glu[l].astype(BF16), b_glu[l].reshape(1, -1), ssm_d[l].reshape(1, -1), wr_hi, wr_lo, br)
        bias = sb_bias[l]

        def attend_p(q, k, v, kb, vb):
            return attn_prompt_pairs(q, kb, vb, bias, nsb=min(8, tp // KEY_BLOCK))

        ck = cache_k[l].transpose(0, 2, 3, 1).reshape(n_pool, aw, page)
        cv = cache_v[l].transpose(0, 2, 3, 1).reshape(n_pool, aw, page)

        def attend_s(q, k, v, kb, vb):
            return attn_sample(q, k, v, ck, cv, page_table, bias, ts, group=math.gcd(16, page_table.shape[1]))

        h0 = _state_to_lanes(state_ssm_re[l], state_ssm_im[l], nb)
        routed_p, kp, vp, hp = _group_forward(xp, mod_p, attend_p, None, tp, lw, mats, rows)
        routed_s, ks, vs, hs = _group_forward(xs, mod_s, attend_s, h0, ts, lw, mats, rows)
        xp, xs = moe_block([routed_p, routed_s], norm_f_g, l == depth - 1, w_gu[l], b_gu[l], w_down[l], b_down[l],
                           bm=EXPERT_ROWS, tile=512)
        rp, ip = _lanes_to_state(hp, g, p)
        rs, is_ = _lanes_to_state(hs, g, p)
        outs["kp"].append(kp.reshape(bp, tp, n_heads, HEAD_DIM))
        outs["vp"].append(vp.reshape(bp, tp, n_heads, HEAD_DIM))
        outs["rp"].append(rp)
        outs["ip"].append(ip)
        outs["ks"].append(ks.reshape(bs, ts, n_heads, HEAD_DIM))
        outs["vs"].append(vs.reshape(bs, ts, n_heads, HEAD_DIM))
        outs["rs"].append(rs)
        outs["is"].append(is_)
    st = lambda name: jnp.stack(outs[name])
    return (xp.reshape(bp, tp, d), xs.reshape(bs, ts, d), st("kp"), st("vp"), st("rp"), st("ip"),
            st("ks"), st("vs"), st("rs"), st("is"))
```
